```python
import math
import jax, jax.numpy as jnp
from jax import lax
import numpy as np

D_MODEL = 1024
BATCH = 8
SEQ = 2048
DEPTH = 2
DEC_BATCH = 128
DEC_SEQ = 8
PAST_LEN = 8192
PAGE_SIZE = 128

MLA_HEADS = 8
Q_LORA = 384
KV_LORA = 256
QK_NOPE = 64
QK_ROPE = 32
V_HEAD = 64
ROPE_THETA = 10000.0
MOBA_HEADS = 8
MOBA_KV_HEADS = 2
MOBA_GROUP = MOBA_HEADS // MOBA_KV_HEADS
MOBA_HEAD_DIM = 64
MOBA_BLOCK = 256
MOBA_TOPK = 3
N_BRANCHES = 2
BRANCH_WIDTH = MLA_HEADS * V_HEAD
FFN_HIDDEN = -(-8 * D_MODEL // (3 * 256)) * 256
RMS_EPS = 1e-6
ATTN_Q_BLOCK = 128
MOBA_ROW_BLOCK = 256
IN_SIZES = (Q_LORA, KV_LORA, QK_ROPE, MOBA_HEADS * MOBA_HEAD_DIM,
            MOBA_KV_HEADS * MOBA_HEAD_DIM, MOBA_KV_HEADS * MOBA_HEAD_DIM, N_BRANCHES * D_MODEL)
D_IN = sum(IN_SIZES)
IN_SPLITS = [int(v) for v in np.cumsum(IN_SIZES)[:-1]]

kernel_name = 'mla_moba_gated_hybrid_step'


def rms_norm(x, g):
    xf = x.astype(jnp.float32)
    y = xf * lax.rsqrt(jnp.mean(xf * xf, axis=-1, keepdims=True) + RMS_EPS)
    return (y * g.astype(jnp.float32)).astype(x.dtype)


def apply_rope(x, pos):
    half = x.shape[-1] // 2
    inv = ROPE_THETA ** (-jnp.arange(half, dtype=jnp.float32) / half)
    ang = pos.astype(jnp.float32)[:, None] * inv[None, :]
    ang = ang.reshape(ang.shape[:1] + (1,) * (x.ndim - 3) + (half,))
    cos = jnp.cos(ang).astype(x.dtype)
    sin = jnp.sin(ang).astype(x.dtype)
    x1, x2 = x[..., :half], x[..., half:]
    return jnp.concatenate([x1 * cos - x2 * sin, x2 * cos + x1 * sin], axis=-1)


def alibi_slopes(n_heads):
    return 2.0 ** (-8.0 * jnp.arange(1, n_heads + 1, dtype=jnp.float32) / n_heads)


def mla_attend(q_lat, q_rope, ckv, krope, q_pos):
    B, Sq = q_lat.shape[:2]
    L = ckv.shape[1]
    qb = min(ATTN_Q_BLOCK, Sq)
    nq = -(-Sq // qb)
    pad = nq * qb - Sq
    scale = (QK_NOPE + QK_ROPE) ** -0.5
    k_pos = jnp.arange(L, dtype=jnp.int32)

    def to_blocks(a):
        a = jnp.pad(a, ((0, 0), (0, pad)) + ((0, 0),) * (a.ndim - 2))
        return jnp.moveaxis(a.reshape((B, nq, qb) + a.shape[2:]), 1, 0)

    pos_b = jnp.pad(q_pos.astype(jnp.int32), (0, pad)).reshape(nq, qb)

    def one_block(args):
        ql, qr, t = args
        s = (jnp.einsum('bqhc,bkc->bhqk', ql, ckv)
             + jnp.einsum('bqhr,bkr->bhqk', qr, krope)).astype(jnp.float32) * scale
        s = jnp.where(k_pos[None, :] <= t[:, None], s, -jnp.inf)
        p = jax.nn.softmax(s, axis=-1).astype(ckv.dtype)
        return jnp.einsum('bhqk,bkc->bqhc', p, ckv)

    o = lax.map(one_block, (to_blocks(q_lat), to_blocks(q_rope), pos_b))
    return jnp.moveaxis(o, 0, 1).reshape(B, nq * qb, MLA_HEADS, KV_LORA)[:, :Sq]


def moba_attend(q, k_all, v_all, q_pos):
    B, Sq = q.shape[:2]
    L = k_all.shape[1]
    nb = -(-L // MOBA_BLOCK)
    kpad = ((0, 0), (0, nb * MOBA_BLOCK - L), (0, 0), (0, 0))
    kb = jnp.pad(k_all, kpad).reshape(B, nb, MOBA_BLOCK, MOBA_KV_HEADS, MOBA_HEAD_DIM)
    vb = jnp.pad(v_all, kpad).reshape(B, nb, MOBA_BLOCK, MOBA_KV_HEADS, MOBA_HEAD_DIM)
    k_mean = jnp.mean(kb.astype(jnp.float32), axis=2)
    n_sel = min(MOBA_TOPK, nb)
    n_rows = B * Sq
    rb = min(MOBA_ROW_BLOCK, n_rows)
    nc = -(-n_rows // rb)
    rpad = nc * rb - n_rows
    q_rows = jnp.pad(q.reshape(n_rows, MOBA_HEADS, MOBA_HEAD_DIM),
                     ((0, rpad), (0, 0), (0, 0))).reshape(nc, rb, MOBA_HEADS, MOBA_HEAD_DIM)
    b_rows = jnp.pad(jnp.repeat(jnp.arange(B, dtype=jnp.int32), Sq), (0, rpad)).reshape(nc, rb)
    t_rows = jnp.pad(jnp.broadcast_to(q_pos.astype(jnp.int32)[None, :], (B, Sq)).reshape(-1),
                     (0, rpad)).reshape(nc, rb)
    slopes = alibi_slopes(MOBA_HEADS)
    kv_head = jnp.arange(MOBA_HEADS, dtype=jnp.int32) // MOBA_GROUP
    offs = jnp.arange(MOBA_BLOCK, dtype=jnp.int32)
    blk_ids = jnp.arange(nb, dtype=jnp.int32)
    scale = MOBA_HEAD_DIM ** -0.5

    def one_group(args):
        qc, bc, tc = args
        own = tc // MOBA_BLOCK
        gate = jnp.einsum('rkgd,rnkd->rkgn',
                          qc.astype(jnp.float32).reshape(rb, MOBA_KV_HEADS, MOBA_GROUP, MOBA_HEAD_DIM),
                          k_mean[bc]).reshape(rb, MOBA_HEADS, nb)
        fully_past = blk_ids[None, None, :] < own[:, None, None]
        gate = jnp.where(fully_past, gate, -jnp.inf)
        _, top = lax.top_k(gate, n_sel)
        sel_ok = top < own[:, None, None]
        blk = jnp.concatenate([top.astype(jnp.int32),
                               jnp.broadcast_to(own[:, None, None], (rb, MOBA_HEADS, 1))], axis=-1)
        slot_ok = jnp.concatenate([sel_ok, jnp.ones((rb, MOBA_HEADS, 1), dtype=bool)], axis=-1)
        ks = kb[bc[:, None, None], blk, :, kv_head[None, :, None], :]
        vs = vb[bc[:, None, None], blk, :, kv_head[None, :, None], :]
        kpos = blk[..., None] * MOBA_BLOCK + offs
        dist = tc[:, None, None, None] - kpos
        s = (jnp.einsum('rhd,rhsnd->rhsn', qc, ks).astype(jnp.float32) * scale
             - slopes[None, :, None, None] * dist.astype(jnp.float32))
        s = jnp.where(slot_ok[..., None] & (dist >= 0), s, -jnp.inf)
        p = jax.nn.softmax(s.reshape(rb, MOBA_HEADS, -1), axis=-1).astype(vs.dtype)
        return jnp.einsum('rhn,rhnd->rhd', p, vs.reshape(rb, MOBA_HEADS, -1, MOBA_HEAD_DIM))

    o = lax.map(one_group, (q_rows, b_rows, t_rows))
    return o.reshape(nc * rb, MOBA_HEADS, MOBA_HEAD_DIM)[:n_rows].reshape(B, Sq, MOBA_HEADS, MOBA_HEAD_DIM)


def trunk_layer(x, pos, past, w_in, g_q, g_kv, w_uq, w_uk, w_uv, w_branch, w_o,
                g_attn, g_ffn, w_gu, w_down):
    B, S, _ = x.shape
    hn = rms_norm(x, g_attn)
    proj = hn @ w_in
    c_q, c_kv, k_r, m_q, m_k, m_v, gate_logits = jnp.split(proj, IN_SPLITS, axis=-1)
    c_q = rms_norm(c_q, g_q)
    c_kv = rms_norm(c_kv, g_kv)
    q = (c_q @ w_uq).reshape(B, S, MLA_HEADS, QK_NOPE + QK_ROPE)
    q_nope = q[..., :QK_NOPE]
    q_rope = apply_rope(q[..., QK_NOPE:], pos)
    k_r = apply_rope(k_r, pos)
    q_lat = jnp.einsum('bshn,chn->bshc', q_nope, w_uk)
    m_q = m_q.reshape(B, S, MOBA_HEADS, MOBA_HEAD_DIM)
    m_k = m_k.reshape(B, S, MOBA_KV_HEADS, MOBA_HEAD_DIM)
    m_v = m_v.reshape(B, S, MOBA_KV_HEADS, MOBA_HEAD_DIM)
    if past is None:
        ckv_all, kr_all, k_all, v_all = c_kv, k_r, m_k, m_v
    else:
        ckv_all = jnp.concatenate([past[0], c_kv], axis=1)
        kr_all = jnp.concatenate([past[1], k_r], axis=1)
        k_all = jnp.concatenate([past[2], m_k], axis=1)
        v_all = jnp.concatenate([past[3], m_v], axis=1)
    o_lat = mla_attend(q_lat, q_rope, ckv_all, kr_all, pos)
    o_a = jnp.einsum('bshc,chv->bshv', o_lat, w_uv).reshape(B, S, BRANCH_WIDTH)
    o_b = moba_attend(m_q, k_all, v_all, pos).reshape(B, S, BRANCH_WIDTH)
    br = jnp.einsum('bsnw,nwd->bsnd', jnp.stack([o_a, o_b], axis=2), w_branch)
    gates = jax.nn.sigmoid(gate_logits.reshape(B, S, N_BRANCHES, D_MODEL))
    h = x + jnp.sum(gates * br, axis=2) @ w_o
    a, u = jnp.split(rms_norm(h, g_ffn) @ w_gu, 2, axis=-1)
    h = h + (jax.nn.silu(a) * u) @ w_down
    return h, (c_kv, k_r, m_k, m_v)


def setup_inputs(seed: int = 0) -> dict:
    key = jax.random.key(seed)
    ks = jax.random.split(key, 24)
    f32 = jnp.float32
    n_pages = PAST_LEN // PAGE_SIZE
    n_used = DEC_BATCH * n_pages
    n_phys = n_used + max(1, n_used // 4)
    page_table = jax.random.permutation(ks[0], n_phys)[:n_used].reshape(DEC_BATCH, n_pages).astype(jnp.int32)

    def nrm(k, shape, fan_in=None):
        w = jax.random.normal(k, shape, dtype=f32)
        return w * (fan_in ** -0.5) if fan_in is not None else w

    def gain(k, shape):
        return 1.0 + 0.05 * jax.random.normal(k, shape, dtype=f32)

    return {
        'x_prompt': nrm(ks[1], (BATCH, SEQ, D_MODEL)),
        'x_sample': nrm(ks[2], (DEC_BATCH, DEC_SEQ, D_MODEL)),
        'cache_ckv': nrm(ks[3], (DEPTH, n_phys, PAGE_SIZE, KV_LORA)),
        'cache_krope': nrm(ks[4], (DEPTH, n_phys, PAGE_SIZE, QK_ROPE)),
        'cache_k': nrm(ks[5], (DEPTH, n_phys, PAGE_SIZE, MOBA_KV_HEADS, MOBA_HEAD_DIM)),
        'cache_v': nrm(ks[6], (DEPTH, n_phys, PAGE_SIZE, MOBA_KV_HEADS, MOBA_HEAD_DIM)),
        'page_table': page_table,
        'w_in': nrm(ks[7], (DEPTH, D_MODEL, D_IN), D_MODEL),
        'g_q': gain(ks[8], (DEPTH, Q_LORA)),
        'g_kv': gain(ks[9], (DEPTH, KV_LORA)),
        'w_uq': nrm(ks[10], (DEPTH, Q_LORA, MLA_HEADS * (QK_NOPE + QK_ROPE)), Q_LORA),
        'w_uk': nrm(ks[11], (DEPTH, KV_LORA, MLA_HEADS, QK_NOPE), KV_LORA),
        'w_uv': nrm(ks[12], (DEPTH, KV_LORA, MLA_HEADS, V_HEAD), KV_LORA),
        'w_branch': nrm(ks[13], (DEPTH, N_BRANCHES, BRANCH_WIDTH, D_MODEL), BRANCH_WIDTH),
        'w_o': nrm(ks[14], (DEPTH, D_MODEL, D_MODEL), D_MODEL),
        'g_attn': gain(ks[15], (DEPTH, D_MODEL)),
        'g_ffn': gain(ks[16], (DEPTH, D_MODEL)),
        'w_gu': nrm(ks[17], (DEPTH, D_MODEL, 2 * FFN_HIDDEN), D_MODEL),
        'w_down': nrm(ks[18], (DEPTH, FFN_HIDDEN, D_MODEL), FFN_HIDDEN),
        'g_final': gain(ks[19], (D_MODEL,)),
    }


def reference(x_prompt, x_sample, cache_ckv, cache_krope, cache_k, cache_v, page_table,
              w_in, g_q, g_kv, w_uq, w_uk, w_uv, w_branch, w_o, g_attn, g_ffn, w_gu, w_down, g_final):
    dec_b = x_sample.shape[0]
    past_len = page_table.shape[1] * PAGE_SIZE
    pos_p = jnp.arange(x_prompt.shape[1], dtype=jnp.int32)
    pos_s = past_len + jnp.arange(x_sample.shape[1], dtype=jnp.int32)
    hp, hs = x_prompt, x_sample
    rows_p, rows_s = [], []
    for l in range(DEPTH):
        wl = (w_in[l], g_q[l], g_kv[l], w_uq[l], w_uk[l], w_uv[l], w_branch[l], w_o[l],
              g_attn[l], g_ffn[l], w_gu[l], w_down[l])
        hp, rp = trunk_layer(hp, pos_p, None, *wl)
        past = (cache_ckv[l, page_table].reshape(dec_b, past_len, KV_LORA),
                cache_krope[l, page_table].reshape(dec_b, past_len, QK_ROPE),
                cache_k[l, page_table].reshape(dec_b, past_len, MOBA_KV_HEADS, MOBA_HEAD_DIM),
                cache_v[l, page_table].reshape(dec_b, past_len, MOBA_KV_HEADS, MOBA_HEAD_DIM))
        hs, rs = trunk_layer(hs, pos_s, past, *wl)
        rows_p.append(rp)
        rows_s.append(rs)
    y_prompt = rms_norm(hp, g_final)
    y_sample = rms_norm(hs, g_final)
    p_ckv = jnp.stack([r[0] for r in rows_p])
    p_krope = jnp.stack([r[1] for r in rows_p])
    p_k = jnp.stack([r[2] for r in rows_p])
    p_v = jnp.stack([r[3] for r in rows_p])
    s_ckv = jnp.stack([r[0] for r in rows_s])
    s_krope = jnp.stack([r[1] for r in rows_s])
    s_k = jnp.stack([r[2] for r in rows_s])
    s_v = jnp.stack([r[3] for r in rows_s])
    return (y_prompt, y_sample, p_ckv, p_krope, p_k, p_v, s_ckv, s_krope, s_k, s_v)
```

```python
import functools

import jax
import jax.numpy as jnp
from jax import lax
from jax.experimental import pallas as pl
from jax.experimental.pallas import tpu as pltpu

F32 = jnp.float32
BF16 = jnp.bfloat16

D_MODEL = 1024
PAGE_SIZE = 128
MLA_HEADS = 8
Q_LORA = 384
KV_LORA = 256
QK_NOPE = 64
QK_ROPE = 32
V_HEAD = 64
ROPE_THETA = 10000.0
MOBA_HEADS = 8
MOBA_KV_HEADS = 2
MOBA_GROUP = MOBA_HEADS // MOBA_KV_HEADS
MOBA_HEAD_DIM = 64
MOBA_BLOCK = 256
MOBA_TOPK = 3
BRANCH_WIDTH = 512
FFN_HIDDEN = 2816
RMS_EPS = 1e-6

LANES = 128
QK_PAD = KV_LORA + LANES
MOBA_KV_WIDTH = MOBA_KV_HEADS * MOBA_HEAD_DIM
HALF_ROPE = QK_ROPE // 2

_OFF_CQ = 0
_OFF_CKV = _OFF_CQ + Q_LORA
_OFF_MQ = _OFF_CKV + KV_LORA
_OFF_MK = _OFF_MQ + MOBA_HEADS * LANES
_OFF_MV = _OFF_MK + MOBA_KV_WIDTH
_OFF_GATE = _OFF_MV + MOBA_KV_WIDTH
_OFF_KR = _OFF_GATE + 2 * D_MODEL
_D_IN_PERM = _OFF_KR + QK_ROPE

VMEM_LIMIT = 56 * 1024 * 1024
PAGES_PER_STEP = 8
NEG_INF = float("-inf")


def _rms(x, g):
    return x * lax.rsqrt(jnp.mean(x * x, axis=-1, keepdims=True) + RMS_EPS) * g


def _dot(a, b):
    return jnp.dot(a, b, preferred_element_type=F32)


def _dot_nt(a, b):
    return lax.dot_general(a, b, (((1,), (1,)), ((), ())), preferred_element_type=F32)


def _const_spec(shape):
    return pl.BlockSpec(shape, lambda *_: (0,) * len(shape), pipeline_mode=pl.Buffered(1))


def _proj_kernel(x_ref, cos_ref, sin_ref, g_attn_ref, w_in_ref, g_q_ref, g_kv_ref,
                 w_uqn_ref, w_uqr_ref, w_uk_ref, *out_refs, prompt):
    if prompt:
        (ckv_ref, kr_ref, mk_ref, mv_ref, gl_ref, qf_ref, qm_ref,
         kf_ref, mkv_ref, kmean_ref) = out_refs
    else:
        ckv_ref, kr_ref, mk_ref, mv_ref, gl_ref, qf_ref, qm_ref = out_refs
    tm = x_ref.shape[0]
    hn = _rms(x_ref[...], g_attn_ref[...]).astype(BF16)

    def seg(lo, hi):
        return _dot(hn, w_in_ref[:, lo:hi])

    cos = cos_ref[...]
    sin = sin_ref[...]

    ckv = _rms(seg(_OFF_CKV, _OFF_MQ), g_kv_ref[...])
    ckv_ref[...] = ckv
    kr = seg(_OFF_KR, _D_IN_PERM)
    kr_swapped = jnp.concatenate([kr[:, HALF_ROPE:], kr[:, :HALF_ROPE]], axis=1)
    lane32 = lax.broadcasted_iota(jnp.int32, (tm, QK_ROPE), 1)
    sin_signed = jnp.where(lane32 < HALF_ROPE, -sin[:, :QK_ROPE], sin[:, :QK_ROPE])
    kr = kr * cos[:, :QK_ROPE] + kr_swapped * sin_signed
    kr_ref[...] = kr

    mk = seg(_OFF_MK, _OFF_MV)
    mv = seg(_OFF_MV, _OFF_GATE)
    mk_ref[...] = mk
    mv_ref[...] = mv
    gl_ref[...] = seg(_OFF_GATE, _OFF_KR)

    mq = seg(_OFF_MQ, _OFF_MK)
    for h in range(MOBA_HEADS):
        qm_ref[h] = mq[:, h * LANES:(h + 1) * LANES].astype(qm_ref.dtype)

    cq = _rms(seg(_OFF_CQ, _OFF_CKV), g_q_ref[...]).astype(BF16)
    q_nope = _dot(cq, w_uqn_ref[...]).astype(BF16)
    q_rot = _dot(cq, w_uqr_ref[...])
    r1, r2 = q_rot[:, :LANES], q_rot[:, LANES:]
    o1 = r1 * cos - r2 * sin
    o2 = r2 * cos + r1 * sin
    lane = lax.broadcasted_iota(jnp.int32, (tm, LANES), 1)
    for h in range(MLA_HEADS):
        q_lat = _dot(q_nope[:, h * LANES:(h + 1) * LANES], w_uk_ref[h])
        shift_a = (LANES - HALF_ROPE * h) % LANES
        shift_b = (HALF_ROPE - HALF_ROPE * h) % LANES
        a = o1 if shift_a == 0 else pltpu.roll(o1, shift_a, axis=1)
        b = o2 if shift_b == 0 else pltpu.roll(o2, shift_b, axis=1)
        rope = jnp.where(lane < HALF_ROPE, a, jnp.where(lane < QK_ROPE, b, 0.0))
        qf_ref[h, :, :KV_LORA] = q_lat.astype(qf_ref.dtype)
        qf_ref[h, :, KV_LORA:] = rope.astype(qf_ref.dtype)

    if prompt:
        kf_ref[:, :KV_LORA] = ckv.astype(BF16)
        kr_pad = jnp.concatenate([kr, jnp.zeros((tm, LANES - QK_ROPE), F32)], axis=1)
        kf_ref[:, KV_LORA:] = kr_pad.astype(BF16)
        mkv_ref[:, :MOBA_KV_WIDTH] = mk.astype(BF16)
        mkv_ref[:, MOBA_KV_WIDTH:] = mv.astype(BF16)
        for j in range(tm // MOBA_BLOCK):
            kmean_ref[j] = jnp.mean(mk[j * MOBA_BLOCK:(j + 1) * MOBA_BLOCK], axis=0, keepdims=True)


def _proj(x, cos_tab, sin_tab, w, *, prompt, tm):
    n = x.shape[0]
    n_tab = cos_tab.shape[0] // tm
    grid = (n // tm,)
    row = lambda i: (i, 0)
    q_dtype = BF16 if prompt else F32
    out_shape = [
        jax.ShapeDtypeStruct((n, KV_LORA), F32),
        jax.ShapeDtypeStruct((n, QK_ROPE), F32),
        jax.ShapeDtypeStruct((n, MOBA_KV_WIDTH), F32),
        jax.ShapeDtypeStruct((n, MOBA_KV_WIDTH), F32),
        jax.ShapeDtypeStruct((n, 2 * D_MODEL), F32),
        jax.ShapeDtypeStruct((MLA_HEADS, n, QK_PAD), q_dtype),
        jax.ShapeDtypeStruct((MOBA_HEADS, n, MOBA_KV_WIDTH), F32),
    ]
    out_specs = [
        pl.BlockSpec((tm, KV_LORA), row),
        pl.BlockSpec((tm, QK_ROPE), row),
        pl.BlockSpec((tm, MOBA_KV_WIDTH), row),
        pl.BlockSpec((tm, MOBA_KV_WIDTH), row),
        pl.BlockSpec((tm, 2 * D_MODEL), row),
        pl.BlockSpec((MLA_HEADS, tm, QK_PAD), lambda i: (0, i, 0)),
        pl.BlockSpec((MOBA_HEADS, tm, MOBA_KV_WIDTH), lambda i: (0, i, 0)),
    ]
    if prompt:
        nblk = tm // MOBA_BLOCK
        out_shape += [
            jax.ShapeDtypeStruct((n, QK_PAD), BF16),
            jax.ShapeDtypeStruct((n, 2 * MOBA_KV_WIDTH), BF16),
            jax.ShapeDtypeStruct((n // MOBA_BLOCK, 1, MOBA_KV_WIDTH), F32),
        ]
        out_specs += [
            pl.BlockSpec((tm, QK_PAD), row),
            pl.BlockSpec((tm, 2 * MOBA_KV_WIDTH), row),
            pl.BlockSpec((nblk, 1, MOBA_KV_WIDTH), lambda i: (i, 0, 0)),
        ]
    tab = lambda i: (i % n_tab, 0)
    in_specs = [
        pl.BlockSpec((tm, D_MODEL), row),
        pl.BlockSpec((tm, LANES), tab),
        pl.BlockSpec((tm, LANES), tab),
        _const_spec((1, D_MODEL)),
        _const_spec((D_MODEL, _D_IN_PERM)),
        _const_spec((1, Q_LORA)),
        _const_spec((1, KV_LORA)),
        _const_spec((Q_LORA, MLA_HEADS * LANES)),
        _const_spec((Q_LORA, 2 * LANES)),
        _const_spec((MLA_HEADS, LANES, KV_LORA)),
    ]
    return pl.pallas_call(
        functools.partial(_proj_kernel, prompt=prompt),
        grid=grid, in_specs=in_specs, out_specs=out_specs, out_shape=out_shape,
        compiler_params=pltpu.CompilerParams(
            dimension_semantics=("arbitrary",), vmem_limit_bytes=VMEM_LIMIT),
        name="proj_prompt" if prompt else "proj_sample",
    )(x, cos_tab, sin_tab, w["g_attn"], w["w_in"], w["g_q"], w["g_kv"],
      w["w_uqn"], w["w_uqr"], w["w_uk"])


def _online_softmax_step(s, v, m_ref, l_ref, acc_ref):
    m_old = m_ref[...]
    m_new = jnp.maximum(m_old, jnp.max(s, axis=1, keepdims=True))
    alpha = jnp.exp(m_old - m_new)
    p = jnp.exp(s - m_new)
    l_ref[...] = alpha * l_ref[...] + jnp.sum(p, axis=1, keepdims=True)
    acc_ref[...] = alpha * acc_ref[...] + _dot(p.astype(BF16), v)
    m_ref[...] = m_new


def _mla_prompt_kernel(q_ref, k_ref, wuv_ref, o_ref, m_ref, l_ref, acc_ref, *, tq, tk, scale):
    qi = pl.program_id(1)
    rows = MLA_HEADS * tq
    q = q_ref[...].reshape(rows, QK_PAD)
    m_ref[...] = jnp.full(m_ref.shape, NEG_INF, F32)
    l_ref[...] = jnp.zeros(l_ref.shape, F32)
    acc_ref[...] = jnp.zeros(acc_ref.shape, F32)
    last = (qi * tq + tq - 1) // tk

    def chunk(kc, masked):
        k = k_ref[pl.ds(pl.multiple_of(kc * tk, tk), tk), :]
        s = _dot_nt(q, k) * scale
        if masked:
            q_pos = qi * tq + lax.broadcasted_iota(jnp.int32, (MLA_HEADS, tq, tk), 1).reshape(rows, tk)
            k_pos = kc * tk + lax.broadcasted_iota(jnp.int32, (rows, tk), 1)
            s = jnp.where(k_pos <= q_pos, s, NEG_INF)
        _online_softmax_step(s, k[:, :KV_LORA], m_ref, l_ref, acc_ref)

    def body(kc, carry):
        chunk(kc, False)
        return carry

    lax.fori_loop(0, last, body, 0)
    chunk(last, True)

    o_lat = (acc_ref[...] / l_ref[...]).astype(BF16)
    o = _dot(o_lat[:tq], wuv_ref[0])
    for h in range(1, MLA_HEADS):
        o = o + _dot(o_lat[h * tq:(h + 1) * tq], wuv_ref[h])
    o_ref[...] = o.astype(o_ref.dtype)


def _mla_prompt(qf, kf, w_uv, *, batch, seq, tq, tk):
    nq = seq // tq
    rows = MLA_HEADS * tq
    qf4 = qf.reshape(MLA_HEADS, batch, seq, QK_PAD)
    kf3 = kf.reshape(batch, seq, QK_PAD)
    scale = (QK_NOPE + QK_ROPE) ** -0.5
    return pl.pallas_call(
        functools.partial(_mla_prompt_kernel, tq=tq, tk=tk, scale=scale),
        grid=(batch, nq),
        in_specs=[
            pl.BlockSpec((MLA_HEADS, None, tq, QK_PAD), lambda b, i: (0, b, i, 0)),
            pl.BlockSpec((None, seq, QK_PAD), lambda b, i: (b, 0, 0)),
            _const_spec((MLA_HEADS, KV_LORA, BRANCH_WIDTH)),
        ],
        out_specs=pl.BlockSpec((tq, BRANCH_WIDTH), lambda b, i: (b * nq + i, 0)),
        out_shape=jax.ShapeDtypeStruct((batch * seq, BRANCH_WIDTH), BF16),
        scratch_shapes=[pltpu.VMEM((rows, 1), F32), pltpu.VMEM((rows, 1), F32),
                        pltpu.VMEM((rows, KV_LORA), F32)],
        compiler_params=pltpu.CompilerParams(
            dimension_semantics=("arbitrary", "arbitrary"), vmem_limit_bytes=VMEM_LIMIT),
        name="mla_prompt",
    )(qf4, kf3, w_uv)


def _gate_scores(qf, kmean):
    q_hi = qf.astype(BF16)
    q_lo = (qf - q_hi.astype(F32)).astype(BF16)
    k_hi = kmean.astype(BF16)
    k_lo = (kmean - k_hi.astype(F32)).astype(BF16)
    return _dot_nt(q_hi, k_hi) + (_dot_nt(q_hi, k_lo) + _dot_nt(q_lo, k_hi))


def _topk_lanes(gate, valid, n_sel):
    nb = gate.shape[1]
    lane = lax.broadcasted_iota(jnp.int32, gate.shape, 1)
    g = jnp.where(valid, gate, NEG_INF)
    picked = jnp.zeros(gate.shape, jnp.bool_)
    for _ in range(n_sel):
        cur = jnp.where(picked, NEG_INF, g)
        best = jnp.max(cur, axis=1, keepdims=True)
        cand = jnp.logical_and(cur == best, jnp.logical_not(picked))
        idx = jnp.min(jnp.where(cand, lane, nb), axis=1, keepdims=True)
        picked = jnp.logical_or(picked, lane == idx)
    return jnp.logical_and(picked, valid)


def _moba_prompt_kernel(q_ref, kv_ref, kmean_ref, slope_ref, o_ref, m_ref, l_ref, acc_ref, *, nb):
    tq = MOBA_BLOCK
    qi = pl.program_id(1)
    rows = MOBA_HEADS * tq
    scale = MOBA_HEAD_DIM ** -0.5
    qf = q_ref[...].reshape(rows, MOBA_KV_WIDTH)
    q = qf.astype(BF16)
    slope = slope_ref[...]
    t_q = lax.broadcasted_iota(jnp.int32, (MOBA_HEADS, tq, 1), 1).reshape(rows, 1)
    off_k = lax.broadcasted_iota(jnp.int32, (1, tq), 1)

    gate = _gate_scores(qf, kmean_ref[...].reshape(nb, MOBA_KV_WIDTH))
    blk = lax.broadcasted_iota(jnp.int32, (rows, nb), 1)
    sel = _topk_lanes(gate, blk < qi, min(MOBA_TOPK, nb))
    sel_bias = jnp.where(sel, 0.0, NEG_INF)

    kv = kv_ref[pl.ds(pl.multiple_of(qi * tq, tq), tq), :]
    dist = t_q - off_k
    s = _dot_nt(q, kv[:, :MOBA_KV_WIDTH]) * scale - slope * dist.astype(F32)
    s = jnp.where(dist >= 0, s, NEG_INF)
    m0 = jnp.max(s, axis=1, keepdims=True)
    p = jnp.exp(s - m0)
    m_ref[...] = m0
    l_ref[...] = jnp.sum(p, axis=1, keepdims=True)
    acc_ref[...] = _dot(p.astype(BF16), kv[:, MOBA_KV_WIDTH:])

    for j in range(nb - 1):
        @pl.when(j < qi)
        def _():
            kvj = kv_ref[j * tq:(j + 1) * tq, :]
            dist_j = ((qi - j) * tq + t_q - off_k).astype(F32)
            sj = _dot_nt(q, kvj[:, :MOBA_KV_WIDTH]) * scale - slope * dist_j + sel_bias[:, j:j + 1]
            _online_softmax_step(sj, kvj[:, MOBA_KV_WIDTH:], m_ref, l_ref, acc_ref)

    o = acc_ref[...] / l_ref[...]
    pieces = []
    for h in range(MOBA_HEADS):
        g = h // MOBA_GROUP
        pieces.append(o[h * tq:(h + 1) * tq, g * MOBA_HEAD_DIM:(g + 1) * MOBA_HEAD_DIM])
    o_ref[...] = jnp.concatenate(pieces, axis=1).astype(o_ref.dtype)


def _moba_prompt(qm, mkv, kmean, slope_rows, *, batch, seq):
    tq = MOBA_BLOCK
    nb = seq // tq
    rows = MOBA_HEADS * tq
    qm4 = qm.reshape(MOBA_HEADS, batch, seq, MOBA_KV_WIDTH)
    mkv3 = mkv.reshape(batch, seq, 2 * MOBA_KV_WIDTH)
    kmean4 = kmean.reshape(batch, nb, 1, MOBA_KV_WIDTH)
    return pl.pallas_call(
        functools.partial(_moba_prompt_kernel, nb=nb),
        grid=(batch, nb),
        in_specs=[
            pl.BlockSpec((MOBA_HEADS, None, tq, MOBA_KV_WIDTH), lambda b, i: (0, b, i, 0)),
            pl.BlockSpec((None, seq, 2 * MOBA_KV_WIDTH), lambda b, i: (b, 0, 0)),
            pl.BlockSpec((None, nb, 1, MOBA_KV_WIDTH), lambda b, i: (b, 0, 0, 0)),
            _const_spec((rows, 1)),
        ],
        out_specs=pl.BlockSpec((tq, BRANCH_WIDTH), lambda b, i: (b * nb + i, 0)),
        out_shape=jax.ShapeDtypeStruct((batch * seq, BRANCH_WIDTH), BF16),
        scratch_shapes=[pltpu.VMEM((rows, 1), F32), pltpu.VMEM((rows, 1), F32),
                        pltpu.VMEM((rows, MOBA_KV_WIDTH), F32)],
        compiler_params=pltpu.CompilerParams(
            dimension_semantics=("arbitrary", "arbitrary"), vmem_limit_bytes=VMEM_LIMIT),
        name="moba_prompt",
    )(qm4, mkv3, kmean4, slope_rows)


def _mla_sample_kernel(pt_ref, q_ref, nckv_ref, nkr_ref, *refs, dec_seq, n_steps, scale):
    del pt_ref
    npg = PAGES_PER_STEP
    ckv_refs, kr_refs = refs[:npg], refs[npg:2 * npg]
    o_ref, m_ref, l_ref, acc_ref = refs[2 * npg:]
    step = pl.program_id(1)
    rows = MLA_HEADS * dec_seq
    q = q_ref[...].reshape(rows, QK_PAD)
    q_lat = q[:, :KV_LORA].astype(BF16)
    q_rope = q[:, KV_LORA:KV_LORA + QK_ROPE].astype(BF16)

    @pl.when(step == 0)
    def _():
        kn = nckv_ref[...].astype(BF16)
        s = (_dot_nt(q_lat, kn) + _dot_nt(q_rope, nkr_ref[...].astype(BF16))) * scale
        t_q = lax.broadcasted_iota(jnp.int32, (MLA_HEADS, dec_seq, dec_seq), 1).reshape(rows, dec_seq)
        t_k = lax.broadcasted_iota(jnp.int32, (rows, dec_seq), 1)
        s = jnp.where(t_k <= t_q, s, NEG_INF)
        m0 = jnp.max(s, axis=1, keepdims=True)
        p = jnp.exp(s - m0)
        m_ref[...] = m0
        l_ref[...] = jnp.sum(p, axis=1, keepdims=True)
        acc_ref[...] = _dot(p.astype(BF16), kn)

    k = jnp.concatenate([r[...] for r in ckv_refs], axis=0).astype(BF16)
    kr = jnp.concatenate([r[...] for r in kr_refs], axis=0).astype(BF16)
    s = (_dot_nt(q_lat, k) + _dot_nt(q_rope, kr)) * scale
    _online_softmax_step(s, k, m_ref, l_ref, acc_ref)

    @pl.when(step == n_steps - 1)
    def _():
        o_ref[...] = (acc_ref[...] / l_ref[...]).reshape(MLA_HEADS, dec_seq, KV_LORA)


def _mla_sample(page_table, qf, new_ckv, new_kr, cache_ckv, cache_krope, layer, *, dec_batch, dec_seq):
    npg = PAGES_PER_STEP
    n_pages = page_table.shape[1]
    n_steps = n_pages // npg
    rows = MLA_HEADS * dec_seq
    scale = (QK_NOPE + QK_ROPE) ** -0.5

    def page_spec(width, i):
        return pl.BlockSpec((None, None, PAGE_SIZE, width),
                            lambda b, s, pt: (layer, pt[b, s * npg + i], 0, 0))

    in_specs = [
        pl.BlockSpec((MLA_HEADS, dec_seq, QK_PAD), lambda b, s, pt: (0, b, 0)),
        pl.BlockSpec((dec_seq, KV_LORA), lambda b, s, pt: (b, 0)),
        pl.BlockSpec((dec_seq, QK_ROPE), lambda b, s, pt: (b, 0)),
    ]
    in_specs += [page_spec(KV_LORA, i) for i in range(npg)]
    in_specs += [page_spec(QK_ROPE, i) for i in range(npg)]
    return pl.pallas_call(
        functools.partial(_mla_sample_kernel, dec_seq=dec_seq, n_steps=n_steps, scale=scale),
        grid_spec=pltpu.PrefetchScalarGridSpec(
            num_scalar_prefetch=1, grid=(dec_batch, n_steps), in_specs=in_specs,
            out_specs=pl.BlockSpec((MLA_HEADS, dec_seq, KV_LORA), lambda b, s, pt: (0, b, 0)),
            scratch_shapes=[pltpu.VMEM((rows, 1), F32), pltpu.VMEM((rows, 1), F32),
                            pltpu.VMEM((rows, KV_LORA), F32)]),
        out_shape=jax.ShapeDtypeStruct((MLA_HEADS, dec_batch * dec_seq, KV_LORA), F32),
        compiler_params=pltpu.CompilerParams(
            dimension_semantics=("arbitrary", "arbitrary"), vmem_limit_bytes=VMEM_LIMIT),
        name="mla_sample",
    )(page_table, qf, new_ckv, new_kr, *([cache_ckv] * npg), *([cache_krope] * npg))


def _uv_kernel(o_lat_ref, wuv_ref, o_ref):
    o = _dot(o_lat_ref[0].astype(BF16), wuv_ref[0])
    for h in range(1, MLA_HEADS):
        o = o + _dot(o_lat_ref[h].astype(BF16), wuv_ref[h])
    o_ref[...] = o


def _uv_sample(o_lat, w_uv):
    n = o_lat.shape[1]
    return pl.pallas_call(
        _uv_kernel,
        out_shape=jax.ShapeDtypeStruct((n, BRANCH_WIDTH), F32),
        compiler_params=pltpu.CompilerParams(vmem_limit_bytes=VMEM_LIMIT),
        name="uv_sample",
    )(o_lat, w_uv)


def _moba_sample_kernel(pt_ref, q_ref, nk_ref, nv_ref, slope_ref, *refs, dec_seq, n_steps, past_len):
    del pt_ref
    npg = PAGES_PER_STEP
    k_refs, v_refs = refs[:npg], refs[npg:2 * npg]
    o_ref, gate_ref, m_ref, l_ref, acc_ref = refs[2 * npg:]
    step = pl.program_id(1)
    rows = MOBA_HEADS * dec_seq
    scale = MOBA_HEAD_DIM ** -0.5
    pages_per_block = MOBA_BLOCK // PAGE_SIZE
    blocks_per_step = npg // pages_per_block
    nb_past = n_steps * blocks_per_step
    qf = q_ref[...].reshape(rows, MOBA_KV_WIDTH)
    q = qf.astype(BF16)
    slope = slope_ref[...]
    t_q = past_len + lax.broadcasted_iota(jnp.int32, (MOBA_HEADS, dec_seq, 1), 1).reshape(rows, 1)
    off_k = lax.broadcasted_iota(jnp.int32, (1, MOBA_BLOCK), 1)
    wide = (rows, MOBA_KV_WIDTH)

    for jj in range(blocks_per_step):
        j = step * blocks_per_step + jj
        kb = jnp.concatenate([k_refs[jj * pages_per_block + i][...] for i in range(pages_per_block)], axis=0)
        vb = jnp.concatenate([v_refs[jj * pages_per_block + i][...] for i in range(pages_per_block)], axis=0)
        kmean = jnp.mean(kb, axis=0, keepdims=True)
        gate = jnp.sum(qf * kmean, axis=1, keepdims=True)
        dist = (t_q - (j * MOBA_BLOCK + off_k)).astype(F32)
        s = _dot_nt(q, kb.astype(BF16)) * scale - slope * dist
        mj = jnp.max(s, axis=1, keepdims=True)
        p = jnp.exp(s - mj)
        gate_ref[j] = jnp.broadcast_to(gate, wide)
        m_ref[j] = jnp.broadcast_to(mj, wide)
        l_ref[j] = jnp.broadcast_to(jnp.sum(p, axis=1, keepdims=True), wide)
        acc_ref[j] = _dot(p.astype(BF16), vb.astype(BF16))

    @pl.when(step == n_steps - 1)
    def _():
        kn = nk_ref[...].astype(BF16)
        t_k = lax.broadcasted_iota(jnp.int32, (1, dec_seq), 1)
        dist = (t_q - past_len) - t_k
        s = _dot_nt(q, kn) * scale - slope * dist.astype(F32)
        s = jnp.where(dist >= 0, s, NEG_INF)
        m_own = jnp.max(s, axis=1, keepdims=True)
        p = jnp.exp(s - m_own)
        l_own = jnp.broadcast_to(jnp.sum(p, axis=1, keepdims=True), wide)
        acc_own = _dot(p.astype(BF16), nv_ref[...].astype(BF16))
        m_own = jnp.broadcast_to(m_own, wide)

        gates = [gate_ref[j] for j in range(nb_past)]
        picked = [jnp.zeros(wide, jnp.bool_)] * nb_past
        for _ in range(min(MOBA_TOPK, nb_past)):
            cur = [jnp.where(picked[j], NEG_INF, gates[j]) for j in range(nb_past)]
            best = functools.reduce(jnp.maximum, cur)
            idx = jnp.full(wide, nb_past, jnp.int32)
            for j in reversed(range(nb_past)):
                idx = jnp.where(jnp.logical_and(cur[j] == best, jnp.logical_not(picked[j])), j, idx)
            picked = [jnp.logical_or(picked[j], idx == j) for j in range(nb_past)]

        m_tot = m_own
        for j in range(nb_past):
            m_tot = jnp.maximum(m_tot, jnp.where(picked[j], m_ref[j], NEG_INF))
        w_own = jnp.exp(m_own - m_tot)
        l_tot = w_own * l_own
        acc = w_own * acc_own
        for j in range(nb_past):
            wj = jnp.where(picked[j], jnp.exp(m_ref[j] - m_tot), 0.0)
            l_tot = l_tot + wj * l_ref[j]
            acc = acc + wj * acc_ref[j]
        o = acc / l_tot
        pieces = []
        for h in range(MOBA_HEADS):
            g = h // MOBA_GROUP
            pieces.append(o[h * dec_seq:(h + 1) * dec_seq, g * MOBA_HEAD_DIM:(g + 1) * MOBA_HEAD_DIM])
        o_ref[...] = jnp.concatenate(pieces, axis=1)


def _moba_sample(page_table, qm, new_k, new_v, slope_rows, cache_k, cache_v, layer, *, dec_batch, dec_seq):
    npg = PAGES_PER_STEP
    n_pages = page_table.shape[1]
    n_steps = n_pages // npg
    past_len = n_pages * PAGE_SIZE
    nb_past = past_len // MOBA_BLOCK
    rows = MOBA_HEADS * dec_seq
    depth, n_phys = cache_k.shape[:2]
    cache_k = cache_k.reshape(depth, n_phys, PAGE_SIZE, MOBA_KV_WIDTH)
    cache_v = cache_v.reshape(depth, n_phys, PAGE_SIZE, MOBA_KV_WIDTH)

    def page_spec(i):
        return pl.BlockSpec((None, None, PAGE_SIZE, MOBA_KV_WIDTH),
                            lambda b, s, pt: (layer, pt[b, s * npg + i], 0, 0))

    in_specs = [
        pl.BlockSpec((MOBA_HEADS, dec_seq, MOBA_KV_WIDTH), lambda b, s, pt: (0, b, 0)),
        pl.BlockSpec((dec_seq, MOBA_KV_WIDTH), lambda b, s, pt: (b, 0)),
        pl.BlockSpec((dec_seq, MOBA_KV_WIDTH), lambda b, s, pt: (b, 0)),
        pl.BlockSpec((rows, 1), lambda b, s, pt: (0, 0)),
    ]
    in_specs += [page_spec(i) for i in range(npg)] * 2
    part =pltpu.VMEM((nb_past, rows, MOBA_KV_WIDTH), F32)
    return pl.pallas_call(
        functools.partial(_moba_sample_kernel, dec_seq=dec_seq, n_steps=n_steps, past_len=past_len),
        grid_spec=pltpu.PrefetchScalarGridSpec(
            num_scalar_prefetch=1, grid=(dec_batch, n_steps), in_specs=in_specs,
            out_specs=pl.BlockSpec((dec_seq, BRANCH_WIDTH), lambda b, s, pt: (b, 0)),
            scratch_shapes=[part, part, part, part]),
        out_shape=jax.ShapeDtypeStruct((dec_batch * dec_seq, BRANCH_WIDTH), F32),
        compiler_params=pltpu.CompilerParams(
            dimension_semantics=("arbitrary", "arbitrary"), vmem_limit_bytes=VMEM_LIMIT),
        name="moba_sample",
    )(page_table, qm, new_k, new_v, slope_rows, *([cache_k] * npg), *([cache_v] * npg))


def _ffn_kernel(x_ref, oa_ref, ob_ref, gl_ref, wbr_ref, wo_ref, g_ffn_ref, wgu_ref, wdn_ref,
                g_fin_ref, out_ref, *, final, n_chunks):
    br_a = _dot(oa_ref[...].astype(BF16), wbr_ref[0])
    br_b = _dot(ob_ref[...].astype(BF16), wbr_ref[1])
    merged = (jax.nn.sigmoid(gl_ref[:, :D_MODEL]) * br_a
              + jax.nn.sigmoid(gl_ref[:, D_MODEL:]) * br_b)
    h = x_ref[...] + _dot(merged.astype(BF16), wo_ref[...])
    hn = _rms(h, g_ffn_ref[...]).astype(BF16)
    cw = FFN_HIDDEN // n_chunks
    acc = h
    for c in range(n_chunks):
        a = _dot(hn, wgu_ref[:, c * cw:(c + 1) * cw])
        u = _dot(hn, wgu_ref[:, FFN_HIDDEN + c * cw:FFN_HIDDEN + (c + 1) * cw])
        act = (jax.nn.silu(a) * u).astype(BF16)
        acc = acc + _dot(act, wdn_ref[c * cw:(c + 1) * cw, :])
    out_ref[...] = _rms(acc, g_fin_ref[...]) if final else acc


def _ffn(x, o_a, o_b, gl, w, g_final, *, final, tm, n_chunks=2):
    n = x.shape[0]
    row = lambda i: (i, 0)
    return pl.pallas_call(
        functools.partial(_ffn_kernel, final=final, n_chunks=n_chunks),
        grid=(n // tm,),
        in_specs=[
            pl.BlockSpec((tm, D_MODEL), row),
            pl.BlockSpec((tm, BRANCH_WIDTH), row),
            pl.BlockSpec((tm, BRANCH_WIDTH), row),
            pl.BlockSpec((tm, 2 * D_MODEL), row),
            _const_spec((2, BRANCH_WIDTH, D_MODEL)),
            _const_spec((D_MODEL, D_MODEL)),
            _const_spec((1, D_MODEL)),
            _const_spec((D_MODEL, 2 * FFN_HIDDEN)),
            _const_spec((FFN_HIDDEN, D_MODEL)),
            _const_spec((1, D_MODEL)),
        ],
        out_specs=pl.BlockSpec((tm, D_MODEL), row),
        out_shape=jax.ShapeDtypeStruct((n, D_MODEL), F32),
        compiler_params=pltpu.CompilerParams(
            dimension_semantics=("arbitrary",), vmem_limit_bytes=VMEM_LIMIT),
        name="ffn_final" if final else "ffn",
    )(x, o_a, o_b, gl, w["w_branch"], w["w_o"], w["g_ffn"], w["w_gu"], w["w_down"], g_final)


def _prep_layer_weights(w_in, g_q, g_kv, w_uq, w_uk, w_uv, w_branch, w_o, g_attn, g_ffn, w_gu, w_down):
    s = [0, Q_LORA, Q_LORA + KV_LORA, Q_LORA + KV_LORA + QK_ROPE]
    s.append(s[-1] + MOBA_HEADS * MOBA_HEAD_DIM)
    s.append(s[-1] + MOBA_KV_WIDTH)
    s.append(s[-1] + MOBA_KV_WIDTH)
    c_q, c_kv, k_r, m_q, m_k, m_v, gate = (w_in[:, s[0]:s[1]], w_in[:, s[1]:s[2]], w_in[:, s[2]:s[3]],
                                           w_in[:, s[3]:s[4]], w_in[:, s[4]:s[5]], w_in[:, s[5]:s[6]],
                                           w_in[:, s[6]:])
    m_q = m_q.reshape(D_MODEL, MOBA_KV_HEADS, MOBA_GROUP, MOBA_HEAD_DIM)
    zeros = jnp.zeros_like(m_q[:, 0])
    m_q_bd = jnp.concatenate([
        jnp.concatenate([m_q[:, 0], zeros], axis=-1),
        jnp.concatenate([zeros, m_q[:, 1]], axis=-1)], axis=1).reshape(D_MODEL, MOBA_HEADS * LANES)
    w_in_perm = jnp.concatenate([c_q, c_kv, m_q_bd, m_k, m_v, gate, k_r], axis=1).astype(BF16)

    uq = w_uq.reshape(Q_LORA, MLA_HEADS, QK_NOPE + QK_ROPE)
    w_uqn = jnp.pad(uq[:, :, :QK_NOPE], ((0, 0), (0, 0), (0, LANES - QK_NOPE))).reshape(Q_LORA, MLA_HEADS * LANES)
    w_uqr = jnp.concatenate([uq[:, :, QK_NOPE:QK_NOPE + HALF_ROPE].reshape(Q_LORA, LANES),
                             uq[:, :, QK_NOPE + HALF_ROPE:].reshape(Q_LORA, LANES)], axis=1)
    uk = jnp.transpose(w_uk, (1, 2, 0))
    uk = jnp.pad(uk, ((0, 0), (0, LANES - QK_NOPE), (0, 0)))
    uv = jnp.transpose(w_uv, (1, 0, 2))
    eye = jnp.eye(MLA_HEADS, dtype=w_uv.dtype)
    uv_pad = (uv[:, :, None, :] * eye[:, None, :, None]).reshape(MLA_HEADS, KV_LORA, BRANCH_WIDTH)
    return {
        "w_in": w_in_perm, "g_q": g_q[None], "g_kv": g_kv[None], "g_attn": g_attn[None], "g_ffn": g_ffn[None],
        "w_uqn": w_uqn.astype(BF16), "w_uqr": w_uqr.astype(BF16), "w_uk": uk.astype(BF16),
        "w_uv": uv_pad.astype(BF16), "w_branch": w_branch.astype(BF16), "w_o": w_o.astype(BF16),
        "w_gu": w_gu.astype(BF16), "w_down": w_down.astype(BF16),
    }


def _rope_tables(pos):
    inv = ROPE_THETA ** (-jnp.arange(HALF_ROPE, dtype=F32) / HALF_ROPE)
    ang = pos.astype(F32)[:, None] * inv[None, :]
    reps = LANES // HALF_ROPE
    return jnp.tile(jnp.cos(ang), (1, reps)), jnp.tile(jnp.sin(ang), (1, reps))


def _slope_rows(tokens_per_head):
    slopes = 2.0 ** (-8.0 * jnp.arange(1, MOBA_HEADS + 1, dtype=F32) / MOBA_HEADS)
    return jnp.repeat(slopes, tokens_per_head)[:, None]


def kernel(x_prompt, x_sample, cache_ckv, cache_krope, cache_k, cache_v, page_table, w_in, g_q, g_kv, w_uq, w_uk, w_uv, w_branch, w_o, g_attn, g_ffn, w_gu, w_down, g_final):
    batch, seq, _ = x_prompt.shape
    dec_batch, dec_seq, _ = x_sample.shape
    depth = w_in.shape[0]
    past_len = page_table.shape[1] * PAGE_SIZE
    n_p, n_s = batch * seq, dec_batch * dec_seq
    tm_p = min(512, seq)
    tm_s = min(512, n_s)

    cos_p, sin_p = _rope_tables(jnp.arange(seq, dtype=jnp.int32))
    pos_s = past_len + jnp.arange(dec_seq, dtype=jnp.int32)
    cos_s, sin_s = _rope_tables(jnp.tile(pos_s, tm_s // dec_seq))
    slope_p = _slope_rows(MOBA_BLOCK)
    slope_s = _slope_rows(dec_seq)
    g_fin = g_final[None]

    hp = x_prompt.reshape(n_p, D_MODEL)
    hs = x_sample.reshape(n_s, D_MODEL)
    rows_p, rows_s = [], []
    for l in range(depth):
        w = _prep_layer_weights(w_in[l], g_q[l], g_kv[l], w_uq[l], w_uk[l], w_uv[l], w_branch[l], w_o[l],
                                g_attn[l], g_ffn[l], w_gu[l], w_down[l])
        final = l == depth - 1

        ckv, kr, mk, mv, gl, qf, qm, kf, mkv, kmean = _proj(hp, cos_p, sin_p, w, prompt=True, tm=tm_p)
        o_a = _mla_prompt(qf, kf, w["w_uv"], batch=batch, seq=seq, tq=min(128, seq), tk=min(256, seq))
        o_b = _moba_prompt(qm, mkv, kmean, slope_p, batch=batch, seq=seq)
        hp = _ffn(hp, o_a, o_b, gl, w, g_fin, final=final, tm=tm_p)
        rows_p.append((ckv, kr, mk, mv))

        ckv, kr, mk, mv, gl, qf, qm = _proj(hs, cos_s, sin_s, w, prompt=False, tm=tm_s)
        o_lat = _mla_sample(page_table, qf, ckv, kr, cache_ckv, cache_krope, l,
                            dec_batch=dec_batch, dec_seq=dec_seq)
        o_a = _uv_sample(o_lat, w["w_uv"])
        o_b = _moba_sample(page_table, qm, mk, mv, slope_s, cache_k, cache_v, l,
                           dec_batch=dec_batch, dec_seq=dec_seq)
        hs = _ffn(hs, o_a, o_b, gl, w, g_fin, final=final, tm=tm_s)
        rows_s.append((ckv, kr, mk, mv))

    def stack(rows, i, shape):
        return jnp.stack([r[i] for r in rows]).reshape((depth,) + shape)

    kv_shape = (MOBA_KV_HEADS, MOBA_HEAD_DIM)
    return (hp.reshape(batch, seq, D_MODEL),
            hs.reshape(dec_batch, dec_seq, D_MODEL),
            stack(rows_p, 0, (batch, seq, KV_LORA)),
            stack(rows_p, 1, (batch, seq, QK_ROPE)),
            stack(rows_p, 2, (batch, seq) + kv_shape),
            stack(rows_p, 3, (batch, seq) + kv_shape),
            stack(rows_s, 0, (dec_batch, dec_seq, KV_LORA)),
            stack(rows_s, 1, (dec_batch, dec_seq, QK_ROPE)),
            stack(rows_s, 2, (dec_batch, dec_seq) + kv_shape),
            stack(rows_s, 3, (dec_batch, dec_seq) + kv_shape))
```

```python
import functools
import operator

import jax
import jax.numpy as jnp
from jax import lax
from jax.experimental import pallas as pl
from jax.experimental.pallas import tpu as pltpu

F32 = jnp.float32
BF16 = jnp.bfloat16

D_MODEL = 1024
PAGE_SIZE = 128
MLA_HEADS = 8
Q_LORA = 384
KV_LORA = 256
QK_NOPE = 64
QK_ROPE = 32
V_HEAD = 64
ROPE_THETA = 10000.0
MOBA_HEADS = 8
MOBA_KV_HEADS = 2
MOBA_GROUP = MOBA_HEADS // MOBA_KV_HEADS
MOBA_HEAD_DIM = 64
MOBA_BLOCK = 256
MOBA_TOPK = 3
BRANCH_WIDTH = 512
FFN_HIDDEN = 2816
RMS_EPS = 1e-6

LANES = 128
QK_PAD = KV_LORA + LANES
MOBA_KV_WIDTH = MOBA_KV_HEADS * MOBA_HEAD_DIM
HALF_ROPE = QK_ROPE // 2
PAGES_PER_BLOCK = MOBA_BLOCK // PAGE_SIZE

_OFF_CQ = 0
_OFF_CKV = _OFF_CQ + Q_LORA
_OFF_MQ = _OFF_CKV + KV_LORA
_OFF_MK = _OFF_MQ + MOBA_HEADS * LANES
_OFF_MV = _OFF_MK + MOBA_KV_WIDTH
_OFF_GATE = _OFF_MV + MOBA_KV_WIDTH
_OFF_KR = _OFF_GATE + 2 * D_MODEL
_D_IN_PERM = _OFF_KR + QK_ROPE

VMEM_LIMIT = 56 * 1024 * 1024
MLA_CHUNK_PAGES = 8
MOBA_GROUP_BLOCKS = 4
NEG_INF = float("-inf")


def _rms(x, g):
    return x * lax.rsqrt(jnp.mean(x * x, axis=-1, keepdims=True) + RMS_EPS) * g


def _dot(a, b):
    return jnp.dot(a, b, preferred_element_type=F32)


def _dot_nt(a, b):
    return lax.dot_general(a, b, (((1,), (1,)), ((), ())), preferred_element_type=F32)


def _const_spec(shape):
    return pl.BlockSpec(shape, lambda *_: (0,) * len(shape), pipeline_mode=pl.Buffered(1))


def _lane_slabs(x):
    return [x[:, j * LANES:(j + 1) * LANES] for j in range(x.shape[1] // LANES)]


def _slab_max(m, s):
    return functools.reduce(jnp.maximum, _lane_slabs(s), m)


def _row_max(m_slab):
    return jnp.broadcast_to(jnp.max(m_slab, axis=1, keepdims=True), m_slab.shape)


def _proj_kernel(x_ref, cos_ref, sin_ref, g_attn_ref, w_in_ref, g_q_ref, g_kv_ref,
                 w_uqn_ref, w_uqr_ref, w_uk_ref, *out_refs, prompt):
    if prompt:
        (ckv_ref, kr_ref, mk_ref, mv_ref, gl_ref, qf_ref, qm_ref,
         kf_ref, mkv_ref, kmean_ref) = out_refs
    else:
        ckv_ref, kr_ref, mk_ref, mv_ref, gl_ref, qf_ref, qm_ref = out_refs
    tm = x_ref.shape[0]
    hn = _rms(x_ref[...], g_attn_ref[...]).astype(BF16)

    def seg(lo, hi):
        return _dot(hn, w_in_ref[:, lo:hi])

    cos = cos_ref[...]
    sin = sin_ref[...]

    ckv = _rms(seg(_OFF_CKV, _OFF_MQ), g_kv_ref[...])
    ckv_ref[...] = ckv
    kr = seg(_OFF_KR, _D_IN_PERM)
    kr_swapped = jnp.concatenate([kr[:, HALF_ROPE:], kr[:, :HALF_ROPE]], axis=1)
    lane32 = lax.broadcasted_iota(jnp.int32, (tm, QK_ROPE), 1)
    sin_signed = jnp.where(lane32 < HALF_ROPE, -sin[:, :QK_ROPE], sin[:, :QK_ROPE])
    kr = kr * cos[:, :QK_ROPE] + kr_swapped * sin_signed
    kr_ref[...] = kr

    mk = seg(_OFF_MK, _OFF_MV)
    mv = seg(_OFF_MV, _OFF_GATE)
    mk_ref[...] = mk
    mv_ref[...] = mv
    gl_ref[...] = seg(_OFF_GATE, _OFF_KR)

    mq = seg(_OFF_MQ, _OFF_MK)
    for h in range(MOBA_HEADS):
        qm_ref[h] = mq[:, h * LANES:(h + 1) * LANES].astype(qm_ref.dtype)

    cq = _rms(seg(_OFF_CQ, _OFF_CKV), g_q_ref[...]).astype(BF16)
    q_nope = _dot(cq, w_uqn_ref[...]).astype(BF16)
    q_rot = _dot(cq, w_uqr_ref[...])
    r1, r2 = q_rot[:, :LANES], q_rot[:, LANES:]
    o1 = r1 * cos - r2 * sin
    o2 = r2 * cos + r1 * sin
    lane = lax.broadcasted_iota(jnp.int32, (tm, LANES), 1)
    for h in range(MLA_HEADS):
        q_lat = _dot(q_nope[:, h * LANES:(h + 1) * LANES], w_uk_ref[h])
        shift_a = (LANES - HALF_ROPE * h) % LANES
        shift_b = (HALF_ROPE - HALF_ROPE * h) % LANES
        a = o1 if shift_a == 0 else pltpu.roll(o1, shift_a, axis=1)
        b = o2 if shift_b == 0 else pltpu.roll(o2, shift_b, axis=1)
        rope = jnp.where(lane < HALF_ROPE, a, jnp.where(lane < QK_ROPE, b, 0.0))
        qf_ref[h, :, :KV_LORA] = q_lat.astype(qf_ref.dtype)
        qf_ref[h, :, KV_LORA:] = rope.astype(qf_ref.dtype)

    if prompt:
        kf_ref[:, :KV_LORA] = ckv.astype(BF16)
        kr_pad = jnp.concatenate([kr, jnp.zeros((tm, LANES - QK_ROPE), F32)], axis=1)
        kf_ref[:, KV_LORA:] = kr_pad.astype(BF16)
        mkv_ref[:, :MOBA_KV_WIDTH] = mk.astype(BF16)
        mkv_ref[:, MOBA_KV_WIDTH:] = mv.astype(BF16)
        for j in range(tm // MOBA_BLOCK):
            kmean_ref[j] = jnp.mean(mk[j * MOBA_BLOCK:(j + 1) * MOBA_BLOCK], axis=0, keepdims=True)


def _proj(x, cos_tab, sin_tab, w, *, prompt, tm):
    n = x.shape[0]
    n_tab = cos_tab.shape[0] // tm
    grid = (n // tm,)
    row = lambda i: (i, 0)
    q_dtype = BF16 if prompt else F32
    out_shape = [
        jax.ShapeDtypeStruct((n, KV_LORA), F32),
        jax.ShapeDtypeStruct((n, QK_ROPE), F32),
        jax.ShapeDtypeStruct((n, MOBA_KV_WIDTH), F32),
        jax.ShapeDtypeStruct((n, MOBA_KV_WIDTH), F32),
        jax.ShapeDtypeStruct((n, 2 * D_MODEL), F32),
        jax.ShapeDtypeStruct((MLA_HEADS, n, QK_PAD), q_dtype),
        jax.ShapeDtypeStruct((MOBA_HEADS, n, MOBA_KV_WIDTH), F32),
    ]
    out_specs = [
        pl.BlockSpec((tm, KV_LORA), row),
        pl.BlockSpec((tm, QK_ROPE), row),
        pl.BlockSpec((tm, MOBA_KV_WIDTH), row),
        pl.BlockSpec((tm, MOBA_KV_WIDTH), row),
        pl.BlockSpec((tm, 2 * D_MODEL), row),
        pl.BlockSpec((MLA_HEADS, tm, QK_PAD), lambda i: (0, i, 0)),
        pl.BlockSpec((MOBA_HEADS, tm, MOBA_KV_WIDTH), lambda i: (0, i, 0)),
    ]
    if prompt:
        nblk = tm // MOBA_BLOCK
        out_shape += [
            jax.ShapeDtypeStruct((n, QK_PAD), BF16),
            jax.ShapeDtypeStruct((n, 2 * MOBA_KV_WIDTH), BF16),
            jax.ShapeDtypeStruct((n // MOBA_BLOCK, 1, MOBA_KV_WIDTH), F32),
        ]
        out_specs += [
            pl.BlockSpec((tm, QK_PAD), row),
            pl.BlockSpec((tm, 2 * MOBA_KV_WIDTH), row),
            pl.BlockSpec((nblk, 1, MOBA_KV_WIDTH), lambda i: (i, 0, 0)),
        ]
    tab = lambda i: (i % n_tab, 0)
    in_specs = [
        pl.BlockSpec((tm, D_MODEL), row),
        pl.BlockSpec((tm, LANES), tab),
        pl.BlockSpec((tm, LANES), tab),
        _const_spec((1, D_MODEL)),
        _const_spec((D_MODEL, _D_IN_PERM)),
        _const_spec((1, Q_LORA)),
        _const_spec((1, KV_LORA)),
        _const_spec((Q_LORA, MLA_HEADS * LANES)),
        _const_spec((Q_LORA, 2 * LANES)),
        _const_spec((MLA_HEADS, LANES, KV_LORA)),
    ]
    return pl.pallas_call(
        functools.partial(_proj_kernel, prompt=prompt),
        grid=grid, in_specs=in_specs, out_specs=out_specs, out_shape=out_shape,
        compiler_params=pltpu.CompilerParams(
            dimension_semantics=("arbitrary",), vmem_limit_bytes=VMEM_LIMIT),
        name="proj_prompt" if prompt else "proj_sample",
    )(x, cos_tab, sin_tab, w["g_attn"], w["w_in"], w["g_q"], w["g_kv"],
      w["w_uqn"], w["w_uqr"], w["w_uk"])


def _exp_accumulate(s, m, scale, v, l_ref, acc_ref):
    ps = [jnp.exp(sl - m if scale == 1.0 else (sl - m) * scale) for sl in _lane_slabs(s)]
    l_ref[...] += functools.reduce(operator.add, ps)
    acc_ref[...] += _dot(jnp.concatenate(ps, axis=1).astype(BF16), v)


def _mla_prompt_kernel(q_ref, k_ref, wuv_ref, o_ref, s_ref, m_ref, l_ref, acc_ref, *, tq, tk, scale):
    qi = pl.program_id(1)
    rows = MLA_HEADS * tq
    q = q_ref[...].reshape(rows, QK_PAD)
    last = (qi * tq + tq - 1) // tk

    def k_chunk(kc):
        return k_ref[pl.ds(pl.multiple_of(kc * tk, tk), tk), :]

    def scores(kc, masked):
        s = _dot_nt(q, k_chunk(kc))
        if masked:
            q_pos = qi * tq + lax.broadcasted_iota(jnp.int32, (MLA_HEADS, tq, tk), 1).reshape(rows, tk)
            k_pos = kc * tk + lax.broadcasted_iota(jnp.int32, (rows, tk), 1)
            s = jnp.where(k_pos <= q_pos, s, NEG_INF)
        s_ref[kc] = s
        m_ref[...] = _slab_max(m_ref[...], s)

    m_ref[...] = jnp.full(m_ref.shape, NEG_INF, F32)

    def score_body(kc, carry):
        scores(kc, False)
        return carry

    lax.fori_loop(0, last, score_body, 0)
    scores(last, True)

    m_ref[...] = _row_max(m_ref[...])
    l_ref[...] = jnp.zeros(l_ref.shape, F32)
    acc_ref[...] = jnp.zeros(acc_ref.shape, F32)

    def value_body(kc, carry):
        _exp_accumulate(s_ref[kc], m_ref[...], scale, k_chunk(kc)[:, :KV_LORA], l_ref, acc_ref)
        return carry

    lax.fori_loop(0, last + 1, value_body, 0)

    o_lat = (acc_ref[...] / jnp.sum(l_ref[...], axis=1, keepdims=True)).astype(BF16)
    o = _dot(o_lat[:tq], wuv_ref[0])
    for h in range(1, MLA_HEADS):
        o = o + _dot(o_lat[h * tq:(h + 1) * tq], wuv_ref[h])
    o_ref[...] = o.astype(o_ref.dtype)


def _mla_prompt(qf, kf, w_uv, *, batch, seq, tq, tk):
    nq = seq // tq
    rows = MLA_HEADS * tq
    qf4 = qf.reshape(MLA_HEADS, batch, seq, QK_PAD)
    kf3 = kf.reshape(batch, seq, QK_PAD)
    scale = (QK_NOPE + QK_ROPE) ** -0.5
    return pl.pallas_call(
        functools.partial(_mla_prompt_kernel, tq=tq, tk=tk, scale=scale),
        grid=(batch, nq),
        in_specs=[
            pl.BlockSpec((MLA_HEADS, None, tq, QK_PAD), lambda b, i: (0, b, i, 0)),
            pl.BlockSpec((None, seq, QK_PAD), lambda b, i: (b, 0, 0)),
            _const_spec((MLA_HEADS, KV_LORA, BRANCH_WIDTH)),
        ],
        out_specs=pl.BlockSpec((tq, BRANCH_WIDTH), lambda b, i: (b * nq + i, 0)),
        out_shape=jax.ShapeDtypeStruct((batch * seq, BRANCH_WIDTH), BF16),
        scratch_shapes=[pltpu.VMEM((seq // tk, rows, tk), F32), pltpu.VMEM((rows, LANES), F32),
                        pltpu.VMEM((rows, LANES), F32), pltpu.VMEM((rows, KV_LORA), F32)],
        compiler_params=pltpu.CompilerParams(
            dimension_semantics=("arbitrary", "arbitrary"), vmem_limit_bytes=VMEM_LIMIT),
        name="mla_prompt",
    )(qf4, kf3, w_uv)


def _gate_scores(qf, kmean, kmean_is_transposed):
    mm = _dot if kmean_is_transposed else _dot_nt
    q_hi = qf.astype(BF16)
    q_lo = (qf - q_hi.astype(F32)).astype(BF16)
    k_hi = kmean.astype(BF16)
    k_lo = (kmean - k_hi.astype(F32)).astype(BF16)
    return mm(q_hi, k_hi) + (mm(q_hi, k_lo) + mm(q_lo, k_hi))


def _topk_lanes(gate, valid, n_sel):
    nb = gate.shape[1]
    lane = lax.broadcasted_iota(jnp.int32, gate.shape, 1)
    g = jnp.where(valid, gate, NEG_INF)
    picked = jnp.zeros(gate.shape, jnp.bool_)
    for _ in range(n_sel):
        cur = jnp.where(picked, NEG_INF, g)
        best = jnp.max(cur, axis=1, keepdims=True)
        cand = jnp.logical_and(cur == best, jnp.logical_not(picked))
        idx = jnp.min(jnp.where(cand, lane, nb), axis=1, keepdims=True)
        picked = jnp.logical_or(picked, lane == idx)
    return jnp.logical_and(picked, valid)


def _moba_prompt_kernel(q_ref, kv_ref, kmean_ref, slope_ref, o_ref, s_ref, m_ref, l_ref, acc_ref, *, nb):
    tq = MOBA_BLOCK
    qi = pl.program_id(1)
    rows = MOBA_HEADS * tq
    scale = MOBA_HEAD_DIM ** -0.5
    own_slot = nb - 1
    qf = q_ref[...].reshape(rows, MOBA_KV_WIDTH)
    q = qf.astype(BF16)
    slope = slope_ref[...]
    t_q = lax.broadcasted_iota(jnp.int32, (MOBA_HEADS, tq, 1), 1).reshape(rows, 1)
    off_k = lax.broadcasted_iota(jnp.int32, (1, tq), 1)

    gate = _gate_scores(qf, kmean_ref[...].reshape(nb, MOBA_KV_WIDTH), False)
    blk = lax.broadcasted_iota(jnp.int32, (rows, nb), 1)
    sel = _topk_lanes(gate, blk < qi, min(MOBA_TOPK, nb))
    sel_bias = jnp.where(sel, 0.0, NEG_INF)

    dist_own = t_q - off_k
    alibi_own = -slope * dist_own.astype(F32)

    def block_rows(j):
        return pl.ds(j * tq if isinstance(j, int) else pl.multiple_of(j * tq, tq), tq)

    def keys(j):
        return kv_ref[block_rows(j), :MOBA_KV_WIDTH]

    def values(j):
        return kv_ref[block_rows(j), MOBA_KV_WIDTH:]

    s = jnp.where(dist_own >= 0, _dot_nt(q, keys(qi)) * scale + alibi_own, NEG_INF)
    s_ref[own_slot] = s
    m_ref[...] = _slab_max(jnp.full(m_ref.shape, NEG_INF, F32), s)
    for j in range(nb - 1):
        @pl.when(j < qi)
        def _():
            shift = sel_bias[:, j:j + 1] - slope * ((qi - j) * tq).astype(F32)
            sj = _dot_nt(q, keys(j)) * scale + alibi_own + shift
            s_ref[j] = sj
            m_ref[...] = _slab_max(m_ref[...], sj)

    m_ref[...] = _row_max(m_ref[...])
    l_ref[...] = jnp.zeros(l_ref.shape, F32)
    acc_ref[...] = jnp.zeros(acc_ref.shape, F32)
    _exp_accumulate(s_ref[own_slot], m_ref[...], 1.0, values(qi), l_ref, acc_ref)
    for j in range(nb - 1):
        @pl.when(j < qi)
        def _():
            _exp_accumulate(s_ref[j], m_ref[...], 1.0, values(j), l_ref, acc_ref)

    o = acc_ref[...] / jnp.sum(l_ref[...], axis=1, keepdims=True)
    pieces = []
    for h in range(MOBA_HEADS):
        g = h // MOBA_GROUP
        pieces.append(o[h * tq:(h + 1) * tq, g * MOBA_HEAD_DIM:(g + 1) * MOBA_HEAD_DIM])
    o_ref[...] = jnp.concatenate(pieces, axis=1).astype(o_ref.dtype)


def _moba_prompt(qm, mkv, kmean, slope_rows, *, batch, seq):
    tq = MOBA_BLOCK
    nb = seq // tq
    rows = MOBA_HEADS * tq
    qm4 = qm.reshape(MOBA_HEADS, batch, seq, MOBA_KV_WIDTH)
    mkv3 = mkv.reshape(batch, seq, 2 * MOBA_KV_WIDTH)
    kmean4 = kmean.reshape(batch, nb, 1, MOBA_KV_WIDTH)
    return pl.pallas_call(
        functools.partial(_moba_prompt_kernel, nb=nb),
        grid=(batch, nb),
        in_specs=[
            pl.BlockSpec((MOBA_HEADS, None, tq, MOBA_KV_WIDTH), lambda b, i: (0, b, i, 0)),
            pl.BlockSpec((None, seq, 2 * MOBA_KV_WIDTH), lambda b, i: (b, 0, 0)),
            pl.BlockSpec((None, nb, 1, MOBA_KV_WIDTH), lambda b, i: (b, 0, 0, 0)),
            _const_spec((rows, 1)),
        ],
        out_specs=pl.BlockSpec((tq, BRANCH_WIDTH), lambda b, i: (b * nb + i, 0)),
        out_shape=jax.ShapeDtypeStruct((batch * seq, BRANCH_WIDTH), BF16),
        scratch_shapes=[pltpu.VMEM((nb, rows, tq), F32), pltpu.VMEM((rows, LANES), F32),
                        pltpu.VMEM((rows, LANES), F32), pltpu.VMEM((rows, MOBA_KV_WIDTH), F32)],
        compiler_params=pltpu.CompilerParams(
            dimension_semantics=("arbitrary", "arbitrary"), vmem_limit_bytes=VMEM_LIMIT),
        name="moba_prompt",
    )(qm4, mkv3, kmean4, slope_rows)


def _fetch_pages(pt_ref, layer, n_pages, hbm_refs, bufs, sems, request, slot, wait):
    def body(p, carry):
        page = pt_ref[request, p]
        for i, (hbm, buf) in enumerate(zip(hbm_refs, bufs)):
            cp = pltpu.make_async_copy(hbm.at[layer, page], buf.at[slot, p], sems.at[i, slot])
            if wait:
                cp.wait()
            else:
                cp.start()
        return carry

    lax.fori_loop(0, n_pages, body, 0)


def _prefetch_and_wait(pt_ref, layer, n_pages, hbm_refs, bufs, sems):
    b = pl.program_id(0)
    slot = b % 2
    fetch = functools.partial(_fetch_pages, pt_ref, layer, n_pages, hbm_refs, bufs, sems)

    @pl.when(b == 0)
    def _():
        fetch(0, 0, wait=False)

    @pl.when(b + 1 < pl.num_programs(0))
    def _():
        fetch(b + 1, 1 - slot, wait=False)

    fetch(b, slot, wait=True)
    return slot


def _mla_sample_kernel(pt_ref, q_ref, nckv_ref, nkr_ref, ckv_hbm, krt_hbm, o_ref,
                       ckv_buf, krt_buf, sems, s_ref, kb_ref, *, layer, dec_seq, n_pages, scale):
    slot = _prefetch_and_wait(pt_ref, layer, n_pages, (ckv_hbm, krt_hbm), (ckv_buf, krt_buf), sems)
    rows = MLA_HEADS * dec_seq
    ch = MLA_CHUNK_PAGES
    ck = ch * PAGE_SIZE
    q = q_ref[...].reshape(rows, QK_PAD)
    q_lat = q[:, :KV_LORA].astype(BF16)
    q_rope = q[:, KV_LORA:KV_LORA + QK_ROPE].astype(BF16)

    kn = nckv_ref[...].astype(BF16)
    s_new = _dot_nt(q_lat, kn) + _dot_nt(q_rope, nkr_ref[...].astype(BF16))
    t_q = lax.broadcasted_iota(jnp.int32, (MLA_HEADS, dec_seq, dec_seq), 1).reshape(rows, dec_seq)
    t_k = lax.broadcasted_iota(jnp.int32, (rows, dec_seq), 1)
    s_new = jnp.where(t_k <= t_q, s_new, NEG_INF)
    m_run = jnp.broadcast_to(jnp.max(s_new, axis=1, keepdims=True), (rows, LANES))

    for c in range(n_pages // ch):
        kb = ckv_buf[slot, c * ch:(c + 1) * ch].reshape(ck, KV_LORA).astype(BF16)
        kb_ref[c * ck:(c + 1) * ck, :] = kb
        krt = jnp.concatenate([krt_buf[slot, c * ch + i] for i in range(ch)], axis=1).astype(BF16)
        s = _dot_nt(q_lat, kb) + _dot(q_rope, krt)
        s_ref[:, c * ck:(c + 1) * ck] = s
        m_run = _slab_max(m_run, s)
    m = _row_max(m_run)

    l_run = jnp.zeros((rows, LANES), F32)
    acc = jnp.zeros((rows, KV_LORA), F32)
    for c in range(n_pages // ch):
        ps = [jnp.exp((sl - m) * scale) for sl in _lane_slabs(s_ref[:, c * ck:(c + 1) * ck])]
        l_run = l_run + functools.reduce(operator.add, ps)
        acc = acc + _dot(jnp.concatenate(ps, axis=1).astype(BF16), kb_ref[c * ck:(c + 1) * ck, :])
    p_new = jnp.exp((s_new - m[:, :1]) * scale)
    l = jnp.sum(l_run, axis=1, keepdims=True) + jnp.sum(p_new, axis=1, keepdims=True)
    acc = acc + _dot(p_new.astype(BF16), kn)
    o_ref[...] = (acc / l).reshape(MLA_HEADS, dec_seq, KV_LORA)


def _mla_sample(page_table, qf, new_ckv, new_kr, cache_ckv, cache_krope_t, layer, *, dec_batch, dec_seq):
    n_pages = page_table.shape[1]
    past_len = n_pages * PAGE_SIZE
    rows = MLA_HEADS * dec_seq
    scale = (QK_NOPE + QK_ROPE) ** -0.5
    in_specs = [
        pl.BlockSpec((MLA_HEADS, dec_seq, QK_PAD), lambda b, pt: (0, b, 0)),
        pl.BlockSpec((dec_seq, KV_LORA), lambda b, pt: (b, 0)),
        pl.BlockSpec((dec_seq, QK_ROPE), lambda b, pt: (b, 0)),
        pl.BlockSpec(memory_space=pl.ANY),
        pl.BlockSpec(memory_space=pl.ANY),
    ]
    return pl.pallas_call(
        functools.partial(_mla_sample_kernel, layer=layer, dec_seq=dec_seq, n_pages=n_pages, scale=scale),
        grid_spec=pltpu.PrefetchScalarGridSpec(
            num_scalar_prefetch=1, grid=(dec_batch,), in_specs=in_specs,
            out_specs=pl.BlockSpec((MLA_HEADS, dec_seq, KV_LORA), lambda b, pt: (0, b, 0)),
            scratch_shapes=[pltpu.VMEM((2, n_pages, PAGE_SIZE, KV_LORA), F32),
                            pltpu.VMEM((2, n_pages, QK_ROPE, PAGE_SIZE), F32),
                            pltpu.SemaphoreType.DMA((2, 2)),
                            pltpu.VMEM((rows, past_len), F32),
                            pltpu.VMEM((past_len, KV_LORA), BF16)]),
        out_shape=jax.ShapeDtypeStruct((MLA_HEADS, dec_batch * dec_seq, KV_LORA), F32),
        compiler_params=pltpu.CompilerParams(
            dimension_semantics=("arbitrary",), vmem_limit_bytes=VMEM_LIMIT),
        name="mla_sample",
    )(page_table, qf, new_ckv, new_kr, cache_ckv, cache_krope_t)


def _uv_kernel(o_lat_ref, wuv_ref, o_ref):
    o = _dot(o_lat_ref[0].astype(BF16), wuv_ref[0])
    for h in range(1, MLA_HEADS):
        o = o + _dot(o_lat_ref[h].astype(BF16), wuv_ref[h])
    o_ref[...] = o


def _uv_sample(o_lat, w_uv):
    n = o_lat.shape[1]
    return pl.pallas_call(
        _uv_kernel,
        out_shape=jax.ShapeDtypeStruct((n, BRANCH_WIDTH), F32),
        compiler_params=pltpu.CompilerParams(vmem_limit_bytes=VMEM_LIMIT),
        name="uv_sample",
    )(o_lat, w_uv)


def _moba_sample_kernel(pt_ref, q_ref, nk_ref, nv_ref, slope_ref, kt_hbm, vt_hbm, o_ref,
                        kt_buf, vt_buf, sems, km_ref, m_ref, l_ref, acc_ref,
                        *, layer, dec_seq, n_pages, past_len):
    slot = _prefetch_and_wait(pt_ref, layer, n_pages, (kt_hbm, vt_hbm), (kt_buf, vt_buf), sems)
    rows = MOBA_HEADS * dec_seq
    scale = MOBA_HEAD_DIM ** -0.5
    nb_past = n_pages // PAGES_PER_BLOCK
    wide = (rows, LANES)
    qf = q_ref[...].reshape(rows, MOBA_KV_WIDTH)
    q = qf.astype(BF16)
    slope = slope_ref[...]
    t_q = past_len + lax.broadcasted_iota(jnp.int32, (MOBA_HEADS, dec_seq, LANES), 1).reshape(wide)
    off_k = lax.broadcasted_iota(jnp.int32, (1, MOBA_BLOCK), 1).astype(F32)
    alibi_off = slope[:, :1] * off_k
    lane = lax.broadcasted_iota(jnp.int32, wide, 1)
    lane_km = lax.broadcasted_iota(jnp.int32, km_ref.shape, 1)

    km_ref[...] = jnp.zeros(km_ref.shape, F32)
    m_ref[...] = jnp.zeros(wide, F32)
    l_ref[...] = jnp.zeros(wide, F32)

    def block(j):
        pages = [PAGES_PER_BLOCK * j + i for i in range(PAGES_PER_BLOCK)]
        kt = jnp.concatenate([kt_buf[slot, p] for p in pages], axis=1)
        vt = jnp.concatenate([vt_buf[slot, p] for p in pages], axis=1)
        kmean = jnp.sum(kt, axis=1, keepdims=True) / MOBA_BLOCK
        km_ref[...] = jnp.where(lane_km == j, kmean, km_ref[...])
        shift = -slope * (t_q - j * MOBA_BLOCK).astype(F32)
        s = _dot(q, kt.astype(BF16)) * scale + alibi_off
        s = jnp.concatenate([sl + shift for sl in _lane_slabs(s)], axis=1)
        mj = _row_max(_slab_max(jnp.full(wide, NEG_INF, F32), s))
        ps = [jnp.exp(sl - mj) for sl in _lane_slabs(s)]
        lj = jnp.sum(functools.reduce(operator.add, ps), axis=1, keepdims=True)
        m_ref[...] = jnp.where(lane == j, mj, m_ref[...])
        l_ref[...] = jnp.where(lane == j, lj, l_ref[...])
        acc_ref[j] = _dot_nt(jnp.concatenate(ps, axis=1).astype(BF16), vt.astype(BF16))

    def group(g, carry):
        for jj in range(MOBA_GROUP_BLOCKS):
            block(g * MOBA_GROUP_BLOCKS + jj)
        return carry

    lax.fori_loop(0, nb_past // MOBA_GROUP_BLOCKS, group, 0)

    kn = nk_ref[...].astype(BF16)
    t_k = lax.broadcasted_iota(jnp.int32, (1, dec_seq), 1)
    dist = (t_q[:, :1] - past_len) - t_k
    s = _dot_nt(q, kn) * scale - slope[:, :1] * dist.astype(F32)
    s = jnp.where(dist >= 0, s, NEG_INF)
    m_own = jnp.max(s, axis=1, keepdims=True)
    p = jnp.exp(s - m_own)
    l_own = jnp.sum(p, axis=1, keepdims=True)
    acc_own = _dot(p.astype(BF16), nv_ref[...].astype(BF16))

    gate = _gate_scores(qf, km_ref[...], True)
    sel = _topk_lanes(gate, lane < nb_past, min(MOBA_TOPK, nb_past))
    m_blk = m_ref[...]
    m_tot = jnp.maximum(m_own, jnp.max(jnp.where(sel, m_blk, NEG_INF), axis=1, keepdims=True))
    w_blk = jnp.where(sel, jnp.exp(m_blk - m_tot), 0.0)
    w_own = jnp.exp(m_own - m_tot)
    l_tot = jnp.sum(w_blk * l_ref[...], axis=1, keepdims=True) + w_own * l_own
    acc = w_own * acc_own
    for j in range(nb_past):
        acc = acc + w_blk[:, j:j + 1] * acc_ref[j]
    o = acc / l_tot
    pieces = []
    for h in range(MOBA_HEADS):
        g = h // MOBA_GROUP
        pieces.append(o[h * dec_seq:(h + 1) * dec_seq, g * MOBA_HEAD_DIM:(g + 1) * MOBA_HEAD_DIM])
    o_ref[...] = jnp.concatenate(pieces, axis=1)


def _moba_sample(page_table, qm, new_k, new_v, slope_rows, cache_kt, cache_vt, layer, *, dec_batch, dec_seq):
    n_pages = page_table.shape[1]
    past_len = n_pages * PAGE_SIZE
    nb_past = past_len // MOBA_BLOCK
    rows = MOBA_HEADS * dec_seq
    in_specs = [
        pl.BlockSpec((MOBA_HEADS, dec_seq, MOBA_KV_WIDTH), lambda b, pt: (0, b, 0)),
        pl.BlockSpec((dec_seq, MOBA_KV_WIDTH), lambda b, pt: (b, 0)),
        pl.BlockSpec((dec_seq, MOBA_KV_WIDTH), lambda b, pt: (b, 0)),
        pl.BlockSpec((rows, LANES), lambda b, pt: (0, 0)),
        pl.BlockSpec(memory_space=pl.ANY),
        pl.BlockSpec(memory_space=pl.ANY),
    ]
    page_buf = pltpu.VMEM((2, n_pages, MOBA_KV_WIDTH, PAGE_SIZE), F32)
    stat = pltpu.VMEM((rows, LANES), F32)
    return pl.pallas_call(
        functools.partial(_moba_sample_kernel, layer=layer, dec_seq=dec_seq, n_pages=n_pages,
                          past_len=past_len),
        grid_spec=pltpu.PrefetchScalarGridSpec(
            num_scalar_prefetch=1, grid=(dec_batch,), in_specs=in_specs,
            out_specs=pl.BlockSpec((dec_seq, BRANCH_WIDTH), lambda b, pt: (b, 0)),
            scratch_shapes=[page_buf, page_buf, pltpu.SemaphoreType.DMA((2, 2)),
                            pltpu.VMEM((MOBA_KV_WIDTH, LANES), F32), stat, stat,
                            pltpu.VMEM((nb_past, rows, MOBA_KV_WIDTH), F32)]),
        out_shape=jax.ShapeDtypeStruct((dec_batch * dec_seq, BRANCH_WIDTH), F32),
        compiler_params=pltpu.CompilerParams(
            dimension_semantics=("arbitrary",), vmem_limit_bytes=VMEM_LIMIT),
        name="moba_sample",
    )(page_table, qm, new_k, new_v, slope_rows, cache_kt, cache_vt)


def _ffn_kernel(x_ref, oa_ref, ob_ref, gl_ref, wbr_ref, wo_ref, g_ffn_ref, wgu_ref, wdn_ref,
                g_fin_ref, out_ref, *, final, n_chunks):
    br_a = _dot(oa_ref[...].astype(BF16), wbr_ref[0])
    br_b = _dot(ob_ref[...].astype(BF16), wbr_ref[1])
    merged = (jax.nn.sigmoid(gl_ref[:, :D_MODEL]) * br_a
              + jax.nn.sigmoid(gl_ref[:, D_MODEL:]) * br_b)
    h = x_ref[...] + _dot(merged.astype(BF16), wo_ref[...])
    hn = _rms(h, g_ffn_ref[...]).astype(BF16)
    cw = FFN_HIDDEN // n_chunks
    acc = h
    for c in range(n_chunks):
        a = _dot(hn, wgu_ref[:, c * cw:(c + 1) * cw])
        u = _dot(hn, wgu_ref[:, FFN_HIDDEN + c * cw:FFN_HIDDEN + (c + 1) * cw])
        act = (jax.nn.silu(a) * u).astype(BF16)
        acc = acc + _dot(act, wdn_ref[c * cw:(c + 1) * cw, :])
    out_ref[...] = _rms(acc, g_fin_ref[...]) if final else acc


def _ffn(x, o_a, o_b, gl, w, g_final, *, final, tm, n_chunks=2):
    n = x.shape[0]
    row = lambda i: (i, 0)
    return pl.pallas_call(
        functools.partial(_ffn_kernel, final=final, n_chunks=n_chunks),
        grid=(n // tm,),
        in_specs=[
            pl.BlockSpec((tm, D_MODEL), row),
            pl.BlockSpec((tm, BRANCH_WIDTH), row),
            pl.BlockSpec((tm, BRANCH_WIDTH), row),
            pl.BlockSpec((tm, 2 * D_MODEL), row),
            _const_spec((2, BRANCH_WIDTH, D_MODEL)),
            _const_spec((D_MODEL, D_MODEL)),
            _const_spec((1, D_MODEL)),
            _const_spec((D_MODEL, 2 * FFN_HIDDEN)),
            _const_spec((FFN_HIDDEN, D_MODEL)),
            _const_spec((1, D_MODEL)),
        ],
        out_specs=pl.BlockSpec((tm, D_MODEL), row),
        out_shape=jax.ShapeDtypeStruct((n, D_MODEL), F32),
        compiler_params=pltpu.CompilerParams(
            dimension_semantics=("arbitrary",), vmem_limit_bytes=VMEM_LIMIT),
        name="ffn_final" if final else "ffn",
    )(x, o_a, o_b, gl, w["w_branch"], w["w_o"], w["g_ffn"], w["w_gu"], w["w_down"], g_final)


def _prep_layer_weights(w_in, g_q, g_kv, w_uq, w_uk, w_uv, w_branch, w_o, g_attn, g_ffn, w_gu, w_down):
    s = [0, Q_LORA, Q_LORA + KV_LORA, Q_LORA + KV_LORA + QK_ROPE]
    s.append(s[-1] + MOBA_HEADS * MOBA_HEAD_DIM)
    s.append(s[-1] + MOBA_KV_WIDTH)
    s.append(s[-1] + MOBA_KV_WIDTH)
    c_q, c_kv, k_r, m_q, m_k, m_v, gate = (w_in[:, s[0]:s[1]], w_in[:, s[1]:s[2]], w_in[:, s[2]:s[3]],
                                           w_in[:, s[3]:s[4]], w_in[:, s[4]:s[5]], w_in[:, s[5]:s[6]],
                                           w_in[:, s[6]:])
    m_q = m_q.reshape(D_MODEL, MOBA_KV_HEADS, MOBA_GROUP, MOBA_HEAD_DIM)
    zeros = jnp.zeros_like(m_q[:, 0])
    m_q_bd = jnp.concatenate([
        jnp.concatenate([m_q[:, 0], zeros], axis=-1),
        jnp.concatenate([zeros, m_q[:, 1]], axis=-1)], axis=1).reshape(D_MODEL, MOBA_HEADS * LANES)
    w_in_perm = jnp.concatenate([c_q, c_kv, m_q_bd, m_k, m_v, gate, k_r], axis=1).astype(BF16)

    uq = w_uq.reshape(Q_LORA, MLA_HEADS, QK_NOPE + QK_ROPE)
    w_uqn = jnp.pad(uq[:, :, :QK_NOPE], ((0, 0), (0, 0), (0, LANES - QK_NOPE))).reshape(Q_LORA, MLA_HEADS * LANES)
    w_uqr = jnp.concatenate([uq[:, :, QK_NOPE:QK_NOPE + HALF_ROPE].reshape(Q_LORA, LANES),
                             uq[:, :, QK_NOPE + HALF_ROPE:].reshape(Q_LORA, LANES)], axis=1)
    uk = jnp.transpose(w_uk, (1, 2, 0))
    uk = jnp.pad(uk, ((0, 0), (0, LANES - QK_NOPE), (0, 0)))
    uv = jnp.transpose(w_uv, (1, 0, 2))
    eye = jnp.eye(MLA_HEADS, dtype=w_uv.dtype)
    uv_pad = (uv[:, :, None, :] * eye[:, None, :, None]).reshape(MLA_HEADS, KV_LORA, BRANCH_WIDTH)
    return {
        "w_in": w_in_perm, "g_q": g_q[None], "g_kv": g_kv[None], "g_attn": g_attn[None], "g_ffn": g_ffn[None],
        "w_uqn": w_uqn.astype(BF16), "w_uqr": w_uqr.astype(BF16), "w_uk": uk.astype(BF16),
        "w_uv": uv_pad.astype(BF16), "w_branch": w_branch.astype(BF16), "w_o": w_o.astype(BF16),
        "w_gu": w_gu.astype(BF16), "w_down": w_down.astype(BF16),
    }


def _rope_tables(pos):
    inv = ROPE_THETA ** (-jnp.arange(HALF_ROPE, dtype=F32) / HALF_ROPE)
    ang = pos.astype(F32)[:, None] * inv[None, :]
    reps = LANES // HALF_ROPE
    return jnp.tile(jnp.cos(ang), (1, reps)), jnp.tile(jnp.sin(ang), (1, reps))


def _slope_rows(tokens_per_head, width):
    slopes = 2.0 ** (-8.0 * jnp.arange(1, MOBA_HEADS + 1, dtype=F32) / MOBA_HEADS)
    return jnp.broadcast_to(jnp.repeat(slopes, tokens_per_head)[:, None], (MOBA_HEADS * tokens_per_head, width))


def kernel(x_prompt, x_sample, cache_ckv, cache_krope, cache_k, cache_v, page_table, w_in, g_q, g_kv, w_uq, w_uk, w_uv, w_branch, w_o, g_attn, g_ffn, w_gu, w_down, g_final):
    batch, seq, _ = x_prompt.shape
    dec_batch, dec_seq, _ = x_sample.shape
    depth, n_phys = cache_k.shape[:2]
    past_len = page_table.shape[1] * PAGE_SIZE
    n_p, n_s = batch * seq, dec_batch * dec_seq
    tm_p = min(512, seq)
    tm_s = min(512, n_s)

    cache_krope_t = jnp.transpose(cache_krope, (0, 1, 3, 2))
    cache_kt = jnp.transpose(cache_k, (0, 1, 3, 4, 2)).reshape(depth, n_phys, MOBA_KV_WIDTH, PAGE_SIZE)
    cache_vt = jnp.transpose(cache_v, (0, 1, 3, 4, 2)).reshape(depth, n_phys, MOBA_KV_WIDTH, PAGE_SIZE)

    cos_p, sin_p = _rope_tables(jnp.arange(seq, dtype=jnp.int32))
    pos_s = past_len + jnp.arange(dec_seq, dtype=jnp.int32)
    cos_s, sin_s = _rope_tables(jnp.tile(pos_s, tm_s // dec_seq))
    slope_p = _slope_rows(MOBA_BLOCK, 1)
    slope_s = _slope_rows(dec_seq, LANES)
    g_fin = g_final[None]

    hp = x_prompt.reshape(n_p, D_MODEL)
    hs = x_sample.reshape(n_s, D_MODEL)
    rows_p, rows_s = [], []
    for l in range(depth):
        w = _prep_layer_weights(w_in[l], g_q[l], g_kv[l], w_uq[l], w_uk[l], w_uv[l], w_branch[l], w_o[l],
                                g_attn[l], g_ffn[l], w_gu[l], w_down[l])
        final = l == depth - 1

        ckv, kr, mk, mv, gl, qf, qm, kf, mkv, kmean = _proj(hp, cos_p, sin_p, w, prompt=True, tm=tm_p)
        o_a = _mla_prompt(qf, kf, w["w_uv"], batch=batch, seq=seq, tq=min(128, seq), tk=min(256, seq))
        o_b = _moba_prompt(qm, mkv, kmean, slope_p, batch=batch, seq=seq)
        hp = _ffn(hp, o_a, o_b, gl, w, g_fin, final=final, tm=tm_p)
        rows_p.append((ckv, kr, mk, mv))

        ckv, kr, mk, mv, gl, qf, qm = _proj(hs, cos_s, sin_s, w, prompt=False, tm=tm_s)
        o_lat = _mla_sample(page_table, qf, ckv, kr, cache_ckv, cache_krope_t, l,
                            dec_batch=dec_batch, dec_seq=dec_seq)
        o_a = _uv_sample(o_lat, w["w_uv"])
        o_b = _moba_sample(page_table, qm, mk, mv, slope_s, cache_kt, cache_vt, l,
                           dec_batch=dec_batch, dec_seq=dec_seq)
        hs = _ffn(hs, o_a, o_b, gl, w, g_fin, final=final, tm=tm_s)
        rows_s.append((ckv, kr, mk, mv))

    def stack(rows, i, shape):
        return jnp.stack([r[i] for r in rows]).reshape((depth,) + shape)

    kv_shape = (MOBA_KV_HEADS, MOBA_HEAD_DIM)
    return (hp.reshape(batch, seq, D_MODEL),
            hs.reshape(dec_batch, dec_seq, D_MODEL),
            stack(rows_p, 0, (batch, seq, KV_LORA)),
            stack(rows_p, 1, (batch, seq, QK_ROPE)),
            stack(rows_p, 2, (batch, seq) + kv_shape),
            stack(rows_p, 3, (batch, seq) + kv_shape),
            stack(rows_s, 0, (dec_batch, dec_seq, KV_LORA)),
            stack(rows_s, 1, (dec_batch, dec_seq, QK_ROPE)),
            stack(rows_s, 2, (dec_batch, dec_seq) + kv_shape),
            stack(rows_s, 3, (dec_batch, dec_seq) + kv_shape))
```

```python
import functools
import operator

import jax
import jax.numpy as jnp
from jax import lax
from jax.experimental import pallas as pl
from jax.experimental.pallas import tpu as pltpu

F32 = jnp.float32
BF16 = jnp.bfloat16

D_MODEL = 1024
PAGE_SIZE = 128
MLA_HEADS = 8
Q_LORA = 384
KV_LORA = 256
QK_NOPE = 64
QK_ROPE = 32
V_HEAD = 64
ROPE_THETA = 10000.0
MOBA_HEADS = 8
MOBA_KV_HEADS = 2
MOBA_GROUP = MOBA_HEADS // MOBA_KV_HEADS
MOBA_HEAD_DIM = 64
MOBA_BLOCK = 256
MOBA_TOPK = 3
BRANCH_WIDTH = 512
FFN_HIDDEN = 2816
RMS_EPS = 1e-6

LANES = 128
QK_PAD = KV_LORA + LANES
MOBA_KV_WIDTH = MOBA_KV_HEADS * MOBA_HEAD_DIM
HALF_ROPE = QK_ROPE // 2
PAGES_PER_BLOCK = MOBA_BLOCK // PAGE_SIZE

_OFF_CQ = 0
_OFF_CKV = _OFF_CQ + Q_LORA
_OFF_MQ = _OFF_CKV + KV_LORA
_OFF_MK = _OFF_MQ + MOBA_HEADS * LANES
_OFF_MV = _OFF_MK + MOBA_KV_WIDTH
_OFF_KR = _OFF_MV + MOBA_KV_WIDTH
_D_IN_PERM = _OFF_KR + QK_ROPE

VMEM_LIMIT = 56 * 1024 * 1024
MLA_CHUNK_PAGES = 8
DMA_ISSUE_UNROLL = 4
NEG_INF = float("-inf")
LOG2_E = 1.4426950408889634


def _rms(x, g):
    return x * lax.rsqrt(jnp.mean(x * x, axis=-1, keepdims=True) + RMS_EPS) * g


def _dot(a, b):
    return jnp.dot(a, b, preferred_element_type=F32)


def _dot_nt(a, b):
    return lax.dot_general(a, b, (((1,), (1,)), ((), ())), preferred_element_type=F32)


def _const_spec(shape):
    return pl.BlockSpec(shape, lambda *_: (0,) * len(shape), pipeline_mode=pl.Buffered(1))


def _lane_slabs(x):
    return [x[:, j * LANES:(j + 1) * LANES] for j in range(x.shape[1] // LANES)]


def _slab_max(m, s):
    return functools.reduce(jnp.maximum, _lane_slabs(s), m)


def _row_max(m_slab):
    return jnp.broadcast_to(jnp.max(m_slab, axis=1, keepdims=True), m_slab.shape)


def _proj_kernel(x_ref, cos_ref, sin_ref, g_attn_ref, w_in_ref, g_q_ref, g_kv_ref,
                 w_uqn_ref, w_uqr_ref, w_uk_ref, *out_refs, prompt):
    if prompt:
        (ckv_ref, kr_ref, mk_ref, mv_ref, qf_ref, qm_ref,
         kf_ref, mkv_ref, kmean_ref) = out_refs
    else:
        ckv_ref, kr_ref, mk_ref, mv_ref, qf_ref, qm_ref = out_refs
    tm = x_ref.shape[0]
    hn = _rms(x_ref[...], g_attn_ref[...]).astype(BF16)

    def seg(lo, hi):
        return _dot(hn, w_in_ref[:, lo:hi])

    cos = cos_ref[...]
    sin = sin_ref[...]

    ckv = _rms(seg(_OFF_CKV, _OFF_MQ), g_kv_ref[...])
    ckv_ref[...] = ckv
    kr = seg(_OFF_KR, _D_IN_PERM)
    kr_swapped = jnp.concatenate([kr[:, HALF_ROPE:], kr[:, :HALF_ROPE]], axis=1)
    lane32 = lax.broadcasted_iota(jnp.int32, (tm, QK_ROPE), 1)
    sin_signed = jnp.where(lane32 < HALF_ROPE, -sin[:, :QK_ROPE], sin[:, :QK_ROPE])
    kr = kr * cos[:, :QK_ROPE] + kr_swapped * sin_signed
    kr_ref[...] = kr

    mk = seg(_OFF_MK, _OFF_MV)
    mv = seg(_OFF_MV, _OFF_KR)
    mk_ref[...] = mk
    mv_ref[...] = mv

    mq = seg(_OFF_MQ, _OFF_MK)
    for h in range(MOBA_HEADS):
        qm_ref[h] = mq[:, h * LANES:(h + 1) * LANES].astype(qm_ref.dtype)

    cq = _rms(seg(_OFF_CQ, _OFF_CKV), g_q_ref[...]).astype(BF16)
    q_nope = _dot(cq, w_uqn_ref[...]).astype(BF16)
    q_rot = _dot(cq, w_uqr_ref[...])
    r1, r2 = q_rot[:, :LANES], q_rot[:, LANES:]
    o1 = r1 * cos - r2 * sin
    o2 = r2 * cos + r1 * sin
    lane = lax.broadcasted_iota(jnp.int32, (tm, LANES), 1)
    for h in range(MLA_HEADS):
        q_lat = _dot(q_nope[:, h * LANES:(h + 1) * LANES], w_uk_ref[h])
        shift_a = (LANES - HALF_ROPE * h) % LANES
        shift_b = (HALF_ROPE - HALF_ROPE * h) % LANES
        a = o1 if shift_a == 0 else pltpu.roll(o1, shift_a, axis=1)
        b = o2 if shift_b == 0 else pltpu.roll(o2, shift_b, axis=1)
        rope = jnp.where(lane < HALF_ROPE, a, jnp.where(lane < QK_ROPE, b, 0.0))
        qf_ref[h, :, :KV_LORA] = q_lat.astype(qf_ref.dtype)
        qf_ref[h, :, KV_LORA:] = rope.astype(qf_ref.dtype)

    if prompt:
        kf_ref[:, :KV_LORA] = ckv.astype(BF16)
        kr_pad = jnp.concatenate([kr, jnp.zeros((tm, LANES - QK_ROPE), F32)], axis=1)
        kf_ref[:, KV_LORA:] = kr_pad.astype(BF16)
        mkv_ref[:, :MOBA_KV_WIDTH] = mk.astype(BF16)
        mkv_ref[:, MOBA_KV_WIDTH:] = mv.astype(BF16)
        for j in range(tm // MOBA_BLOCK):
            kmean_ref[j] = jnp.mean(mk[j * MOBA_BLOCK:(j + 1) * MOBA_BLOCK], axis=0, keepdims=True)


def _proj(x, cos_tab, sin_tab, w, *, prompt, tm):
    n = x.shape[0]
    n_tab = cos_tab.shape[0] // tm
    grid = (n // tm,)
    row = lambda i: (i, 0)
    q_dtype = BF16 if prompt else F32
    out_shape = [
        jax.ShapeDtypeStruct((n, KV_LORA), F32),
        jax.ShapeDtypeStruct((n, QK_ROPE), F32),
        jax.ShapeDtypeStruct((n, MOBA_KV_WIDTH), F32),
        jax.ShapeDtypeStruct((n, MOBA_KV_WIDTH), F32),
        jax.ShapeDtypeStruct((MLA_HEADS, n, QK_PAD), q_dtype),
        jax.ShapeDtypeStruct((MOBA_HEADS, n, MOBA_KV_WIDTH), F32),
    ]
    out_specs = [
        pl.BlockSpec((tm, KV_LORA), row),
        pl.BlockSpec((tm, QK_ROPE), row),
        pl.BlockSpec((tm, MOBA_KV_WIDTH), row),
        pl.BlockSpec((tm, MOBA_KV_WIDTH), row),
        pl.BlockSpec((MLA_HEADS, tm, QK_PAD), lambda i: (0, i, 0)),
        pl.BlockSpec((MOBA_HEADS, tm, MOBA_KV_WIDTH), lambda i: (0, i, 0)),
    ]
    if prompt:
        nblk = tm // MOBA_BLOCK
        out_shape += [
            jax.ShapeDtypeStruct((n, QK_PAD), BF16),
            jax.ShapeDtypeStruct((n, 2 * MOBA_KV_WIDTH), BF16),
            jax.ShapeDtypeStruct((n // MOBA_BLOCK, 1, MOBA_KV_WIDTH), F32),
        ]
        out_specs += [
            pl.BlockSpec((tm, QK_PAD), row),
            pl.BlockSpec((tm, 2 * MOBA_KV_WIDTH), row),
            pl.BlockSpec((nblk, 1, MOBA_KV_WIDTH), lambda i: (i, 0, 0)),
        ]
    tab = lambda i: (i % n_tab, 0)
    in_specs = [
        pl.BlockSpec((tm, D_MODEL), row),
        pl.BlockSpec((tm, LANES), tab),
        pl.BlockSpec((tm, LANES), tab),
        _const_spec((1, D_MODEL)),
        _const_spec((D_MODEL, _D_IN_PERM)),
        _const_spec((1, Q_LORA)),
        _const_spec((1, KV_LORA)),
        _const_spec((Q_LORA, MLA_HEADS * LANES)),
        _const_spec((Q_LORA, 2 * LANES)),
        _const_spec((MLA_HEADS, LANES, KV_LORA)),
    ]
    return pl.pallas_call(
        functools.partial(_proj_kernel, prompt=prompt),
        grid=grid, in_specs=in_specs, out_specs=out_specs, out_shape=out_shape,
        compiler_params=pltpu.CompilerParams(
            dimension_semantics=("arbitrary",), vmem_limit_bytes=VMEM_LIMIT),
        name="proj_prompt" if prompt else "proj_sample",
    )(x, cos_tab, sin_tab, w["g_attn"], w["w_in"], w["g_q"], w["g_kv"],
      w["w_uqn"], w["w_uqr"], w["w_uk"])


def _exp_scaled(x, scale):
    return jnp.exp(x) if scale == 1.0 else jnp.exp2(x * (scale * LOG2_E))


def _exp_accumulate(s, m, scale, v, l_ref, acc_ref):
    ps = [_exp_scaled(sl - m, scale) for sl in _lane_slabs(s)]
    l_ref[...] += functools.reduce(operator.add, ps)
    acc_ref[...] += _dot(jnp.concatenate(ps, axis=1).astype(BF16), v)


def _mla_prompt_kernel(q_ref, k_ref, wuv_ref, o_ref, s_ref, m_ref, l_ref, acc_ref, *, tq, tk, scale):
    qi = pl.program_id(1)
    rows = MLA_HEADS * tq
    q = q_ref[...].reshape(rows, QK_PAD)
    last = (qi * tq + tq - 1) // tk

    def k_chunk(kc):
        return k_ref[pl.ds(pl.multiple_of(kc * tk, tk), tk), :]

    def scores(kc, masked):
        s = _dot_nt(q, k_chunk(kc))
        if masked:
            q_pos = qi * tq + lax.broadcasted_iota(jnp.int32, (MLA_HEADS, tq, tk), 1).reshape(rows, tk)
            k_pos = kc * tk + lax.broadcasted_iota(jnp.int32, (rows, tk), 1)
            s = jnp.where(k_pos <= q_pos, s, NEG_INF)
        s_ref[kc] = s
        m_ref[...] = _slab_max(m_ref[...], s)

    m_ref[...] = jnp.full(m_ref.shape, NEG_INF, F32)

    def score_body(kc, carry):
        scores(kc, False)
        return carry

    lax.fori_loop(0, last, score_body, 0)
    scores(last, True)

    m_ref[...] = _row_max(m_ref[...])
    l_ref[...] = jnp.zeros(l_ref.shape, F32)
    acc_ref[...] = jnp.zeros(acc_ref.shape, F32)

    def value_body(kc, carry):
        _exp_accumulate(s_ref[kc], m_ref[...], scale, k_chunk(kc)[:, :KV_LORA], l_ref, acc_ref)
        return carry

    lax.fori_loop(0, last + 1, value_body, 0)

    o_lat = (acc_ref[...] / jnp.sum(l_ref[...], axis=1, keepdims=True)).astype(BF16)
    o = _dot(o_lat[:tq], wuv_ref[0])
    for h in range(1, MLA_HEADS):
        o = o + _dot(o_lat[h * tq:(h + 1) * tq], wuv_ref[h])
    o_ref[...] = o.astype(o_ref.dtype)


def _mla_prompt(qf, kf, w_uv, *, batch, seq, tq, tk):
    nq = seq // tq
    rows = MLA_HEADS * tq
    qf4 = qf.reshape(MLA_HEADS, batch, seq, QK_PAD)
    kf3 = kf.reshape(batch, seq, QK_PAD)
    scale = (QK_NOPE + QK_ROPE) ** -0.5
    return pl.pallas_call(
        functools.partial(_mla_prompt_kernel, tq=tq, tk=tk, scale=scale),
        grid=(batch, nq),
        in_specs=[
            pl.BlockSpec((MLA_HEADS, None, tq, QK_PAD), lambda b, i: (0, b, i, 0)),
            pl.BlockSpec((None, seq, QK_PAD), lambda b, i: (b, 0, 0)),
            _const_spec((MLA_HEADS, KV_LORA, BRANCH_WIDTH)),
        ],
        out_specs=pl.BlockSpec((tq, BRANCH_WIDTH), lambda b, i: (b * nq + i, 0)),
        out_shape=jax.ShapeDtypeStruct((batch * seq, BRANCH_WIDTH), BF16),
        scratch_shapes=[pltpu.VMEM((seq // tk, rows, tk), F32), pltpu.VMEM((rows, LANES), F32),
                        pltpu.VMEM((rows, LANES), F32), pltpu.VMEM((rows, KV_LORA), F32)],
        compiler_params=pltpu.CompilerParams(
            dimension_semantics=("arbitrary", "arbitrary"), vmem_limit_bytes=VMEM_LIMIT),
        name="mla_prompt",
    )(qf4, kf3, w_uv)


def _gate_scores(qf, kmean, kmean_is_transposed):
    mm = _dot if kmean_is_transposed else _dot_nt
    q_hi = qf.astype(BF16)
    q_lo = (qf - q_hi.astype(F32)).astype(BF16)
    k_hi = kmean.astype(BF16)
    k_lo = (kmean - k_hi.astype(F32)).astype(BF16)
    return mm(q_hi, k_hi) + (mm(q_hi, k_lo) + mm(q_lo, k_hi))


def _topk_lanes(gate, valid, n_sel):
    nb = gate.shape[1]
    lane = lax.broadcasted_iota(jnp.int32, gate.shape, 1)
    g = jnp.where(valid, gate, NEG_INF)
    picked = jnp.zeros(gate.shape, jnp.bool_)
    for _ in range(n_sel):
        cur = jnp.where(picked, NEG_INF, g)
        best = jnp.max(cur, axis=1, keepdims=True)
        cand = jnp.logical_and(cur == best, jnp.logical_not(picked))
        idx = jnp.min(jnp.where(cand, lane, nb), axis=1, keepdims=True)
        picked = jnp.logical_or(picked, lane == idx)
    return jnp.logical_and(picked, valid)


def _moba_prompt_kernel(q_ref, kv_ref, kmean_ref, slope_ref, o_ref, s_ref, m_ref, l_ref, acc_ref, *, nb):
    tq = MOBA_BLOCK
    qi = pl.program_id(1)
    rows = MOBA_HEADS * tq
    scale = MOBA_HEAD_DIM ** -0.5
    own_slot = nb - 1
    qf = q_ref[...].reshape(rows, MOBA_KV_WIDTH)
    q = qf.astype(BF16)
    slope = slope_ref[...]
    t_q = lax.broadcasted_iota(jnp.int32, (MOBA_HEADS, tq, 1), 1).reshape(rows, 1)
    off_k = lax.broadcasted_iota(jnp.int32, (1, tq), 1)

    gate = _gate_scores(qf, kmean_ref[...].reshape(nb, MOBA_KV_WIDTH), False)
    blk = lax.broadcasted_iota(jnp.int32, (rows, nb), 1)
    sel = _topk_lanes(gate, blk < qi, min(MOBA_TOPK, nb))
    sel_bias = jnp.where(sel, 0.0, NEG_INF)

    dist_own = t_q - off_k
    alibi_own = -slope * dist_own.astype(F32)

    def block_rows(j):
        return pl.ds(j * tq if isinstance(j, int) else pl.multiple_of(j * tq, tq), tq)

    def keys(j):
        return kv_ref[block_rows(j), :MOBA_KV_WIDTH]

    def values(j):
        return kv_ref[block_rows(j), MOBA_KV_WIDTH:]

    s = jnp.where(dist_own >= 0, _dot_nt(q, keys(qi)) * scale + alibi_own, NEG_INF)
    s_ref[own_slot] = s
    m_ref[...] = _slab_max(jnp.full(m_ref.shape, NEG_INF, F32), s)
    for j in range(nb - 1):
        @pl.when(j < qi)
        def _():
            shift = sel_bias[:, j:j + 1] - slope * ((qi - j) * tq).astype(F32)
            sj = _dot_nt(q, keys(j)) * scale + alibi_own + shift
            s_ref[j] = sj
            m_ref[...] = _slab_max(m_ref[...], sj)

    m_ref[...] = _row_max(m_ref[...])
    l_ref[...] = jnp.zeros(l_ref.shape, F32)
    acc_ref[...] = jnp.zeros(acc_ref.shape, F32)
    _exp_accumulate(s_ref[own_slot], m_ref[...], 1.0, values(qi), l_ref, acc_ref)
    for j in range(nb - 1):
        @pl.when(j < qi)
        def _():
            _exp_accumulate(s_ref[j], m_ref[...], 1.0, values(j), l_ref, acc_ref)

    o = acc_ref[...] / jnp.sum(l_ref[...], axis=1, keepdims=True)
    pieces = []
    for h in range(MOBA_HEADS):
        g = h // MOBA_GROUP
        pieces.append(o[h * tq:(h + 1) * tq, g * MOBA_HEAD_DIM:(g + 1) * MOBA_HEAD_DIM])
    o_ref[...] = jnp.concatenate(pieces, axis=1).astype(o_ref.dtype)


def _moba_prompt(qm, mkv, kmean, slope_rows, *, batch, seq):
    tq = MOBA_BLOCK
    nb = seq // tq
    rows = MOBA_HEADS * tq
    qm4 = qm.reshape(MOBA_HEADS, batch, seq, MOBA_KV_WIDTH)
    mkv3 = mkv.reshape(batch, seq, 2 * MOBA_KV_WIDTH)
    kmean4 = kmean.reshape(batch, nb, 1, MOBA_KV_WIDTH)
    return pl.pallas_call(
        functools.partial(_moba_prompt_kernel, nb=nb),
        grid=(batch, nb),
        in_specs=[
            pl.BlockSpec((MOBA_HEADS, None, tq, MOBA_KV_WIDTH), lambda b, i: (0, b, i, 0)),
            pl.BlockSpec((None, seq, 2 * MOBA_KV_WIDTH), lambda b, i: (b, 0, 0)),
            pl.BlockSpec((None, nb, 1, MOBA_KV_WIDTH), lambda b, i: (b, 0, 0, 0)),
            _const_spec((rows, 1)),
        ],
        out_specs=pl.BlockSpec((tq, BRANCH_WIDTH), lambda b, i: (b * nb + i, 0)),
        out_shape=jax.ShapeDtypeStruct((batch * seq, BRANCH_WIDTH), BF16),
        scratch_shapes=[pltpu.VMEM((nb, rows, tq), F32), pltpu.VMEM((rows, LANES), F32),
                        pltpu.VMEM((rows, LANES), F32), pltpu.VMEM((rows, MOBA_KV_WIDTH), F32)],
        compiler_params=pltpu.CompilerParams(
            dimension_semantics=("arbitrary", "arbitrary"), vmem_limit_bytes=VMEM_LIMIT),
        name="moba_prompt",
    )(qm4, mkv3, kmean4, slope_rows)


def _page_copies(layer, hbm_refs, bufs, sems, page, slot, p):
    return [pltpu.make_async_copy(hbm.at[layer, page], buf.at[slot, p], sems.at[i, slot])
            for i, (hbm, buf) in enumerate(zip(hbm_refs, bufs))]


def _prefetch_and_wait(pt_ref, layer, n_pages, hbm_refs, bufs, sems):
    b = pl.program_id(0)
    slot = b % 2
    copies = functools.partial(_page_copies, layer, hbm_refs, bufs, sems)

    def start_fetch(request, into):
        def body(p, carry):
            for cp in copies(pt_ref[request, p], into, p):
                cp.start()
            return carry
        lax.fori_loop(0, n_pages, body, 0, unroll=DMA_ISSUE_UNROLL)

    @pl.when(b == 0)
    def _():
        start_fetch(0, 0)

    @pl.when(b + 1 < pl.num_programs(0))
    def _():
        start_fetch(b + 1, 1 - slot)

    for p in range(n_pages):
        for cp in copies(0, slot, p):
            cp.wait()
    return slot


def _mla_sample_kernel(pt_ref, q_ref, nckv_ref, nkr_ref, ckv_hbm, krt_hbm, o_ref,
                       ckv_buf, krt_buf, sems, m_ref, l_ref, acc_ref, *, layer, dec_seq, n_pages, scale):
    slot = _prefetch_and_wait(pt_ref, layer, n_pages, (ckv_hbm, krt_hbm), (ckv_buf, krt_buf), sems)
    rows = MLA_HEADS * dec_seq
    ch = MLA_CHUNK_PAGES
    ck = ch * PAGE_SIZE
    q = q_ref[...].reshape(rows, QK_PAD)
    q_lat = q[:, :KV_LORA].astype(BF16)
    q_rope = q[:, KV_LORA:KV_LORA + QK_ROPE].astype(BF16)

    kn = nckv_ref[...].astype(BF16)
    s_new = _dot_nt(q_lat, kn) + _dot_nt(q_rope, nkr_ref[...].astype(BF16))
    t_q = lax.broadcasted_iota(jnp.int32, (MLA_HEADS, dec_seq, dec_seq), 1).reshape(rows, dec_seq)
    t_k = lax.broadcasted_iota(jnp.int32, (rows, dec_seq), 1)
    s_new = jnp.where(t_k <= t_q, s_new, NEG_INF)
    m_new = jnp.max(s_new, axis=1, keepdims=True)
    p_new = _exp_scaled(s_new - m_new, scale)
    l_new = jnp.sum(p_new, axis=1, keepdims=True)
    acc_new = _dot(p_new.astype(BF16), kn)

    n_chunks = n_pages // ch

    def score(c):
        kb = ckv_buf[slot, c * ch:(c + 1) * ch].reshape(ck, KV_LORA).astype(BF16)
        krt = jnp.concatenate([krt_buf[slot, c * ch + i] for i in range(ch)], axis=1).astype(BF16)
        return kb, _dot_nt(q_lat, kb) + _dot(q_rope, krt)

    nxt = score(0)
    for c in range(n_chunks):
        kb, s = nxt
        if c + 1 < n_chunks:
            nxt = score(c + 1)
        mc = _row_max(_slab_max(jnp.full((rows, LANES), NEG_INF, F32), s))
        ps = [_exp_scaled(sl - mc, scale) for sl in _lane_slabs(s)]
        m_ref[c] = mc
        l_ref[c] = functools.reduce(operator.add, ps)
        acc_ref[c] = _dot(jnp.concatenate(ps, axis=1).astype(BF16), kb)

    m = functools.reduce(jnp.maximum, [m_ref[c] for c in range(n_chunks)],
                         jnp.broadcast_to(m_new, (rows, LANES)))
    l_lanes = jnp.zeros((rows, LANES), F32)
    acc = [jnp.zeros((rows, LANES), F32)] * (KV_LORA // LANES)
    for c in range(n_chunks):
        w = _exp_scaled(m_ref[c] - m, scale)
        l_lanes = l_lanes + w * l_ref[c]
        acc = [a + w * sl for a, sl in zip(acc, _lane_slabs(acc_ref[c]))]
    w_new = _exp_scaled(m_new - m[:, :1], scale)
    l = jnp.sum(l_lanes, axis=1, keepdims=True) + w_new * l_new
    o = (jnp.concatenate(acc, axis=1) + w_new * acc_new) / l
    o_ref[...] = o.reshape(MLA_HEADS, dec_seq, KV_LORA)


def _mla_sample(page_table, qf, new_ckv, new_kr, cache_ckv, cache_krope_t, layer, *, dec_batch, dec_seq):
    n_pages = page_table.shape[1]
    n_chunks = n_pages // MLA_CHUNK_PAGES
    rows = MLA_HEADS * dec_seq
    scale = (QK_NOPE + QK_ROPE) ** -0.5
    in_specs = [
        pl.BlockSpec((MLA_HEADS, dec_seq, QK_PAD), lambda b, pt: (0, b, 0)),
        pl.BlockSpec((dec_seq, KV_LORA), lambda b, pt: (b, 0)),
        pl.BlockSpec((dec_seq, QK_ROPE), lambda b, pt: (b, 0)),
        pl.BlockSpec(memory_space=pl.ANY),
        pl.BlockSpec(memory_space=pl.ANY),
    ]
    return pl.pallas_call(
        functools.partial(_mla_sample_kernel, layer=layer, dec_seq=dec_seq, n_pages=n_pages, scale=scale),
        grid_spec=pltpu.PrefetchScalarGridSpec(
            num_scalar_prefetch=1, grid=(dec_batch,), in_specs=in_specs,
            out_specs=pl.BlockSpec((MLA_HEADS, dec_seq, KV_LORA), lambda b, pt: (0, b, 0)),
            scratch_shapes=[pltpu.VMEM((2, n_pages, PAGE_SIZE, KV_LORA), F32),
                            pltpu.VMEM((2, n_pages, QK_ROPE, PAGE_SIZE), F32),
                            pltpu.SemaphoreType.DMA((2, 2)),
                            pltpu.VMEM((n_chunks, rows, LANES), F32),
                            pltpu.VMEM((n_chunks, rows, LANES), F32),
                            pltpu.VMEM((n_chunks, rows, KV_LORA), F32)]),
        out_shape=jax.ShapeDtypeStruct((MLA_HEADS, dec_batch * dec_seq, KV_LORA), F32),
        compiler_params=pltpu.CompilerParams(
            dimension_semantics=("arbitrary",), vmem_limit_bytes=VMEM_LIMIT),
        name="mla_sample",
    )(page_table, qf, new_ckv, new_kr, cache_ckv, cache_krope_t)


def _uv_kernel(o_lat_ref, wuv_ref, o_ref):
    o = _dot(o_lat_ref[0].astype(BF16), wuv_ref[0])
    for h in range(1, MLA_HEADS):
        o = o + _dot(o_lat_ref[h].astype(BF16), wuv_ref[h])
    o_ref[...] = o


def _uv_sample(o_lat, w_uv):
    n = o_lat.shape[1]
    return pl.pallas_call(
        _uv_kernel,
        out_shape=jax.ShapeDtypeStruct((n, BRANCH_WIDTH), F32),
        compiler_params=pltpu.CompilerParams(vmem_limit_bytes=VMEM_LIMIT),
        name="uv_sample",
    )(o_lat, w_uv)


def _moba_sample_kernel(pt_ref, q_ref, nk_ref, nv_ref, slope_ref, kt_hbm, vt_hbm, o_ref,
                        kt_buf, vt_buf, sems, m_ref, l_ref, acc_ref,
                        *, layer, dec_seq, n_pages, past_len):
    slot = _prefetch_and_wait(pt_ref, layer, n_pages, (kt_hbm, vt_hbm), (kt_buf, vt_buf), sems)
    rows = MOBA_HEADS * dec_seq
    scale = MOBA_HEAD_DIM ** -0.5
    nb_past = n_pages // PAGES_PER_BLOCK
    wide = (rows, LANES)
    qf = q_ref[...].reshape(rows, MOBA_KV_WIDTH)
    q = qf.astype(BF16)
    slope = slope_ref[...]
    t_q = past_len + lax.broadcasted_iota(jnp.int32, (MOBA_HEADS, dec_seq, LANES), 1).reshape(wide)
    off_k = lax.broadcasted_iota(jnp.int32, (1, MOBA_BLOCK), 1).astype(F32)
    alibi_off = slope[:, :1] * off_k
    lane = lax.broadcasted_iota(jnp.int32, wide, 1)
    lane_km = lax.broadcasted_iota(jnp.int32, (MOBA_KV_WIDTH, LANES), 1)

    def block_t(buf, j):
        return jnp.concatenate([buf[slot, PAGES_PER_BLOCK * j + i] for i in range(PAGES_PER_BLOCK)], axis=1)

    km = jnp.zeros((MOBA_KV_WIDTH, LANES), F32)
    for j in range(nb_past):
        kmean = jnp.sum(block_t(kt_buf, j), axis=1, keepdims=True) / MOBA_BLOCK
        km = jnp.where(lane_km == j, kmean, km)
    gate = _gate_scores(qf, km, True)
    sel = _topk_lanes(gate, lane < nb_past, min(MOBA_TOPK, nb_past))

    def score(j):
        shift = -slope * (t_q - j * MOBA_BLOCK).astype(F32)
        s = _dot(q, block_t(kt_buf, j).astype(BF16)) * scale + alibi_off
        return [sl + shift for sl in _lane_slabs(s)]

    nxt = score(0)
    for j in range(nb_past):
        s = nxt
        if j + 1 < nb_past:
            nxt = score(j + 1)
        mj = _row_max(functools.reduce(jnp.maximum, s))
        ps = [jnp.exp(sl - mj) for sl in s]
        lj = jnp.sum(functools.reduce(operator.add, ps), axis=1, keepdims=True)
        m_ref[j] = mj
        l_ref[j] = jnp.broadcast_to(lj, wide)
        acc_ref[j] = _dot_nt(block_t(vt_buf, j).astype(BF16), jnp.concatenate(ps, axis=1).astype(BF16))

    kn = nk_ref[...].astype(BF16)
    t_k = lax.broadcasted_iota(jnp.int32, (1, dec_seq), 1)
    dist = (t_q[:, :1] - past_len) - t_k
    s = _dot_nt(q, kn) * scale - slope[:, :1] * dist.astype(F32)
    s = jnp.where(dist >= 0, s, NEG_INF)
    m_own = jnp.max(s, axis=1, keepdims=True)
    p = jnp.exp(s - m_own)
    l_own = jnp.sum(p, axis=1, keepdims=True)
    acc_own = _dot(p.astype(BF16), nv_ref[...].astype(BF16))

    m_blk = jnp.zeros(wide, F32)
    l_blk = jnp.zeros(wide, F32)
    for j in range(nb_past):
        m_blk = jnp.where(lane == j, m_ref[j], m_blk)
        l_blk = jnp.where(lane == j, l_ref[j], l_blk)
    m_tot = jnp.maximum(m_own, jnp.max(jnp.where(sel, m_blk, NEG_INF), axis=1, keepdims=True))
    w_blk = jnp.where(sel, jnp.exp(m_blk - m_tot), 0.0)
    w_own = jnp.exp(m_own - m_tot)
    l_tot = jnp.sum(w_blk * l_blk, axis=1, keepdims=True) + w_own * l_own
    w_t = jnp.concatenate([w_blk, jnp.zeros((LANES - rows, LANES), F32)], axis=0).T[:, :rows]
    acc_t = jnp.zeros((MOBA_KV_WIDTH, rows), F32)
    for j in range(nb_past):
        acc_t = acc_t + w_t[j:j + 1, :] * acc_ref[j]
    acc_t = jnp.concatenate([acc_t, jnp.zeros((MOBA_KV_WIDTH, LANES - rows), F32)], axis=1)
    o = (acc_t.T[:rows, :] + w_own * acc_own) / l_tot
    pieces = []
    for h in range(MOBA_HEADS):
        g = h // MOBA_GROUP
        pieces.append(o[h * dec_seq:(h + 1) * dec_seq, g * MOBA_HEAD_DIM:(g + 1) * MOBA_HEAD_DIM])
    o_ref[...] = jnp.concatenate(pieces, axis=1)


def _moba_sample(page_table, qm, new_k, new_v, slope_rows, cache_kt, cache_vt, layer, *, dec_batch, dec_seq):
    n_pages = page_table.shape[1]
    past_len = n_pages * PAGE_SIZE
    nb_past = past_len // MOBA_BLOCK
    rows = MOBA_HEADS * dec_seq
    in_specs = [
        pl.BlockSpec((MOBA_HEADS, dec_seq, MOBA_KV_WIDTH), lambda b, pt: (0, b, 0)),
        pl.BlockSpec((dec_seq, MOBA_KV_WIDTH), lambda b, pt: (b, 0)),
        pl.BlockSpec((dec_seq, MOBA_KV_WIDTH), lambda b, pt: (b, 0)),
        pl.BlockSpec((rows, LANES), lambda b, pt: (0, 0)),
        pl.BlockSpec(memory_space=pl.ANY),
        pl.BlockSpec(memory_space=pl.ANY),
    ]
    page_buf = pltpu.VMEM((2, n_pages, MOBA_KV_WIDTH, PAGE_SIZE), F32)
    stat = pltpu.VMEM((nb_past, rows, LANES), F32)
    return pl.pallas_call(
        functools.partial(_moba_sample_kernel, layer=layer, dec_seq=dec_seq, n_pages=n_pages,
                          past_len=past_len),
        grid_spec=pltpu.PrefetchScalarGridSpec(
            num_scalar_prefetch=1, grid=(dec_batch,), in_specs=in_specs,
            out_specs=pl.BlockSpec((dec_seq, BRANCH_WIDTH), lambda b, pt: (b, 0)),
            scratch_shapes=[page_buf, page_buf, pltpu.SemaphoreType.DMA((2, 2)), stat, stat,
                            pltpu.VMEM((nb_past, MOBA_KV_WIDTH, rows), F32)]),
        out_shape=jax.ShapeDtypeStruct((dec_batch * dec_seq, BRANCH_WIDTH), F32),
        compiler_params=pltpu.CompilerParams(
            dimension_semantics=("arbitrary",), vmem_limit_bytes=VMEM_LIMIT),
        name="moba_sample",
    )(page_table, qm, new_k, new_v, slope_rows, cache_kt, cache_vt)


def _ffn_kernel(x_ref, oa_ref, ob_ref, g_attn_ref, wgate_ref, wbr_ref, wo_ref, g_ffn_ref, wgu_ref, wdn_ref,
                g_fin_ref, out_ref, *, final, n_chunks):
    x = x_ref[...]
    xn = _rms(x, g_attn_ref[...]).astype(BF16)
    merged = (jax.nn.sigmoid(_dot(xn, wgate_ref[:, :D_MODEL])) * _dot(oa_ref[...].astype(BF16), wbr_ref[0])
              + jax.nn.sigmoid(_dot(xn, wgate_ref[:, D_MODEL:])) * _dot(ob_ref[...].astype(BF16), wbr_ref[1]))
    h = x + _dot(merged.astype(BF16), wo_ref[...])
    hn = _rms(h, g_ffn_ref[...]).astype(BF16)
    cw = FFN_HIDDEN // n_chunks
    acc = h
    for c in range(n_chunks):
        a = _dot(hn, wgu_ref[:, c * cw:(c + 1) * cw])
        u = _dot(hn, wgu_ref[:, FFN_HIDDEN + c * cw:FFN_HIDDEN + (c + 1) * cw])
        act = (jax.nn.silu(a) * u).astype(BF16)
        acc = acc + _dot(act, wdn_ref[c * cw:(c + 1) * cw, :])
    out_ref[...] = _rms(acc, g_fin_ref[...]) if final else acc


def _ffn(x, o_a, o_b, w, g_final, *, final, tm, n_chunks=2):
    n = x.shape[0]
    row = lambda i: (i, 0)
    return pl.pallas_call(
        functools.partial(_ffn_kernel, final=final, n_chunks=n_chunks),
        grid=(n // tm,),
        in_specs=[
            pl.BlockSpec((tm, D_MODEL), row),
            pl.BlockSpec((tm, BRANCH_WIDTH), row),
            pl.BlockSpec((tm, BRANCH_WIDTH), row),
            _const_spec((1, D_MODEL)),
            _const_spec((D_MODEL, 2 * D_MODEL)),
            _const_spec((2, BRANCH_WIDTH, D_MODEL)),
            _const_spec((D_MODEL, D_MODEL)),
            _const_spec((1, D_MODEL)),
            _const_spec((D_MODEL, 2 * FFN_HIDDEN)),
            _const_spec((FFN_HIDDEN, D_MODEL)),
            _const_spec((1, D_MODEL)),
        ],
        out_specs=pl.BlockSpec((tm, D_MODEL), row),
        out_shape=jax.ShapeDtypeStruct((n, D_MODEL), F32),
        compiler_params=pltpu.CompilerParams(
            dimension_semantics=("arbitrary",), vmem_limit_bytes=VMEM_LIMIT),
        name="ffn_final" if final else "ffn",
    )(x, o_a, o_b, w["g_attn"], w["w_gate"], w["w_branch"], w["w_o"], w["g_ffn"], w["w_gu"], w["w_down"],
      g_final)


def _prep_layer_weights(w_in, g_q, g_kv, w_uq, w_uk, w_uv, w_branch, w_o, g_attn, g_ffn, w_gu, w_down):
    s = [0, Q_LORA, Q_LORA + KV_LORA, Q_LORA + KV_LORA + QK_ROPE]
    s.append(s[-1] + MOBA_HEADS * MOBA_HEAD_DIM)
    s.append(s[-1] + MOBA_KV_WIDTH)
    s.append(s[-1] + MOBA_KV_WIDTH)
    c_q, c_kv, k_r, m_q, m_k, m_v, gate = (w_in[:, s[0]:s[1]], w_in[:, s[1]:s[2]], w_in[:, s[2]:s[3]],
                                           w_in[:, s[3]:s[4]], w_in[:, s[4]:s[5]], w_in[:, s[5]:s[6]],
                                           w_in[:, s[6]:])
    m_q = m_q.reshape(D_MODEL, MOBA_KV_HEADS, MOBA_GROUP, MOBA_HEAD_DIM)
    zeros = jnp.zeros_like(m_q[:, 0])
    m_q_bd = jnp.concatenate([
        jnp.concatenate([m_q[:, 0], zeros], axis=-1),
        jnp.concatenate([zeros, m_q[:, 1]], axis=-1)], axis=1).reshape(D_MODEL, MOBA_HEADS * LANES)
    w_in_perm = jnp.concatenate([c_q, c_kv, m_q_bd, m_k, m_v, k_r], axis=1).astype(BF16)

    uq = w_uq.reshape(Q_LORA, MLA_HEADS, QK_NOPE + QK_ROPE)
    w_uqn = jnp.pad(uq[:, :, :QK_NOPE], ((0, 0), (0, 0), (0, LANES - QK_NOPE))).reshape(Q_LORA, MLA_HEADS * LANES)
    w_uqr = jnp.concatenate([uq[:, :, QK_NOPE:QK_NOPE + HALF_ROPE].reshape(Q_LORA, LANES),
                             uq[:, :, QK_NOPE + HALF_ROPE:].reshape(Q_LORA, LANES)], axis=1)
    uk = jnp.transpose(w_uk, (1, 2, 0))
    uk = jnp.pad(uk, ((0, 0), (0, LANES - QK_NOPE), (0, 0)))
    uv = jnp.transpose(w_uv, (1, 0, 2))
    eye = jnp.eye(MLA_HEADS, dtype=w_uv.dtype)
    uv_pad = (uv[:, :, None, :] * eye[:, None, :, None]).reshape(MLA_HEADS, KV_LORA, BRANCH_WIDTH)
    return {
        "w_in": w_in_perm, "w_gate": gate.astype(BF16), "g_q": g_q[None], "g_kv": g_kv[None], "g_attn": g_attn[None], "g_ffn": g_ffn[None],
        "w_uqn": w_uqn.astype(BF16), "w_uqr": w_uqr.astype(BF16), "w_uk": uk.astype(BF16),
        "w_uv": uv_pad.astype(BF16), "w_branch": w_branch.astype(BF16), "w_o": w_o.astype(BF16),
        "w_gu": w_gu.astype(BF16), "w_down": w_down.astype(BF16),
    }


def _rope_tables(pos):
    inv = ROPE_THETA ** (-jnp.arange(HALF_ROPE, dtype=F32) / HALF_ROPE)
    ang = pos.astype(F32)[:, None] * inv[None, :]
    reps = LANES // HALF_ROPE
    return jnp.tile(jnp.cos(ang), (1, reps)), jnp.tile(jnp.sin(ang), (1, reps))


def _slope_rows(tokens_per_head, width):
    slopes = 2.0 ** (-8.0 * jnp.arange(1, MOBA_HEADS + 1, dtype=F32) / MOBA_HEADS)
    return jnp.broadcast_to(jnp.repeat(slopes, tokens_per_head)[:, None], (MOBA_HEADS * tokens_per_head, width))


def kernel(x_prompt, x_sample, cache_ckv, cache_krope, cache_k, cache_v, page_table, w_in, g_q, g_kv, w_uq, w_uk, w_uv, w_branch, w_o, g_attn, g_ffn, w_gu, w_down, g_final):
    batch, seq, _ = x_prompt.shape
    dec_batch, dec_seq, _ = x_sample.shape
    depth, n_phys = cache_k.shape[:2]
    past_len = page_table.shape[1] * PAGE_SIZE
    n_p, n_s = batch * seq, dec_batch * dec_seq
    tm_p = min(512, seq)
    tm_s = min(512, n_s)

    cache_krope_t = jnp.transpose(cache_krope, (0, 1, 3, 2))
    cache_kt = jnp.transpose(cache_k, (0, 1, 3, 4, 2)).reshape(depth, n_phys, MOBA_KV_WIDTH, PAGE_SIZE)
    cache_vt = jnp.transpose(cache_v, (0, 1, 3, 4, 2)).reshape(depth, n_phys, MOBA_KV_WIDTH, PAGE_SIZE)

    cos_p, sin_p = _rope_tables(jnp.arange(seq, dtype=jnp.int32))
    pos_s = past_len + jnp.arange(dec_seq, dtype=jnp.int32)
    cos_s, sin_s = _rope_tables(jnp.tile(pos_s, tm_s // dec_seq))
    slope_p = _slope_rows(MOBA_BLOCK, 1)
    slope_s = _slope_rows(dec_seq, LANES)
    g_fin = g_final[None]

    hp = x_prompt.reshape(n_p, D_MODEL)
    hs = x_sample.reshape(n_s, D_MODEL)
    rows_p, rows_s = [], []
    for l in range(depth):
        w = _prep_layer_weights(w_in[l], g_q[l], g_kv[l], w_uq[l], w_uk[l], w_uv[l], w_branch[l], w_o[l],
                                g_attn[l], g_ffn[l], w_gu[l], w_down[l])
        final = l == depth - 1

        ckv, kr, mk, mv, qf, qm, kf, mkv, kmean = _proj(hp, cos_p, sin_p, w, prompt=True, tm=tm_p)
        o_a = _mla_prompt(qf, kf, w["w_uv"], batch=batch, seq=seq, tq=min(128, seq), tk=min(256, seq))
        o_b = _moba_prompt(qm, mkv, kmean, slope_p, batch=batch, seq=seq)
        hp = _ffn(hp, o_a, o_b, w, g_fin, final=final, tm=tm_p)
        rows_p.append((ckv, kr, mk, mv))

        ckv, kr, mk, mv, qf, qm = _proj(hs, cos_s, sin_s, w, prompt=False, tm=tm_s)
        o_lat = _mla_sample(page_table, qf, ckv, kr, cache_ckv, cache_krope_t, l,
                            dec_batch=dec_batch, dec_seq=dec_seq)
        o_a = _uv_sample(o_lat, w["w_uv"])
        o_b = _moba_sample(page_table, qm, mk, mv, slope_s, cache_kt, cache_vt, l,
                           dec_batch=dec_batch, dec_seq=dec_seq)
        hs = _ffn(hs, o_a, o_b, w, g_fin, final=final, tm=tm_s)
        rows_s.append((ckv, kr, mk, mv))

    def stack(rows, i, shape):
        return jnp.stack([r[i] for r in rows]).reshape((depth,) + shape)

    kv_shape = (MOBA_KV_HEADS, MOBA_HEAD_DIM)
    return (hp.reshape(batch, seq, D_MODEL),
            hs.reshape(dec_batch, dec_seq, D_MODEL),
            stack(rows_p, 0, (batch, seq, KV_LORA)),
            stack(rows_p, 1, (batch, seq, QK_ROPE)),
            stack(rows_p, 2, (batch, seq) + kv_shape),
            stack(rows_p, 3, (batch, seq) + kv_shape),
            stack(rows_s, 0, (dec_batch, dec_seq, KV_LORA)),
            stack(rows_s, 1, (dec_batch, dec_seq, QK_ROPE)),
            stack(rows_s, 2, (dec_batch, dec_seq) + kv_shape),
            stack(rows_s, 3, (dec_batch, dec_seq) + kv_shape))
```

```python
import functools
import math
import operator

import jax
import jax.numpy as jnp
from jax import lax
from jax.experimental import pallas as pl
from jax.experimental.pallas import tpu as pltpu

F32 = jnp.float32
BF16 = jnp.bfloat16

D_MODEL = 1024
PAGE_SIZE = 128
MLA_HEADS = 8
Q_LORA = 384
KV_LORA = 256
QK_NOPE = 64
QK_ROPE = 32
V_HEAD = 64
ROPE_THETA = 10000.0
MOBA_HEADS = 8
MOBA_KV_HEADS = 2
MOBA_GROUP = MOBA_HEADS // MOBA_KV_HEADS
MOBA_HEAD_DIM = 64
MOBA_BLOCK = 256
MOBA_TOPK = 3
BRANCH_WIDTH = 512
FFN_HIDDEN = 2816
RMS_EPS = 1e-6

LANES = 128
QK_PAD = KV_LORA + LANES
MOBA_KV_WIDTH = MOBA_KV_HEADS * MOBA_HEAD_DIM
HALF_ROPE = QK_ROPE // 2
PAGES_PER_BLOCK = MOBA_BLOCK // PAGE_SIZE

_OFF_CQ = 0
_OFF_CKV = _OFF_CQ + Q_LORA
_OFF_MQ = _OFF_CKV + KV_LORA
_OFF_MK = _OFF_MQ + MOBA_HEADS * LANES
_OFF_MV = _OFF_MK + MOBA_KV_WIDTH
_OFF_KR = _OFF_MV + MOBA_KV_WIDTH
_D_IN_PERM = _OFF_KR + QK_ROPE

VMEM_LIMIT = 56 * 1024 * 1024
MLA_CHUNK_PAGES = 8
DMA_ISSUE_UNROLL = 4
NEG_INF = float("-inf")
LOG2_E = 1.4426950408889634


def _rms(x, g):
    return x * lax.rsqrt(jnp.mean(x * x, axis=-1, keepdims=True) + RMS_EPS) * g


def _dot(a, b):
    return jnp.dot(a, b, preferred_element_type=F32)


def _dot_nt(a, b):
    return lax.dot_general(a, b, (((1,), (1,)), ((), ())), preferred_element_type=F32)


def _const_spec(shape):
    return pl.BlockSpec(shape, lambda *_: (0,) * len(shape), pipeline_mode=pl.Buffered(1))


def _lane_slabs(x):
    return [x[:, j * LANES:(j + 1) * LANES] for j in range(x.shape[1] // LANES)]


def _slab_max(m, s):
    return functools.reduce(jnp.maximum, _lane_slabs(s), m)


def _row_max(m_slab):
    return jnp.broadcast_to(jnp.max(m_slab, axis=1, keepdims=True), m_slab.shape)


def _proj_kernel(x_ref, cos_ref, sin_ref, g_attn_ref, w_in_ref, g_q_ref, g_kv_ref,
                 w_uqn_ref, w_uqr_ref, w_uk_ref, *out_refs, prompt):
    if prompt:
        (ckv_ref, kr_ref, mk_ref, mv_ref, qf_ref, qm_ref,
         kf_ref, mkv_ref, kmean_ref) = out_refs
    else:
        ckv_ref, kr_ref, mk_ref, mv_ref, qf_ref, qm_ref = out_refs
    tm = x_ref.shape[0]
    hn = _rms(x_ref[...], g_attn_ref[...]).astype(BF16)

    def seg(lo, hi):
        return _dot(hn, w_in_ref[:, lo:hi])

    cos = cos_ref[...]
    sin = sin_ref[...]

    ckv = _rms(seg(_OFF_CKV, _OFF_MQ), g_kv_ref[...])
    ckv_ref[...] = ckv
    kr = seg(_OFF_KR, _D_IN_PERM)
    kr_swapped = jnp.concatenate([kr[:, HALF_ROPE:], kr[:, :HALF_ROPE]], axis=1)
    lane32 = lax.broadcasted_iota(jnp.int32, (tm, QK_ROPE), 1)
    sin_signed = jnp.where(lane32 < HALF_ROPE, -sin[:, :QK_ROPE], sin[:, :QK_ROPE])
    kr = kr * cos[:, :QK_ROPE] + kr_swapped * sin_signed
    kr_ref[...] = kr

    mk = seg(_OFF_MK, _OFF_MV)
    mv = seg(_OFF_MV, _OFF_KR)
    mk_ref[...] = mk
    mv_ref[...] = mv

    mq = seg(_OFF_MQ, _OFF_MK)
    for h in range(MOBA_HEADS):
        qm_ref[h] = mq[:, h * LANES:(h + 1) * LANES].astype(qm_ref.dtype)

    cq = _rms(seg(_OFF_CQ, _OFF_CKV), g_q_ref[...]).astype(BF16)
    q_nope = _dot(cq, w_uqn_ref[...]).astype(BF16)
    q_rot = _dot(cq, w_uqr_ref[...])
    r1, r2 = q_rot[:, :LANES], q_rot[:, LANES:]
    o1 = r1 * cos - r2 * sin
    o2 = r2 * cos + r1 * sin
    lane = lax.broadcasted_iota(jnp.int32, (tm, LANES), 1)
    for h in range(MLA_HEADS):
        q_lat = _dot(q_nope[:, h * LANES:(h + 1) * LANES], w_uk_ref[h])
        shift_a = (LANES - HALF_ROPE * h) % LANES
        shift_b = (HALF_ROPE - HALF_ROPE * h) % LANES
        a = o1 if shift_a == 0 else pltpu.roll(o1, shift_a, axis=1)
        b = o2 if shift_b == 0 else pltpu.roll(o2, shift_b, axis=1)
        rope = jnp.where(lane < HALF_ROPE, a, jnp.where(lane < QK_ROPE, b, 0.0))
        qf_ref[h, :, :KV_LORA] = q_lat.astype(qf_ref.dtype)
        qf_ref[h, :, KV_LORA:] = rope.astype(qf_ref.dtype)

    if prompt:
        kf_ref[:, :KV_LORA] = ckv.astype(BF16)
        kr_pad = jnp.concatenate([kr, jnp.zeros((tm, LANES - QK_ROPE), F32)], axis=1)
        kf_ref[:, KV_LORA:] = kr_pad.astype(BF16)
        mkv_ref[:, :MOBA_KV_WIDTH] = mk.astype(BF16)
        mkv_ref[:, MOBA_KV_WIDTH:] = mv.astype(BF16)
        for j in range(tm // MOBA_BLOCK):
            kmean_ref[j] = jnp.mean(mk[j * MOBA_BLOCK:(j + 1) * MOBA_BLOCK], axis=0, keepdims=True)


def _proj(x, cos_tab, sin_tab, w, *, prompt, tm):
    n = x.shape[0]
    n_tab = cos_tab.shape[0] // tm
    grid = (n // tm,)
    row = lambda i: (i, 0)
    q_dtype = BF16 if prompt else F32
    out_shape = [
        jax.ShapeDtypeStruct((n, KV_LORA), F32),
        jax.ShapeDtypeStruct((n, QK_ROPE), F32),
        jax.ShapeDtypeStruct((n, MOBA_KV_WIDTH), F32),
        jax.ShapeDtypeStruct((n, MOBA_KV_WIDTH), F32),
        jax.ShapeDtypeStruct((MLA_HEADS, n, QK_PAD), q_dtype),
        jax.ShapeDtypeStruct((MOBA_HEADS, n, MOBA_KV_WIDTH), F32),
    ]
    out_specs = [
        pl.BlockSpec((tm, KV_LORA), row),
        pl.BlockSpec((tm, QK_ROPE), row),
        pl.BlockSpec((tm, MOBA_KV_WIDTH), row),
        pl.BlockSpec((tm, MOBA_KV_WIDTH), row),
        pl.BlockSpec((MLA_HEADS, tm, QK_PAD), lambda i: (0, i, 0)),
        pl.BlockSpec((MOBA_HEADS, tm, MOBA_KV_WIDTH), lambda i: (0, i, 0)),
    ]
    if prompt:
        nblk = tm // MOBA_BLOCK
        out_shape += [
            jax.ShapeDtypeStruct((n, QK_PAD), BF16),
            jax.ShapeDtypeStruct((n, 2 * MOBA_KV_WIDTH), BF16),
            jax.ShapeDtypeStruct((n // MOBA_BLOCK, 1, MOBA_KV_WIDTH), F32),
        ]
        out_specs += [
            pl.BlockSpec((tm, QK_PAD), row),
            pl.BlockSpec((tm, 2 * MOBA_KV_WIDTH), row),
            pl.BlockSpec((nblk, 1, MOBA_KV_WIDTH), lambda i: (i, 0, 0)),
        ]
    tab = lambda i: (i % n_tab, 0)
    in_specs = [
        pl.BlockSpec((tm, D_MODEL), row),
        pl.BlockSpec((tm, LANES), tab),
        pl.BlockSpec((tm, LANES), tab),
        _const_spec((1, D_MODEL)),
        _const_spec((D_MODEL, _D_IN_PERM)),
        _const_spec((1, Q_LORA)),
        _const_spec((1, KV_LORA)),
        _const_spec((Q_LORA, MLA_HEADS * LANES)),
        _const_spec((Q_LORA, 2 * LANES)),
        _const_spec((MLA_HEADS, LANES, KV_LORA)),
    ]
    return pl.pallas_call(
        functools.partial(_proj_kernel, prompt=prompt),
        grid=grid, in_specs=in_specs, out_specs=out_specs, out_shape=out_shape,
        compiler_params=pltpu.CompilerParams(
            dimension_semantics=("arbitrary",), vmem_limit_bytes=VMEM_LIMIT),
        name="proj_prompt" if prompt else "proj_sample",
    )(x, cos_tab, sin_tab, w["g_attn"], w["w_in"], w["g_q"], w["g_kv"],
      w["w_uqn"], w["w_uqr"], w["w_uk"])


def _exp_scaled(x, scale):
    return jnp.exp(x) if scale == 1.0 else jnp.exp2(x * (scale * LOG2_E))


def _exp_accumulate(s, m, scale, v, l_ref, acc_ref):
    ps = [_exp_scaled(sl - m, scale) for sl in _lane_slabs(s)]
    l_ref[...] += functools.reduce(operator.add, ps)
    acc_ref[...] += _dot(jnp.concatenate(ps, axis=1).astype(BF16), v)


def _mla_prompt_kernel(q_ref, k_ref, wuv_ref, o_ref, s_ref, m_ref, l_ref, acc_ref, *, tq, tk, scale):
    qi = pl.program_id(1)
    rows = MLA_HEADS * tq
    q = q_ref[...].reshape(rows, QK_PAD)
    last = (qi * tq + tq - 1) // tk

    def k_chunk(kc):
        return k_ref[pl.ds(pl.multiple_of(kc * tk, tk), tk), :]

    def scores(kc, masked):
        s = _dot_nt(q, k_chunk(kc))
        if masked:
            q_pos = qi * tq + lax.broadcasted_iota(jnp.int32, (MLA_HEADS, tq, tk), 1).reshape(rows, tk)
            k_pos = kc * tk + lax.broadcasted_iota(jnp.int32, (rows, tk), 1)
            s = jnp.where(k_pos <= q_pos, s, NEG_INF)
        s_ref[kc] = s
        m_ref[...] = _slab_max(m_ref[...], s)

    m_ref[...] = jnp.full(m_ref.shape, NEG_INF, F32)

    def score_body(kc, carry):
        scores(kc, False)
        return carry

    lax.fori_loop(0, last, score_body, 0)
    scores(last, True)

    m_ref[...] = _row_max(m_ref[...])
    l_ref[...] = jnp.zeros(l_ref.shape, F32)
    acc_ref[...] = jnp.zeros(acc_ref.shape, F32)

    def value_body(kc, carry):
        _exp_accumulate(s_ref[kc], m_ref[...], scale, k_chunk(kc)[:, :KV_LORA], l_ref, acc_ref)
        return carry

    lax.fori_loop(0, last + 1, value_body, 0)

    o_lat = (acc_ref[...] / jnp.sum(l_ref[...], axis=1, keepdims=True)).astype(BF16)
    o = _dot(o_lat[:tq], wuv_ref[0])
    for h in range(1, MLA_HEADS):
        o = o + _dot(o_lat[h * tq:(h + 1) * tq], wuv_ref[h])
    o_ref[...] = o.astype(o_ref.dtype)


def _mla_prompt(qf, kf, w_uv, *, batch, seq, tq, tk):
    nq = seq // tq
    rows = MLA_HEADS * tq
    qf4 = qf.reshape(MLA_HEADS, batch, seq, QK_PAD)
    kf3 = kf.reshape(batch, seq, QK_PAD)
    scale = (QK_NOPE + QK_ROPE) ** -0.5
    return pl.pallas_call(
        functools.partial(_mla_prompt_kernel, tq=tq, tk=tk, scale=scale),
        grid=(batch, nq),
        in_specs=[
            pl.BlockSpec((MLA_HEADS, None, tq, QK_PAD), lambda b, i: (0, b, i, 0)),
            pl.BlockSpec((None, seq, QK_PAD), lambda b, i: (b, 0, 0)),
            _const_spec((MLA_HEADS, KV_LORA, BRANCH_WIDTH)),
        ],
        out_specs=pl.BlockSpec((tq, BRANCH_WIDTH), lambda b, i: (b * nq + i, 0)),
        out_shape=jax.ShapeDtypeStruct((batch * seq, BRANCH_WIDTH), BF16),
        scratch_shapes=[pltpu.VMEM((seq // tk, rows, tk), F32), pltpu.VMEM((rows, LANES), F32),
                        pltpu.VMEM((rows, LANES), F32), pltpu.VMEM((rows, KV_LORA), F32)],
        compiler_params=pltpu.CompilerParams(
            dimension_semantics=("arbitrary", "arbitrary"), vmem_limit_bytes=VMEM_LIMIT),
        name="mla_prompt",
    )(qf4, kf3, w_uv)


def _gate_scores(qf, kmean, kmean_is_transposed):
    mm = _dot if kmean_is_transposed else _dot_nt
    q_hi = qf.astype(BF16)
    q_lo = (qf - q_hi.astype(F32)).astype(BF16)
    k_hi = kmean.astype(BF16)
    k_lo = (kmean - k_hi.astype(F32)).astype(BF16)
    return mm(q_hi, k_hi) + (mm(q_hi, k_lo) + mm(q_lo, k_hi))


def _prescaled_query(qf, scale):
    if math.frexp(scale)[0] == 0.5:
        return (qf * scale).astype(BF16), 1.0
    return qf.astype(BF16), scale


def _topk_lanes(gate, valid, n_sel):
    nb = gate.shape[1]
    lane = lax.broadcasted_iota(jnp.int32, gate.shape, 1)
    g = jnp.where(valid, gate, NEG_INF)
    picked = jnp.zeros(gate.shape, jnp.bool_)
    for _ in range(n_sel):
        cur = jnp.where(picked, NEG_INF, g)
        best = jnp.max(cur, axis=1, keepdims=True)
        cand = jnp.logical_and(cur == best, jnp.logical_not(picked))
        idx = jnp.min(jnp.where(cand, lane, nb), axis=1, keepdims=True)
        picked = jnp.logical_or(picked, lane == idx)
    return jnp.logical_and(picked, valid)


def _moba_prompt_kernel(q_ref, kv_ref, kmean_ref, slope_ref, o_ref, s_ref, m_ref, l_ref, acc_ref, *, nb):
    tq = MOBA_BLOCK
    qi = pl.program_id(1)
    rows = MOBA_HEADS * tq
    own_slot = nb - 1
    qf = q_ref[...].reshape(rows, MOBA_KV_WIDTH)
    q, scale = _prescaled_query(qf, MOBA_HEAD_DIM ** -0.5)
    slope = slope_ref[...]
    t_q = lax.broadcasted_iota(jnp.int32, (MOBA_HEADS, tq, 1), 1).reshape(rows, 1)
    off_k = lax.broadcasted_iota(jnp.int32, (1, tq), 1)

    gate = _gate_scores(qf, kmean_ref[...].reshape(nb, MOBA_KV_WIDTH), False)
    blk = lax.broadcasted_iota(jnp.int32, (rows, nb), 1)
    sel = _topk_lanes(gate, blk < qi, min(MOBA_TOPK, nb))
    sel_bias = jnp.where(sel, 0.0, NEG_INF)

    dist_own = t_q - off_k
    alibi_own = -slope * dist_own.astype(F32)

    def block_rows(j):
        return pl.ds(j * tq if isinstance(j, int) else pl.multiple_of(j * tq, tq), tq)

    def keys(j):
        return kv_ref[block_rows(j), :MOBA_KV_WIDTH]

    def values(j):
        return kv_ref[block_rows(j), MOBA_KV_WIDTH:]

    def qk(j):
        raw = _dot_nt(q, keys(j))
        return raw if scale == 1.0 else raw * scale

    s = jnp.where(dist_own >= 0, qk(qi) + alibi_own, NEG_INF)
    s_ref[own_slot] = s
    m_ref[...] = _slab_max(jnp.full(m_ref.shape, NEG_INF, F32), s)
    for j in range(nb - 1):
        @pl.when(j < qi)
        def _():
            shift = sel_bias[:, j:j + 1] - slope * ((qi - j) * tq).astype(F32)
            sj = qk(j) + alibi_own + shift
            s_ref[j] = sj
            m_ref[...] = _slab_max(m_ref[...], sj)

    m_ref[...] = _row_max(m_ref[...])
    l_ref[...] = jnp.zeros(l_ref.shape, F32)
    acc_ref[...] = jnp.zeros(acc_ref.shape, F32)
    _exp_accumulate(s_ref[own_slot], m_ref[...], 1.0, values(qi), l_ref, acc_ref)
    for j in range(nb - 1):
        @pl.when(j < qi)
        def _():
            _exp_accumulate(s_ref[j], m_ref[...], 1.0, values(j), l_ref, acc_ref)

    o = acc_ref[...] / jnp.sum(l_ref[...], axis=1, keepdims=True)
    pieces = []
    for h in range(MOBA_HEADS):
        g = h // MOBA_GROUP
        pieces.append(o[h * tq:(h + 1) * tq, g * MOBA_HEAD_DIM:(g + 1) * MOBA_HEAD_DIM])
    o_ref[...] = jnp.concatenate(pieces, axis=1).astype(o_ref.dtype)


def _moba_prompt(qm, mkv, kmean, slope_rows, *, batch, seq):
    tq = MOBA_BLOCK
    nb = seq // tq
    rows = MOBA_HEADS * tq
    qm4 = qm.reshape(MOBA_HEADS, batch, seq, MOBA_KV_WIDTH)
    mkv3 = mkv.reshape(batch, seq, 2 * MOBA_KV_WIDTH)
    kmean4 = kmean.reshape(batch, nb, 1, MOBA_KV_WIDTH)
    return pl.pallas_call(
        functools.partial(_moba_prompt_kernel, nb=nb),
        grid=(batch, nb),
        in_specs=[
            pl.BlockSpec((MOBA_HEADS, None, tq, MOBA_KV_WIDTH), lambda b, i: (0, b, i, 0)),
            pl.BlockSpec((None, seq, 2 * MOBA_KV_WIDTH), lambda b, i: (b, 0, 0)),
            pl.BlockSpec((None, nb, 1, MOBA_KV_WIDTH), lambda b, i: (b, 0, 0, 0)),
            _const_spec((rows, 1)),
        ],
        out_specs=pl.BlockSpec((tq, BRANCH_WIDTH), lambda b, i: (b * nb + i, 0)),
        out_shape=jax.ShapeDtypeStruct((batch * seq, BRANCH_WIDTH), BF16),
        scratch_shapes=[pltpu.VMEM((nb, rows, tq), F32), pltpu.VMEM((rows, LANES), F32),
                        pltpu.VMEM((rows, LANES), F32), pltpu.VMEM((rows, MOBA_KV_WIDTH), F32)],
        compiler_params=pltpu.CompilerParams(
            dimension_semantics=("arbitrary", "arbitrary"), vmem_limit_bytes=VMEM_LIMIT),
        name="moba_prompt",
    )(qm4, mkv3, kmean4, slope_rows)


def _page_copies(layer, hbm_refs, bufs, sems, page, slot, p):
    return [pltpu.make_async_copy(hbm.at[layer, page], buf.at[slot, p], sems.at[i, slot])
            for i, (hbm, buf) in enumerate(zip(hbm_refs, bufs))]


def _page_pipeline(pt_ref, layer, n_pages, hbm_refs, bufs, sems):
    b = pl.program_id(0)
    last = pl.num_programs(0) - 1
    slot = b % 2
    nxt = jnp.minimum(b + 1, last)
    copies = functools.partial(_page_copies, layer, hbm_refs, bufs, sems)

    def wait_all(slot_):
        for p in range(n_pages):
            for cp in copies(0, slot_, p):
                cp.wait()

    @pl.when(b == 0)
    def _():
        def body(p, carry):
            for cp in copies(pt_ref[0, p], 0, p):
                cp.start()
            return carry
        lax.fori_loop(0, n_pages, body, 0, unroll=DMA_ISSUE_UNROLL)

    wait_all(slot)

    def prefetch(pages):
        for p in pages:
            for cp in copies(pt_ref[nxt, p], 1 - slot, p):
                cp.start()

    def drain():
        @pl.when(b == last)
        def _():
            wait_all(1 - slot)

    return slot, prefetch, drain


def _mla_sample_kernel(pt_ref, q_ref, nckv_ref, nkr_ref, ckv_hbm, krt_hbm, o_ref,
                       ckv_buf, krt_buf, sems, m_ref, l_ref, acc_ref, *, layer, dec_seq, n_pages, scale):
    slot, prefetch, drain = _page_pipeline(pt_ref, layer, n_pages, (ckv_hbm, krt_hbm), (ckv_buf, krt_buf), sems)
    rows = MLA_HEADS * dec_seq
    ch = MLA_CHUNK_PAGES
    ck = ch * PAGE_SIZE
    q = q_ref[...].reshape(rows, QK_PAD)
    q_lat = q[:, :KV_LORA].astype(BF16)
    q_rope = q[:, KV_LORA:KV_LORA + QK_ROPE].astype(BF16)

    kn = nckv_ref[...].astype(BF16)
    s_new = _dot_nt(q_lat, kn) + _dot_nt(q_rope, nkr_ref[...].astype(BF16))
    t_q = lax.broadcasted_iota(jnp.int32, (MLA_HEADS, dec_seq, dec_seq), 1).reshape(rows, dec_seq)
    t_k = lax.broadcasted_iota(jnp.int32, (rows, dec_seq), 1)
    s_new = jnp.where(t_k <= t_q, s_new, NEG_INF)
    m_new = jnp.max(s_new, axis=1, keepdims=True)
    p_new = _exp_scaled(s_new - m_new, scale)
    l_new = jnp.sum(p_new, axis=1, keepdims=True)
    acc_new = _dot(p_new.astype(BF16), kn)

    n_chunks = n_pages // ch

    def score(c):
        kb = ckv_buf[slot, c * ch:(c + 1) * ch].reshape(ck, KV_LORA).astype(BF16)
        krt = jnp.concatenate([krt_buf[slot, c * ch + i] for i in range(ch)], axis=1).astype(BF16)
        return kb, _dot_nt(q_lat, kb) + _dot(q_rope, krt)

    nxt = score(0)
    for c in range(n_chunks):
        kb, s = nxt
        if c + 1 < n_chunks:
            nxt = score(c + 1)
        mc = _row_max(_slab_max(jnp.full((rows, LANES), NEG_INF, F32), s))
        ps = [_exp_scaled(sl - mc, scale) for sl in _lane_slabs(s)]
        m_ref[c] = mc
        l_ref[c] = functools.reduce(operator.add, ps)
        acc_ref[c] = _dot(jnp.concatenate(ps, axis=1).astype(BF16), kb)
        prefetch(range(c * ch, (c + 1) * ch))

    m = functools.reduce(jnp.maximum, [m_ref[c] for c in range(n_chunks)],
                         jnp.broadcast_to(m_new, (rows, LANES)))
    l_lanes = jnp.zeros((rows, LANES), F32)
    acc = [jnp.zeros((rows, LANES), F32)] * (KV_LORA // LANES)
    for c in range(n_chunks):
        w = _exp_scaled(m_ref[c] - m, scale)
        l_lanes = l_lanes + w * l_ref[c]
        acc = [a + w * sl for a, sl in zip(acc, _lane_slabs(acc_ref[c]))]
    w_new = _exp_scaled(m_new - m[:, :1], scale)
    l = jnp.sum(l_lanes, axis=1, keepdims=True) + w_new * l_new
    o = (jnp.concatenate(acc, axis=1) + w_new * acc_new) / l
    o_ref[...] = o.reshape(MLA_HEADS, dec_seq, KV_LORA)
    drain()


def _mla_sample(page_table, qf, new_ckv, new_kr, cache_ckv, cache_krope_t, layer, *, dec_batch, dec_seq):
    n_pages = page_table.shape[1]
    n_chunks = n_pages // MLA_CHUNK_PAGES
    rows = MLA_HEADS * dec_seq
    scale = (QK_NOPE + QK_ROPE) ** -0.5
    in_specs = [
        pl.BlockSpec((MLA_HEADS, dec_seq, QK_PAD), lambda b, pt: (0, b, 0)),
        pl.BlockSpec((dec_seq, KV_LORA), lambda b, pt: (b, 0)),
        pl.BlockSpec((dec_seq, QK_ROPE), lambda b, pt: (b, 0)),
        pl.BlockSpec(memory_space=pl.ANY),
        pl.BlockSpec(memory_space=pl.ANY),
    ]
    return pl.pallas_call(
        functools.partial(_mla_sample_kernel, layer=layer, dec_seq=dec_seq, n_pages=n_pages, scale=scale),
        grid_spec=pltpu.PrefetchScalarGridSpec(
            num_scalar_prefetch=1, grid=(dec_batch,), in_specs=in_specs,
            out_specs=pl.BlockSpec((MLA_HEADS, dec_seq, KV_LORA), lambda b, pt: (0, b, 0)),
            scratch_shapes=[pltpu.VMEM((2, n_pages, PAGE_SIZE, KV_LORA), F32),
                            pltpu.VMEM((2, n_pages, QK_ROPE, PAGE_SIZE), F32),
                            pltpu.SemaphoreType.DMA((2, 2)),
                            pltpu.VMEM((n_chunks, rows, LANES), F32),
                            pltpu.VMEM((n_chunks, rows, LANES), F32),
                            pltpu.VMEM((n_chunks, rows, KV_LORA), F32)]),
        out_shape=jax.ShapeDtypeStruct((MLA_HEADS, dec_batch * dec_seq, KV_LORA), F32),
        compiler_params=pltpu.CompilerParams(
            dimension_semantics=("arbitrary",), vmem_limit_bytes=VMEM_LIMIT),
        name="mla_sample",
    )(page_table, qf, new_ckv, new_kr, cache_ckv, cache_krope_t)


def _uv_kernel(o_lat_ref, wuv_ref, o_ref):
    o = _dot(o_lat_ref[0].astype(BF16), wuv_ref[0])
    for h in range(1, MLA_HEADS):
        o = o + _dot(o_lat_ref[h].astype(BF16), wuv_ref[h])
    o_ref[...] = o


def _uv_sample(o_lat, w_uv):
    n = o_lat.shape[1]
    return pl.pallas_call(
        _uv_kernel,
        out_shape=jax.ShapeDtypeStruct((n, BRANCH_WIDTH), F32),
        compiler_params=pltpu.CompilerParams(vmem_limit_bytes=VMEM_LIMIT),
        name="uv_sample",
    )(o_lat, w_uv)


def _moba_sample_kernel(pt_ref, q_ref, nk_ref, nv_ref, slope_ref, kt_hbm, vt_hbm, o_ref,
                        kt_buf, vt_buf, sems, m_ref, l_ref, acc_ref,
                        *, layer, dec_seq, n_pages, past_len):
    slot, prefetch, drain = _page_pipeline(pt_ref, layer, n_pages, (kt_hbm, vt_hbm), (kt_buf, vt_buf), sems)
    rows = MOBA_HEADS * dec_seq
    scale = MOBA_HEAD_DIM ** -0.5
    nb_past = n_pages // PAGES_PER_BLOCK
    wide = (rows, LANES)
    qf = q_ref[...].reshape(rows, MOBA_KV_WIDTH)
    q = qf.astype(BF16)
    slope = slope_ref[...]
    t_q = past_len + lax.broadcasted_iota(jnp.int32, (MOBA_HEADS, dec_seq, LANES), 1).reshape(wide)
    off_k = lax.broadcasted_iota(jnp.int32, (1, MOBA_BLOCK), 1).astype(F32)
    alibi_off = slope[:, :1] * off_k
    lane = lax.broadcasted_iota(jnp.int32, wide, 1)
    lane_km = lax.broadcasted_iota(jnp.int32, (MOBA_KV_WIDTH, LANES), 1)

    def block_t(buf, j):
        return jnp.concatenate([buf[slot, PAGES_PER_BLOCK * j + i] for i in range(PAGES_PER_BLOCK)], axis=1)

    km = jnp.zeros((MOBA_KV_WIDTH, LANES), F32)
    for j in range(nb_past):
        kmean = jnp.sum(block_t(kt_buf, j), axis=1, keepdims=True) / MOBA_BLOCK
        km = jnp.where(lane_km == j, kmean, km)
    gate = _gate_scores(qf, km, True)
    sel = _topk_lanes(gate, lane < nb_past, min(MOBA_TOPK, nb_past))

    def score(j):
        shift = -slope * (t_q - j * MOBA_BLOCK).astype(F32)
        s = _dot(q, block_t(kt_buf, j).astype(BF16)) * scale + alibi_off
        return [sl + shift for sl in _lane_slabs(s)]

    nxt = score(0)
    for j in range(nb_past):
        s = nxt
        if j + 1 < nb_past:
            nxt = score(j + 1)
        mj = _row_max(functools.reduce(jnp.maximum, s))
        ps = [jnp.exp(sl - mj) for sl in s]
        lj = jnp.sum(functools.reduce(operator.add, ps), axis=1, keepdims=True)
        m_ref[j] = mj
        l_ref[j] = jnp.broadcast_to(lj, wide)
        acc_ref[j] = _dot_nt(block_t(vt_buf, j).astype(BF16), jnp.concatenate(ps, axis=1).astype(BF16))
        prefetch(range(PAGES_PER_BLOCK * j, PAGES_PER_BLOCK * (j + 1)))

    kn = nk_ref[...].astype(BF16)
    t_k = lax.broadcasted_iota(jnp.int32, (1, dec_seq), 1)
    dist = (t_q[:, :1] - past_len) - t_k
    s = _dot_nt(q, kn) * scale - slope[:, :1] * dist.astype(F32)
    s = jnp.where(dist >= 0, s, NEG_INF)
    m_own = jnp.max(s, axis=1, keepdims=True)
    p = jnp.exp(s - m_own)
    l_own = jnp.sum(p, axis=1, keepdims=True)
    acc_own = _dot(p.astype(BF16), nv_ref[...].astype(BF16))

    m_blk = jnp.zeros(wide, F32)
    l_blk = jnp.zeros(wide, F32)
    for j in range(nb_past):
        m_blk = jnp.where(lane == j, m_ref[j], m_blk)
        l_blk = jnp.where(lane == j, l_ref[j], l_blk)
    m_tot = jnp.maximum(m_own, jnp.max(jnp.where(sel, m_blk, NEG_INF), axis=1, keepdims=True))
    w_blk = jnp.where(sel, jnp.exp(m_blk - m_tot), 0.0)
    w_own = jnp.exp(m_own - m_tot)
    l_tot = jnp.sum(w_blk * l_blk, axis=1, keepdims=True) + w_own * l_own
    w_t = jnp.concatenate([w_blk, jnp.zeros((LANES - rows, LANES), F32)], axis=0).T[:, :rows]
    acc_t = jnp.zeros((MOBA_KV_WIDTH, rows), F32)
    for j in range(nb_past):
        acc_t = acc_t + w_t[j:j + 1, :] * acc_ref[j]
    acc_t = jnp.concatenate([acc_t, jnp.zeros((MOBA_KV_WIDTH, LANES - rows), F32)], axis=1)
    o = (acc_t.T[:rows, :] + w_own * acc_own) / l_tot
    pieces = []
    for h in range(MOBA_HEADS):
        g = h // MOBA_GROUP
        pieces.append(o[h * dec_seq:(h + 1) * dec_seq, g * MOBA_HEAD_DIM:(g + 1) * MOBA_HEAD_DIM])
    o_ref[...] = jnp.concatenate(pieces, axis=1)
    drain()


def _moba_sample(page_table, qm, new_k, new_v, slope_rows, cache_kt, cache_vt, layer, *, dec_batch, dec_seq):
    n_pages = page_table.shape[1]
    past_len = n_pages * PAGE_SIZE
    nb_past = past_len // MOBA_BLOCK
    rows = MOBA_HEADS * dec_seq
    in_specs = [
        pl.BlockSpec((MOBA_HEADS, dec_seq, MOBA_KV_WIDTH), lambda b, pt: (0, b, 0)),
        pl.BlockSpec((dec_seq, MOBA_KV_WIDTH), lambda b, pt: (b, 0)),
        pl.BlockSpec((dec_seq, MOBA_KV_WIDTH), lambda b, pt: (b, 0)),
        pl.BlockSpec((rows, LANES), lambda b, pt: (0, 0)),
        pl.BlockSpec(memory_space=pl.ANY),
        pl.BlockSpec(memory_space=pl.ANY),
    ]
    page_buf = pltpu.VMEM((2, n_pages, MOBA_KV_WIDTH, PAGE_SIZE), F32)
    stat = pltpu.VMEM((nb_past, rows, LANES), F32)
    return pl.pallas_call(
        functools.partial(_moba_sample_kernel, layer=layer, dec_seq=dec_seq, n_pages=n_pages,
                          past_len=past_len),
        grid_spec=pltpu.PrefetchScalarGridSpec(
            num_scalar_prefetch=1, grid=(dec_batch,), in_specs=in_specs,
            out_specs=pl.BlockSpec((dec_seq, BRANCH_WIDTH), lambda b, pt: (b, 0)),
            scratch_shapes=[page_buf, page_buf, pltpu.SemaphoreType.DMA((2, 2)), stat, stat,
                            pltpu.VMEM((nb_past, MOBA_KV_WIDTH, rows), F32)]),
        out_shape=jax.ShapeDtypeStruct((dec_batch * dec_seq, BRANCH_WIDTH), F32),
        compiler_params=pltpu.CompilerParams(
            dimension_semantics=("arbitrary",), vmem_limit_bytes=VMEM_LIMIT),
        name="moba_sample",
    )(page_table, qm, new_k, new_v, slope_rows, cache_kt, cache_vt)


def _ffn_kernel(x_ref, oa_ref, ob_ref, g_attn_ref, wgate_ref, wbr_ref, wo_ref, g_ffn_ref, wgu_ref, wdn_ref,
                g_fin_ref, out_ref, *, final, n_chunks):
    x = x_ref[...]
    xn = _rms(x, g_attn_ref[...]).astype(BF16)
    merged = (jax.nn.sigmoid(_dot(xn, wgate_ref[:, :D_MODEL])) * _dot(oa_ref[...].astype(BF16), wbr_ref[0])
              + jax.nn.sigmoid(_dot(xn, wgate_ref[:, D_MODEL:])) * _dot(ob_ref[...].astype(BF16), wbr_ref[1]))
    h = x + _dot(merged.astype(BF16), wo_ref[...])
    hn = _rms(h, g_ffn_ref[...]).astype(BF16)
    cw = FFN_HIDDEN // n_chunks
    acc = h
    for c in range(n_chunks):
        a = _dot(hn, wgu_ref[:, c * cw:(c + 1) * cw])
        u = _dot(hn, wgu_ref[:, FFN_HIDDEN + c * cw:FFN_HIDDEN + (c + 1) * cw])
        act = (jax.nn.silu(a) * u).astype(BF16)
        acc = acc + _dot(act, wdn_ref[c * cw:(c + 1) * cw, :])
    out_ref[...] = _rms(acc, g_fin_ref[...]) if final else acc


def _ffn(x, o_a, o_b, w, g_final, *, final, tm, n_chunks=2):
    n = x.shape[0]
    row = lambda i: (i, 0)
    return pl.pallas_call(
        functools.partial(_ffn_kernel, final=final, n_chunks=n_chunks),
        grid=(n // tm,),
        in_specs=[
            pl.BlockSpec((tm, D_MODEL), row),
            pl.BlockSpec((tm, BRANCH_WIDTH), row),
            pl.BlockSpec((tm, BRANCH_WIDTH), row),
            _const_spec((1, D_MODEL)),
            _const_spec((D_MODEL, 2 * D_MODEL)),
            _const_spec((2, BRANCH_WIDTH, D_MODEL)),
            _const_spec((D_MODEL, D_MODEL)),
            _const_spec((1, D_MODEL)),
            _const_spec((D_MODEL, 2 * FFN_HIDDEN)),
            _const_spec((FFN_HIDDEN, D_MODEL)),
            _const_spec((1, D_MODEL)),
        ],
        out_specs=pl.BlockSpec((tm, D_MODEL), row),
        out_shape=jax.ShapeDtypeStruct((n, D_MODEL), F32),
        compiler_params=pltpu.CompilerParams(
            dimension_semantics=("arbitrary",), vmem_limit_bytes=VMEM_LIMIT),
        name="ffn_final" if final else "ffn",
    )(x, o_a, o_b, w["g_attn"], w["w_gate"], w["w_branch"], w["w_o"], w["g_ffn"], w["w_gu"], w["w_down"],
      g_final)


def _prep_layer_weights(w_in, g_q, g_kv, w_uq, w_uk, w_uv, w_branch, w_o, g_attn, g_ffn, w_gu, w_down):
    s = [0, Q_LORA, Q_LORA + KV_LORA, Q_LORA + KV_LORA + QK_ROPE]
    s.append(s[-1] + MOBA_HEADS * MOBA_HEAD_DIM)
    s.append(s[-1] + MOBA_KV_WIDTH)
    s.append(s[-1] + MOBA_KV_WIDTH)
    c_q, c_kv, k_r, m_q, m_k, m_v, gate = (w_in[:, s[0]:s[1]], w_in[:, s[1]:s[2]], w_in[:, s[2]:s[3]],
                                           w_in[:, s[3]:s[4]], w_in[:, s[4]:s[5]], w_in[:, s[5]:s[6]],
                                           w_in[:, s[6]:])
    m_q = m_q.reshape(D_MODEL, MOBA_KV_HEADS, MOBA_GROUP, MOBA_HEAD_DIM)
    zeros = jnp.zeros_like(m_q[:, 0])
    m_q_bd = jnp.concatenate([
        jnp.concatenate([m_q[:, 0], zeros], axis=-1),
        jnp.concatenate([zeros, m_q[:, 1]], axis=-1)], axis=1).reshape(D_MODEL, MOBA_HEADS * LANES)
    w_in_perm = jnp.concatenate([c_q, c_kv, m_q_bd, m_k, m_v, k_r], axis=1).astype(BF16)

    uq = w_uq.reshape(Q_LORA, MLA_HEADS, QK_NOPE + QK_ROPE)
    w_uqn = jnp.pad(uq[:, :, :QK_NOPE], ((0, 0), (0, 0), (0, LANES - QK_NOPE))).reshape(Q_LORA, MLA_HEADS * LANES)
    w_uqr = jnp.concatenate([uq[:, :, QK_NOPE:QK_NOPE + HALF_ROPE].reshape(Q_LORA, LANES),
                             uq[:, :, QK_NOPE + HALF_ROPE:].reshape(Q_LORA, LANES)], axis=1)
    uk = jnp.transpose(w_uk, (1, 2, 0))
    uk = jnp.pad(uk, ((0, 0), (0, LANES - QK_NOPE), (0, 0)))
    uv = jnp.transpose(w_uv, (1, 0, 2))
    eye = jnp.eye(MLA_HEADS, dtype=w_uv.dtype)
    uv_pad = (uv[:, :, None, :] * eye[:, None, :, None]).reshape(MLA_HEADS, KV_LORA, BRANCH_WIDTH)
    return {
        "w_in": w_in_perm, "w_gate": gate.astype(BF16), "g_q": g_q[None], "g_kv": g_kv[None], "g_attn": g_attn[None], "g_ffn": g_ffn[None],
        "w_uqn": w_uqn.astype(BF16), "w_uqr": w_uqr.astype(BF16), "w_uk": uk.astype(BF16),
        "w_uv": uv_pad.astype(BF16), "w_branch": w_branch.astype(BF16), "w_o": w_o.astype(BF16),
        "w_gu": w_gu.astype(BF16), "w_down": w_down.astype(BF16),
    }


def _rope_tables(pos):
    inv = ROPE_THETA ** (-jnp.arange(HALF_ROPE, dtype=F32) / HALF_ROPE)
    ang = pos.astype(F32)[:, None] * inv[None, :]
    reps = LANES // HALF_ROPE
    return jnp.tile(jnp.cos(ang), (1, reps)), jnp.tile(jnp.sin(ang), (1, reps))


def _slope_rows(tokens_per_head, width):
    slopes = 2.0 ** (-8.0 * jnp.arange(1, MOBA_HEADS + 1, dtype=F32) / MOBA_HEADS)
    return jnp.broadcast_to(jnp.repeat(slopes, tokens_per_head)[:, None], (MOBA_HEADS * tokens_per_head, width))


def kernel(x_prompt, x_sample, cache_ckv, cache_krope, cache_k, cache_v, page_table, w_in, g_q, g_kv, w_uq, w_uk, w_uv, w_branch, w_o, g_attn, g_ffn, w_gu, w_down, g_final):
    batch, seq, _ = x_prompt.shape
    dec_batch, dec_seq, _ = x_sample.shape
    depth, n_phys = cache_k.shape[:2]
    past_len = page_table.shape[1] * PAGE_SIZE
    n_p, n_s = batch * seq, dec_batch * dec_seq
    tm_p = min(512, seq)
    tm_s = min(512, n_s)

    cache_krope_t = jnp.transpose(cache_krope, (0, 1, 3, 2))
    cache_kt = jnp.transpose(cache_k, (0, 1, 3, 4, 2)).reshape(depth, n_phys, MOBA_KV_WIDTH, PAGE_SIZE)
    cache_vt = jnp.transpose(cache_v, (0, 1, 3, 4, 2)).reshape(depth, n_phys, MOBA_KV_WIDTH, PAGE_SIZE)

    cos_p, sin_p = _rope_tables(jnp.arange(seq, dtype=jnp.int32))
    pos_s = past_len + jnp.arange(dec_seq, dtype=jnp.int32)
    cos_s, sin_s = _rope_tables(jnp.tile(pos_s, tm_s // dec_seq))
    slope_p = _slope_rows(MOBA_BLOCK, 1)
    slope_s = _slope_rows(dec_seq, LANES)
    g_fin = g_final[None]

    hp = x_prompt.reshape(n_p, D_MODEL)
    hs = x_sample.reshape(n_s, D_MODEL)
    rows_p, rows_s = [], []
    for l in range(depth):
        w = _prep_layer_weights(w_in[l], g_q[l], g_kv[l], w_uq[l], w_uk[l], w_uv[l], w_branch[l], w_o[l],
                                g_attn[l], g_ffn[l], w_gu[l], w_down[l])
        final = l == depth - 1

        ckv, kr, mk, mv, qf, qm, kf, mkv, kmean = _proj(hp, cos_p, sin_p, w, prompt=True, tm=tm_p)
        o_a = _mla_prompt(qf, kf, w["w_uv"], batch=batch, seq=seq, tq=min(256, seq), tk=min(256, seq))
        o_b = _moba_prompt(qm, mkv, kmean, slope_p, batch=batch, seq=seq)
        hp = _ffn(hp, o_a, o_b, w, g_fin, final=final, tm=tm_p)
        rows_p.append((ckv, kr, mk, mv))

        ckv, kr, mk, mv, qf, qm = _proj(hs, cos_s, sin_s, w, prompt=False, tm=tm_s)
        o_lat = _mla_sample(page_table, qf, ckv, kr, cache_ckv, cache_krope_t, l,
                            dec_batch=dec_batch, dec_seq=dec_seq)
        o_a = _uv_sample(o_lat, w["w_uv"])
        o_b = _moba_sample(page_table, qm, mk, mv, slope_s, cache_kt, cache_vt, l,
                           dec_batch=dec_batch, dec_seq=dec_seq)
        hs = _ffn(hs, o_a, o_b, w, g_fin, final=final, tm=tm_s)
        rows_s.append((ckv, kr, mk, mv))

    def stack(rows, i, shape):
        return jnp.stack([r[i] for r in rows]).reshape((depth,) + shape)

    kv_shape = (MOBA_KV_HEADS, MOBA_HEAD_DIM)
    return (hp.reshape(batch, seq, D_MODEL),
            hs.reshape(dec_batch, dec_seq, D_MODEL),
            stack(rows_p, 0, (batch, seq, KV_LORA)),
            stack(rows_p, 1, (batch, seq, QK_ROPE)),
            stack(rows_p, 2, (batch, seq) + kv_shape),
            stack(rows_p, 3, (batch, seq) + kv_shape),
            stack(rows_s, 0, (dec_batch, dec_seq, KV_LORA)),
            stack(rows_s, 1, (dec_batch, dec_seq, QK_ROPE)),
            stack(rows_s, 2, (dec_batch, dec_seq) + kv_shape),
            stack(rows_s, 3, (dec_batch, dec_seq) + kv_shape))
```

```python
import functools
import math
import operator

import jax
import jax.numpy as jnp
from jax import lax
from jax.experimental import pallas as pl
from jax.experimental.pallas import tpu as pltpu

F32 = jnp.float32
BF16 = jnp.bfloat16

D_MODEL = 1024
PAGE_SIZE = 128
MLA_HEADS = 8
Q_LORA = 384
KV_LORA = 256
QK_NOPE = 64
QK_ROPE = 32
V_HEAD = 64
ROPE_THETA = 10000.0
MOBA_HEADS = 8
MOBA_KV_HEADS = 2
MOBA_GROUP = MOBA_HEADS // MOBA_KV_HEADS
MOBA_HEAD_DIM = 64
MOBA_BLOCK = 256
MOBA_TOPK = 3
BRANCH_WIDTH = 512
FFN_HIDDEN = 2816
RMS_EPS = 1e-6

LANES = 128
QK_PAD = KV_LORA + LANES
MOBA_KV_WIDTH = MOBA_KV_HEADS * MOBA_HEAD_DIM
HALF_ROPE = QK_ROPE // 2
PAGES_PER_BLOCK = MOBA_BLOCK // PAGE_SIZE

_OFF_CQ = 0
_OFF_CKV = _OFF_CQ + Q_LORA
_OFF_MQ = _OFF_CKV + KV_LORA
_OFF_MK = _OFF_MQ + MOBA_HEADS * LANES
_OFF_MV = _OFF_MK + MOBA_KV_WIDTH
_OFF_KR = _OFF_MV + MOBA_KV_WIDTH
_D_IN_PERM = _OFF_KR + QK_ROPE

VMEM_LIMIT = 56 * 1024 * 1024
MLA_CHUNK_PAGES = 32
DMA_ISSUE_UNROLL = 4
NEG_INF = float("-inf")
LOG2_E = 1.4426950408889634


def _rms(x, g):
    return x * lax.rsqrt(jnp.mean(x * x, axis=-1, keepdims=True) + RMS_EPS) * g


def _dot(a, b):
    return jnp.dot(a, b, preferred_element_type=F32)


def _dot_nt(a, b):
    return lax.dot_general(a, b, (((1,), (1,)), ((), ())), preferred_element_type=F32)


def _const_spec(shape):
    return pl.BlockSpec(shape, lambda *_: (0,) * len(shape), pipeline_mode=pl.Buffered(1))


def _lane_slabs(x):
    return [x[:, j * LANES:(j + 1) * LANES] for j in range(x.shape[1] // LANES)]


def _slab_max(m, s):
    return functools.reduce(jnp.maximum, _lane_slabs(s), m)


def _row_max(m_slab):
    return jnp.broadcast_to(jnp.max(m_slab, axis=1, keepdims=True), m_slab.shape)


def _proj_kernel(x_ref, cos_ref, sin_ref, g_attn_ref, w_in_ref, g_q_ref, g_kv_ref,
                 w_uqn_ref, w_uqr_ref, w_uk_ref, *out_refs, prompt):
    if prompt:
        (ckv_ref, kr_ref, mk_ref, mv_ref, qf_ref, qm_ref,
         kf_ref, mkv_ref, kmean_ref) = out_refs
    else:
        ckv_ref, kr_ref, mk_ref, mv_ref, qf_ref, qm_ref = out_refs
    tm = x_ref.shape[0]
    hn = _rms(x_ref[...], g_attn_ref[...]).astype(BF16)

    def seg(lo, hi):
        return _dot(hn, w_in_ref[:, lo:hi])

    cos = cos_ref[...]
    sin = sin_ref[...]

    ckv = _rms(seg(_OFF_CKV, _OFF_MQ), g_kv_ref[...])
    ckv_ref[...] = ckv
    kr = seg(_OFF_KR, _D_IN_PERM)
    kr_swapped = jnp.concatenate([kr[:, HALF_ROPE:], kr[:, :HALF_ROPE]], axis=1)
    lane32 = lax.broadcasted_iota(jnp.int32, (tm, QK_ROPE), 1)
    sin_signed = jnp.where(lane32 < HALF_ROPE, -sin[:, :QK_ROPE], sin[:, :QK_ROPE])
    kr = kr * cos[:, :QK_ROPE] + kr_swapped * sin_signed
    kr_ref[...] = kr

    mk = seg(_OFF_MK, _OFF_MV)
    mv = seg(_OFF_MV, _OFF_KR)
    mk_ref[...] = mk
    mv_ref[...] = mv

    mq = seg(_OFF_MQ, _OFF_MK)
    for h in range(MOBA_HEADS):
        qm_ref[h] = mq[:, h * LANES:(h + 1) * LANES].astype(qm_ref.dtype)

    cq = _rms(seg(_OFF_CQ, _OFF_CKV), g_q_ref[...]).astype(BF16)
    q_nope = _dot(cq, w_uqn_ref[...]).astype(BF16)
    q_rot = _dot(cq, w_uqr_ref[...])
    r1, r2 = q_rot[:, :LANES], q_rot[:, LANES:]
    o1 = r1 * cos - r2 * sin
    o2 = r2 * cos + r1 * sin
    lane = lax.broadcasted_iota(jnp.int32, (tm, LANES), 1)
    for h in range(MLA_HEADS):
        q_lat = _dot(q_nope[:, h * LANES:(h + 1) * LANES], w_uk_ref[h])
        shift_a = (LANES - HALF_ROPE * h) % LANES
        shift_b = (HALF_ROPE - HALF_ROPE * h) % LANES
        a = o1 if shift_a == 0 else pltpu.roll(o1, shift_a, axis=1)
        b = o2 if shift_b == 0 else pltpu.roll(o2, shift_b, axis=1)
        rope = jnp.where(lane < HALF_ROPE, a, jnp.where(lane < QK_ROPE, b, 0.0))
        qf_ref[h, :, :KV_LORA] = q_lat.astype(qf_ref.dtype)
        qf_ref[h, :, KV_LORA:] = rope.astype(qf_ref.dtype)

    if prompt:
        kf_ref[:, :KV_LORA] = ckv.astype(BF16)
        kr_pad = jnp.concatenate([kr, jnp.zeros((tm, LANES - QK_ROPE), F32)], axis=1)
        kf_ref[:, KV_LORA:] = kr_pad.astype(BF16)
        mkv_ref[:, :MOBA_KV_WIDTH] = mk.astype(BF16)
        mkv_ref[:, MOBA_KV_WIDTH:] = mv.astype(BF16)
        for j in range(tm // MOBA_BLOCK):
            kmean_ref[j] = jnp.mean(mk[j * MOBA_BLOCK:(j + 1) * MOBA_BLOCK], axis=0, keepdims=True)


def _proj(x, cos_tab, sin_tab, w, *, prompt, tm):
    n = x.shape[0]
    n_tab = cos_tab.shape[0] // tm
    grid = (n // tm,)
    row = lambda i: (i, 0)
    q_dtype = BF16 if prompt else F32
    out_shape = [
        jax.ShapeDtypeStruct((n, KV_LORA), F32),
        jax.ShapeDtypeStruct((n, QK_ROPE), F32),
        jax.ShapeDtypeStruct((n, MOBA_KV_WIDTH), F32),
        jax.ShapeDtypeStruct((n, MOBA_KV_WIDTH), F32),
        jax.ShapeDtypeStruct((MLA_HEADS, n, QK_PAD), q_dtype),
        jax.ShapeDtypeStruct((MOBA_HEADS, n, MOBA_KV_WIDTH), F32),
    ]
    out_specs = [
        pl.BlockSpec((tm, KV_LORA), row),
        pl.BlockSpec((tm, QK_ROPE), row),
        pl.BlockSpec((tm, MOBA_KV_WIDTH), row),
        pl.BlockSpec((tm, MOBA_KV_WIDTH), row),
        pl.BlockSpec((MLA_HEADS, tm, QK_PAD), lambda i: (0, i, 0)),
        pl.BlockSpec((MOBA_HEADS, tm, MOBA_KV_WIDTH), lambda i: (0, i, 0)),
    ]
    if prompt:
        nblk = tm // MOBA_BLOCK
        out_shape += [
            jax.ShapeDtypeStruct((n, QK_PAD), BF16),
            jax.ShapeDtypeStruct((n, 2 * MOBA_KV_WIDTH), BF16),
            jax.ShapeDtypeStruct((n // MOBA_BLOCK, 1, MOBA_KV_WIDTH), F32),
        ]
        out_specs += [
            pl.BlockSpec((tm, QK_PAD), row),
            pl.BlockSpec((tm, 2 * MOBA_KV_WIDTH), row),
            pl.BlockSpec((nblk, 1, MOBA_KV_WIDTH), lambda i: (i, 0, 0)),
        ]
    tab = lambda i: (i % n_tab, 0)
    in_specs = [
        pl.BlockSpec((tm, D_MODEL), row),
        pl.BlockSpec((tm, LANES), tab),
        pl.BlockSpec((tm, LANES), tab),
        _const_spec((1, D_MODEL)),
        _const_spec((D_MODEL, _D_IN_PERM)),
        _const_spec((1, Q_LORA)),
        _const_spec((1, KV_LORA)),
        _const_spec((Q_LORA, MLA_HEADS * LANES)),
        _const_spec((Q_LORA, 2 * LANES)),
        _const_spec((MLA_HEADS, LANES, KV_LORA)),
    ]
    return pl.pallas_call(
        functools.partial(_proj_kernel, prompt=prompt),
        grid=grid, in_specs=in_specs, out_specs=out_specs, out_shape=out_shape,
        compiler_params=pltpu.CompilerParams(
            dimension_semantics=("arbitrary",), vmem_limit_bytes=VMEM_LIMIT),
        name="proj_prompt" if prompt else "proj_sample",
    )(x, cos_tab, sin_tab, w["g_attn"], w["w_in"], w["g_q"], w["g_kv"],
      w["w_uqn"], w["w_uqr"], w["w_uk"])


def _exp_scaled(x, scale):
    return jnp.exp(x) if scale == 1.0 else jnp.exp2(x * (scale * LOG2_E))


def _exp_accumulate(s, m, scale, v, l_ref, acc_ref):
    ps = [_exp_scaled(sl - m, scale) for sl in _lane_slabs(s)]
    l_ref[...] += functools.reduce(operator.add, ps)
    acc_ref[...] += _dot(jnp.concatenate(ps, axis=1).astype(BF16), v)


def _mla_prompt_kernel(q_ref, k_ref, wuv_ref, o_ref, s_ref, m_ref, l_ref, acc_ref, *, tq, tk, scale):
    qi = pl.program_id(1)
    rows = MLA_HEADS * tq
    q = q_ref[...].reshape(rows, QK_PAD)
    last = (qi * tq + tq - 1) // tk

    def k_chunk(kc):
        return k_ref[pl.ds(pl.multiple_of(kc * tk, tk), tk), :]

    def scores(kc, masked):
        s = _dot_nt(q, k_chunk(kc))
        if masked:
            q_pos = qi * tq + lax.broadcasted_iota(jnp.int32, (MLA_HEADS, tq, tk), 1).reshape(rows, tk)
            k_pos = kc * tk + lax.broadcasted_iota(jnp.int32, (rows, tk), 1)
            s = jnp.where(k_pos <= q_pos, s, NEG_INF)
        s_ref[kc] = s
        m_ref[...] = _slab_max(m_ref[...], s)

    m_ref[...] = jnp.full(m_ref.shape, NEG_INF, F32)

    def score_body(kc, carry):
        scores(kc, False)
        return carry

    lax.fori_loop(0, last, score_body, 0)
    scores(last, True)

    m_ref[...] = _row_max(m_ref[...])
    l_ref[...] = jnp.zeros(l_ref.shape, F32)
    acc_ref[...] = jnp.zeros(acc_ref.shape, F32)

    def value_body(kc, carry):
        _exp_accumulate(s_ref[kc], m_ref[...], scale, k_chunk(kc)[:, :KV_LORA], l_ref, acc_ref)
        return carry

    lax.fori_loop(0, last + 1, value_body, 0)

    o_lat = (acc_ref[...] / jnp.sum(l_ref[...], axis=1, keepdims=True)).astype(BF16)
    o = _dot(o_lat[:tq], wuv_ref[0])
    for h in range(1, MLA_HEADS):
        o = o + _dot(o_lat[h * tq:(h + 1) * tq], wuv_ref[h])
    o_ref[...] = o.astype(o_ref.dtype)


def _mla_prompt(qf, kf, w_uv, *, batch, seq, tq, tk):
    nq = seq // tq
    rows = MLA_HEADS * tq
    qf4 = qf.reshape(MLA_HEADS, batch, seq, QK_PAD)
    kf3 = kf.reshape(batch, seq, QK_PAD)
    scale = (QK_NOPE + QK_ROPE) ** -0.5
    return pl.pallas_call(
        functools.partial(_mla_prompt_kernel, tq=tq, tk=tk, scale=scale),
        grid=(batch, nq),
        in_specs=[
            pl.BlockSpec((MLA_HEADS, None, tq, QK_PAD), lambda b, i: (0, b, i, 0)),
            pl.BlockSpec((None, seq, QK_PAD), lambda b, i: (b, 0, 0)),
            _const_spec((MLA_HEADS, KV_LORA, BRANCH_WIDTH)),
        ],
        out_specs=pl.BlockSpec((tq, BRANCH_WIDTH), lambda b, i: (b * nq + i, 0)),
        out_shape=jax.ShapeDtypeStruct((batch * seq, BRANCH_WIDTH), BF16),
        scratch_shapes=[pltpu.VMEM((seq // tk, rows, tk), F32), pltpu.VMEM((rows, LANES), F32),
                        pltpu.VMEM((rows, LANES), F32), pltpu.VMEM((rows, KV_LORA), F32)],
        compiler_params=pltpu.CompilerParams(
            dimension_semantics=("arbitrary", "arbitrary"), vmem_limit_bytes=VMEM_LIMIT),
        name="mla_prompt",
    )(qf4, kf3, w_uv)


def _gate_scores(qf, kmean, kmean_is_transposed):
    mm = _dot if kmean_is_transposed else _dot_nt
    q_hi = qf.astype(BF16)
    q_lo = (qf - q_hi.astype(F32)).astype(BF16)
    k_hi = kmean.astype(BF16)
    k_lo = (kmean - k_hi.astype(F32)).astype(BF16)
    return mm(q_hi, k_hi) + (mm(q_hi, k_lo) + mm(q_lo, k_hi))


def _prescaled_query(qf, scale):
    if math.frexp(scale)[0] == 0.5:
        return (qf * scale).astype(BF16), 1.0
    return qf.astype(BF16), scale


def _topk_lanes(gate, valid, n_sel):
    nb = gate.shape[1]
    lane = lax.broadcasted_iota(jnp.int32, gate.shape, 1)
    g = jnp.where(valid, gate, NEG_INF)
    picked = jnp.zeros(gate.shape, jnp.bool_)
    for _ in range(n_sel):
        cur = jnp.where(picked, NEG_INF, g)
        best = jnp.max(cur, axis=1, keepdims=True)
        cand = jnp.logical_and(cur == best, jnp.logical_not(picked))
        idx = jnp.min(jnp.where(cand, lane, nb), axis=1, keepdims=True)
        picked = jnp.logical_or(picked, lane == idx)
    return jnp.logical_and(picked, valid)


def _moba_prompt_kernel(q_ref, kv_ref, kmean_ref, slope_ref, o_ref, s_ref, m_ref, l_ref, acc_ref, *, nb):
    tq = MOBA_BLOCK
    qi = pl.program_id(1)
    rows = MOBA_HEADS * tq
    own_slot = nb - 1
    qf = q_ref[...].reshape(rows, MOBA_KV_WIDTH)
    q, scale = _prescaled_query(qf, MOBA_HEAD_DIM ** -0.5)
    slope = slope_ref[...]
    t_q = lax.broadcasted_iota(jnp.int32, (MOBA_HEADS, tq, 1), 1).reshape(rows, 1)
    off_k = lax.broadcasted_iota(jnp.int32, (1, tq), 1)

    gate = _gate_scores(qf, kmean_ref[...].reshape(nb, MOBA_KV_WIDTH), False)
    blk = lax.broadcasted_iota(jnp.int32, (rows, nb), 1)
    sel = _topk_lanes(gate, blk < qi, min(MOBA_TOPK, nb))
    sel_bias = jnp.where(sel, 0.0, NEG_INF)

    dist_own = t_q - off_k
    alibi_own = -slope * dist_own.astype(F32)

    def block_rows(j):
        return pl.ds(j * tq if isinstance(j, int) else pl.multiple_of(j * tq, tq), tq)

    def keys(j):
        return kv_ref[block_rows(j), :MOBA_KV_WIDTH]

    def values(j):
        return kv_ref[block_rows(j), MOBA_KV_WIDTH:]

    def qk(j):
        raw = _dot_nt(q, keys(j))
        return raw if scale == 1.0 else raw * scale

    s = jnp.where(dist_own >= 0, qk(qi) + alibi_own, NEG_INF)
    s_ref[own_slot] = s
    m_ref[...] = _slab_max(jnp.full(m_ref.shape, NEG_INF, F32), s)
    for j in range(nb - 1):
        @pl.when(j < qi)
        def _():
            shift = sel_bias[:, j:j + 1] - slope * ((qi - j) * tq).astype(F32)
            sj = qk(j) + alibi_own + shift
            s_ref[j] = sj
            m_ref[...] = _slab_max(m_ref[...], sj)

    m_ref[...] = _row_max(m_ref[...])
    l_ref[...] = jnp.zeros(l_ref.shape, F32)
    acc_ref[...] = jnp.zeros(acc_ref.shape, F32)
    _exp_accumulate(s_ref[own_slot], m_ref[...], 1.0, values(qi), l_ref, acc_ref)
    for j in range(nb - 1):
        @pl.when(j < qi)
        def _():
            _exp_accumulate(s_ref[j], m_ref[...], 1.0, values(j), l_ref, acc_ref)

    o = acc_ref[...] / jnp.sum(l_ref[...], axis=1, keepdims=True)
    pieces = []
    for h in range(MOBA_HEADS):
        g = h // MOBA_GROUP
        pieces.append(o[h * tq:(h + 1) * tq, g * MOBA_HEAD_DIM:(g + 1) * MOBA_HEAD_DIM])
    o_ref[...] = jnp.concatenate(pieces, axis=1).astype(o_ref.dtype)


def _moba_prompt(qm, mkv, kmean, slope_rows, *, batch, seq):
    tq = MOBA_BLOCK
    nb = seq // tq
    rows = MOBA_HEADS * tq
    qm4 = qm.reshape(MOBA_HEADS, batch, seq, MOBA_KV_WIDTH)
    mkv3 = mkv.reshape(batch, seq, 2 * MOBA_KV_WIDTH)
    kmean4 = kmean.reshape(batch, nb, 1, MOBA_KV_WIDTH)
    return pl.pallas_call(
        functools.partial(_moba_prompt_kernel, nb=nb),
        grid=(batch, nb),
        in_specs=[
            pl.BlockSpec((MOBA_HEADS, None, tq, MOBA_KV_WIDTH), lambda b, i: (0, b, i, 0)),
            pl.BlockSpec((None, seq, 2 * MOBA_KV_WIDTH), lambda b, i: (b, 0, 0)),
            pl.BlockSpec((None, nb, 1, MOBA_KV_WIDTH), lambda b, i: (b, 0, 0, 0)),
            _const_spec((rows, 1)),
        ],
        out_specs=pl.BlockSpec((tq, BRANCH_WIDTH), lambda b, i: (b * nb + i, 0)),
        out_shape=jax.ShapeDtypeStruct((batch * seq, BRANCH_WIDTH), BF16),
        scratch_shapes=[pltpu.VMEM((nb, rows, tq), F32), pltpu.VMEM((rows, LANES), F32),
                        pltpu.VMEM((rows, LANES), F32), pltpu.VMEM((rows, MOBA_KV_WIDTH), F32)],
        compiler_params=pltpu.CompilerParams(
            dimension_semantics=("arbitrary", "arbitrary"), vmem_limit_bytes=VMEM_LIMIT),
        name="moba_prompt",
    )(qm4, mkv3, kmean4, slope_rows)


def _page_copies(layer, hbm_refs, bufs, sems, page, slot, p):
    return [pltpu.make_async_copy(hbm.at[layer, page], buf.at[slot, p], sems.at[i, slot])
            for i, (hbm, buf) in enumerate(zip(hbm_refs, bufs))]


def _page_pipeline(pt_ref, layer, n_pages, hbm_refs, bufs, sems):
    b = pl.program_id(0)
    slot = b % 2
    copies = functools.partial(_page_copies, layer, hbm_refs, bufs, sems)

    def start_fetch(request, into):
        def body(p, carry):
            for cp in copies(pt_ref[request, p], into, p):
                cp.start()
            return carry
        lax.fori_loop(0, n_pages, body, 0, unroll=DMA_ISSUE_UNROLL)

    @pl.when(b == 0)
    def _():
        start_fetch(0, 0)

    @pl.when(b + 1 < pl.num_programs(0))
    def _():
        start_fetch(b + 1, 1 - slot)

    for p in range(n_pages):
        for cp in copies(0, slot, p):
            cp.wait()
    return slot


def _mla_sample_kernel(pt_ref, q_ref, nckv_ref, nkr_ref, ckv_hbm, krt_hbm, o_ref,
                       ckv_buf, krt_buf, sems, m_ref, l_ref, acc_ref, *, layer, dec_seq, n_pages, scale):
    slot = _page_pipeline(pt_ref, layer, n_pages, (ckv_hbm, krt_hbm), (ckv_buf, krt_buf), sems)
    rows = MLA_HEADS * dec_seq
    ch = min(MLA_CHUNK_PAGES, n_pages)
    ck = ch * PAGE_SIZE
    q = q_ref[...].reshape(rows, QK_PAD)
    q_lat = q[:, :KV_LORA].astype(BF16)
    q_rope = q[:, KV_LORA:KV_LORA + QK_ROPE].astype(BF16)

    kn = nckv_ref[...].astype(BF16)
    s_new = _dot_nt(q_lat, kn) + _dot_nt(q_rope, nkr_ref[...].astype(BF16))
    t_q = lax.broadcasted_iota(jnp.int32, (MLA_HEADS, dec_seq, dec_seq), 1).reshape(rows, dec_seq)
    t_k = lax.broadcasted_iota(jnp.int32, (rows, dec_seq), 1)
    s_new = jnp.where(t_k <= t_q, s_new, NEG_INF)
    m_new = jnp.max(s_new, axis=1, keepdims=True)
    p_new = _exp_scaled(s_new - m_new, scale)
    l_new = jnp.sum(p_new, axis=1, keepdims=True)
    acc_new = _dot(p_new.astype(BF16), kn)

    n_chunks = n_pages // ch

    def score(c):
        kb = ckv_buf[slot, c * ch:(c + 1) * ch].reshape(ck, KV_LORA).astype(BF16)
        krt = jnp.concatenate([krt_buf[slot, c * ch + i] for i in range(ch)], axis=1).astype(BF16)
        return kb, _dot_nt(q_lat, kb) + _dot(q_rope, krt)

    nxt = score(0)
    for c in range(n_chunks):
        kb, s = nxt
        if c + 1 < n_chunks:
            nxt = score(c + 1)
        mc = _row_max(_slab_max(jnp.full((rows, LANES), NEG_INF, F32), s))
        ps = [_exp_scaled(sl - mc, scale) for sl in _lane_slabs(s)]
        m_ref[c] = mc
        l_ref[c] = functools.reduce(operator.add, ps)
        acc_ref[c] = _dot(jnp.concatenate(ps, axis=1).astype(BF16), kb)

    m = functools.reduce(jnp.maximum, [m_ref[c] for c in range(n_chunks)],
                         jnp.broadcast_to(m_new, (rows, LANES)))
    l_lanes = jnp.zeros((rows, LANES), F32)
    acc = [jnp.zeros((rows, LANES), F32)] * (KV_LORA // LANES)
    for c in range(n_chunks):
        w = _exp_scaled(m_ref[c] - m, scale)
        l_lanes = l_lanes + w * l_ref[c]
        acc = [a + w * sl for a, sl in zip(acc, _lane_slabs(acc_ref[c]))]
    w_new = _exp_scaled(m_new - m[:, :1], scale)
    l = jnp.sum(l_lanes, axis=1, keepdims=True) + w_new * l_new
    o = (jnp.concatenate(acc, axis=1) + w_new * acc_new) / l
    o_ref[...] = o.reshape(MLA_HEADS, dec_seq, KV_LORA)


def _mla_sample(page_table, qf, new_ckv, new_kr, cache_ckv, cache_krope_t, layer, *, dec_batch, dec_seq):
    n_pages = page_table.shape[1]
    n_chunks = n_pages // min(MLA_CHUNK_PAGES, n_pages)
    rows = MLA_HEADS * dec_seq
    scale = (QK_NOPE + QK_ROPE) ** -0.5
    in_specs = [
        pl.BlockSpec((MLA_HEADS, dec_seq, QK_PAD), lambda b, pt: (0, b, 0)),
        pl.BlockSpec((dec_seq, KV_LORA), lambda b, pt: (b, 0)),
        pl.BlockSpec((dec_seq, QK_ROPE), lambda b, pt: (b, 0)),
        pl.BlockSpec(memory_space=pl.ANY),
        pl.BlockSpec(memory_space=pl.ANY),
    ]
    return pl.pallas_call(
        functools.partial(_mla_sample_kernel, layer=layer, dec_seq=dec_seq, n_pages=n_pages, scale=scale),
        grid_spec=pltpu.PrefetchScalarGridSpec(
            num_scalar_prefetch=1, grid=(dec_batch,), in_specs=in_specs,
            out_specs=pl.BlockSpec((MLA_HEADS, dec_seq, KV_LORA), lambda b, pt: (0, b, 0)),
            scratch_shapes=[pltpu.VMEM((2, n_pages, PAGE_SIZE, KV_LORA), F32),
                            pltpu.VMEM((2, n_pages, QK_ROPE, PAGE_SIZE), F32),
                            pltpu.SemaphoreType.DMA((2, 2)),
                            pltpu.VMEM((n_chunks, rows, LANES), F32),
                            pltpu.VMEM((n_chunks, rows, LANES), F32),
                            pltpu.VMEM((n_chunks, rows, KV_LORA), F32)]),
        out_shape=jax.ShapeDtypeStruct((MLA_HEADS, dec_batch * dec_seq, KV_LORA), F32),
        compiler_params=pltpu.CompilerParams(
            dimension_semantics=("arbitrary",), vmem_limit_bytes=VMEM_LIMIT),
        name="mla_sample",
    )(page_table, qf, new_ckv, new_kr, cache_ckv, cache_krope_t)


def _uv_kernel(o_lat_ref, wuv_ref, o_ref):
    o = _dot(o_lat_ref[0].astype(BF16), wuv_ref[0])
    for h in range(1, MLA_HEADS):
        o = o + _dot(o_lat_ref[h].astype(BF16), wuv_ref[h])
    o_ref[...] = o


def _uv_sample(o_lat, w_uv):
    n = o_lat.shape[1]
    return pl.pallas_call(
        _uv_kernel,
        out_shape=jax.ShapeDtypeStruct((n, BRANCH_WIDTH), F32),
        compiler_params=pltpu.CompilerParams(vmem_limit_bytes=VMEM_LIMIT),
        name="uv_sample",
    )(o_lat, w_uv)


def _moba_sample_kernel(pt_ref, q_ref, nk_ref, nv_ref, slope_ref, kt_hbm, vt_hbm, o_ref,
                        kt_buf, vt_buf, sems, m_ref, l_ref, acc_ref,
                        *, layer, dec_seq, n_pages, past_len):
    slot = _page_pipeline(pt_ref, layer, n_pages, (kt_hbm, vt_hbm), (kt_buf, vt_buf), sems)
    rows = MOBA_HEADS * dec_seq
    scale = MOBA_HEAD_DIM ** -0.5
    nb_past = n_pages // PAGES_PER_BLOCK
    wide = (rows, LANES)
    qf = q_ref[...].reshape(rows, MOBA_KV_WIDTH)
    q = qf.astype(BF16)
    slope = slope_ref[...]
    t_q = past_len + lax.broadcasted_iota(jnp.int32, (MOBA_HEADS, dec_seq, LANES), 1).reshape(wide)
    off_k = lax.broadcasted_iota(jnp.int32, (1, MOBA_BLOCK), 1).astype(F32)
    alibi_off = slope[:, :1] * off_k
    lane = lax.broadcasted_iota(jnp.int32, wide, 1)
    lane_km = lax.broadcasted_iota(jnp.int32, (MOBA_KV_WIDTH, LANES), 1)

    def block_t(buf, j):
        return jnp.concatenate([buf[slot, PAGES_PER_BLOCK * j + i] for i in range(PAGES_PER_BLOCK)], axis=1)

    km = jnp.zeros((MOBA_KV_WIDTH, LANES), F32)
    for j in range(nb_past):
        kmean = jnp.sum(block_t(kt_buf, j), axis=1, keepdims=True) / MOBA_BLOCK
        km = jnp.where(lane_km == j, kmean, km)
    gate = _gate_scores(qf, km, True)
    sel = _topk_lanes(gate, lane < nb_past, min(MOBA_TOPK, nb_past))

    def score(j):
        shift = -slope * (t_q - j * MOBA_BLOCK).astype(F32)
        s = _dot(q, block_t(kt_buf, j).astype(BF16)) * scale + alibi_off
        return [sl + shift for sl in _lane_slabs(s)]

    nxt = score(0)
    for j in range(nb_past):
        s = nxt
        if j + 1 < nb_past:
            nxt = score(j + 1)
        mj = _row_max(functools.reduce(jnp.maximum, s))
        ps = [jnp.exp(sl - mj) for sl in s]
        lj = jnp.sum(functools.reduce(operator.add, ps), axis=1, keepdims=True)
        m_ref[j] = mj
        l_ref[j] = jnp.broadcast_to(lj, wide)
        acc_ref[j] = _dot_nt(block_t(vt_buf, j).astype(BF16), jnp.concatenate(ps, axis=1).astype(BF16))

    kn = nk_ref[...].astype(BF16)
    t_k = lax.broadcasted_iota(jnp.int32, (1, dec_seq), 1)
    dist = (t_q[:, :1] - past_len) - t_k
    s = _dot_nt(q, kn) * scale - slope[:, :1] * dist.astype(F32)
    s = jnp.where(dist >= 0, s, NEG_INF)
    m_own = jnp.max(s, axis=1, keepdims=True)
    p = jnp.exp(s - m_own)
    l_own = jnp.sum(p, axis=1, keepdims=True)
    acc_own = _dot(p.astype(BF16), nv_ref[...].astype(BF16))

    m_blk = jnp.zeros(wide, F32)
    l_blk = jnp.zeros(wide, F32)
    for j in range(nb_past):
        m_blk = jnp.where(lane == j, m_ref[j], m_blk)
        l_blk = jnp.where(lane == j, l_ref[j], l_blk)
    m_tot = jnp.maximum(m_own, jnp.max(jnp.where(sel, m_blk, NEG_INF), axis=1, keepdims=True))
    w_blk = jnp.where(sel, jnp.exp(m_blk - m_tot), 0.0)
    w_own = jnp.exp(m_own - m_tot)
    l_tot = jnp.sum(w_blk * l_blk, axis=1, keepdims=True) + w_own * l_own
    w_t = jnp.concatenate([w_blk, jnp.zeros((LANES - rows, LANES), F32)], axis=0).T[:, :rows]
    acc_t = jnp.zeros((MOBA_KV_WIDTH, rows), F32)
    for j in range(nb_past):
        acc_t = acc_t + w_t[j:j + 1, :] * acc_ref[j]
    acc_t = jnp.concatenate([acc_t, jnp.zeros((MOBA_KV_WIDTH, LANES - rows), F32)], axis=1)
    o = (acc_t.T[:rows, :] + w_own * acc_own) / l_tot
    pieces = []
    for h in range(MOBA_HEADS):
        g = h // MOBA_GROUP
        pieces.append(o[h * dec_seq:(h + 1) * dec_seq, g * MOBA_HEAD_DIM:(g + 1) * MOBA_HEAD_DIM])
    o_ref[...] = jnp.concatenate(pieces, axis=1)


def _moba_sample(page_table, qm, new_k, new_v, slope_rows, cache_kt, cache_vt, layer, *, dec_batch, dec_seq):
    n_pages = page_table.shape[1]
    past_len = n_pages * PAGE_SIZE
    nb_past = past_len // MOBA_BLOCK
    rows = MOBA_HEADS * dec_seq
    in_specs = [
        pl.BlockSpec((MOBA_HEADS, dec_seq, MOBA_KV_WIDTH), lambda b, pt: (0, b, 0)),
        pl.BlockSpec((dec_seq, MOBA_KV_WIDTH), lambda b, pt: (b, 0)),
        pl.BlockSpec((dec_seq, MOBA_KV_WIDTH), lambda b, pt: (b, 0)),
        pl.BlockSpec((rows, LANES), lambda b, pt: (0, 0)),
        pl.BlockSpec(memory_space=pl.ANY),
        pl.BlockSpec(memory_space=pl.ANY),
    ]
    page_buf = pltpu.VMEM((2, n_pages, MOBA_KV_WIDTH, PAGE_SIZE), F32)
    stat = pltpu.VMEM((nb_past, rows, LANES), F32)
    return pl.pallas_call(
        functools.partial(_moba_sample_kernel, layer=layer, dec_seq=dec_seq, n_pages=n_pages,
                          past_len=past_len),
        grid_spec=pltpu.PrefetchScalarGridSpec(
            num_scalar_prefetch=1, grid=(dec_batch,), in_specs=in_specs,
            out_specs=pl.BlockSpec((dec_seq, BRANCH_WIDTH), lambda b, pt: (b, 0)),
            scratch_shapes=[page_buf, page_buf, pltpu.SemaphoreType.DMA((2, 2)), stat, stat,
                            pltpu.VMEM((nb_past, MOBA_KV_WIDTH, rows), F32)]),
        out_shape=jax.ShapeDtypeStruct((dec_batch * dec_seq, BRANCH_WIDTH), F32),
        compiler_params=pltpu.CompilerParams(
            dimension_semantics=("arbitrary",), vmem_limit_bytes=VMEM_LIMIT),
        name="moba_sample",
    )(page_table, qm, new_k, new_v, slope_rows, cache_kt, cache_vt)


def _ffn_kernel(x_ref, oa_ref, ob_ref, g_attn_ref, wgate_ref, wbr_ref, wo_ref, g_ffn_ref, wgu_ref, wdn_ref,
                g_fin_ref, out_ref, *, final, n_chunks):
    x = x_ref[...]
    xn = _rms(x, g_attn_ref[...]).astype(BF16)
    merged = (jax.nn.sigmoid(_dot(xn, wgate_ref[:, :D_MODEL])) * _dot(oa_ref[...].astype(BF16), wbr_ref[0])
              + jax.nn.sigmoid(_dot(xn, wgate_ref[:, D_MODEL:])) * _dot(ob_ref[...].astype(BF16), wbr_ref[1]))
    h = x + _dot(merged.astype(BF16), wo_ref[...])
    hn = _rms(h, g_ffn_ref[...]).astype(BF16)
    cw = FFN_HIDDEN // n_chunks
    acc = h
    for c in range(n_chunks):
        a = _dot(hn, wgu_ref[:, c * cw:(c + 1) * cw])
        u = _dot(hn, wgu_ref[:, FFN_HIDDEN + c * cw:FFN_HIDDEN + (c + 1) * cw])
        act = (jax.nn.silu(a) * u).astype(BF16)
        acc = acc + _dot(act, wdn_ref[c * cw:(c + 1) * cw, :])
    out_ref[...] = _rms(acc, g_fin_ref[...]) if final else acc


def _ffn(x, o_a, o_b, w, g_final, *, final, tm, n_chunks=2):
    n = x.shape[0]
    row = lambda i: (i, 0)
    return pl.pallas_call(
        functools.partial(_ffn_kernel, final=final, n_chunks=n_chunks),
        grid=(n // tm,),
        in_specs=[
            pl.BlockSpec((tm, D_MODEL), row),
            pl.BlockSpec((tm, BRANCH_WIDTH), row),
            pl.BlockSpec((tm, BRANCH_WIDTH), row),
            _const_spec((1, D_MODEL)),
            _const_spec((D_MODEL, 2 * D_MODEL)),
            _const_spec((2, BRANCH_WIDTH, D_MODEL)),
            _const_spec((D_MODEL, D_MODEL)),
            _const_spec((1, D_MODEL)),
            _const_spec((D_MODEL, 2 * FFN_HIDDEN)),
            _const_spec((FFN_HIDDEN, D_MODEL)),
            _const_spec((1, D_MODEL)),
        ],
        out_specs=pl.BlockSpec((tm, D_MODEL), row),
        out_shape=jax.ShapeDtypeStruct((n, D_MODEL), F32),
        compiler_params=pltpu.CompilerParams(
            dimension_semantics=("arbitrary",), vmem_limit_bytes=VMEM_LIMIT),
        name="ffn_final" if final else "ffn",
    )(x, o_a, o_b, w["g_attn"], w["w_gate"], w["w_branch"], w["w_o"], w["g_ffn"], w["w_gu"], w["w_down"],
      g_final)


def _prep_layer_weights(w_in, g_q, g_kv, w_uq, w_uk, w_uv, w_branch, w_o, g_attn, g_ffn, w_gu, w_down):
    s = [0, Q_LORA, Q_LORA + KV_LORA, Q_LORA + KV_LORA + QK_ROPE]
    s.append(s[-1] + MOBA_HEADS * MOBA_HEAD_DIM)
    s.append(s[-1] + MOBA_KV_WIDTH)
    s.append(s[-1] + MOBA_KV_WIDTH)
    c_q, c_kv, k_r, m_q, m_k, m_v, gate = (w_in[:, s[0]:s[1]], w_in[:, s[1]:s[2]], w_in[:, s[2]:s[3]],
                                           w_in[:, s[3]:s[4]], w_in[:, s[4]:s[5]], w_in[:, s[5]:s[6]],
                                           w_in[:, s[6]:])
    m_q = m_q.reshape(D_MODEL, MOBA_KV_HEADS, MOBA_GROUP, MOBA_HEAD_DIM)
    zeros = jnp.zeros_like(m_q[:, 0])
    m_q_bd = jnp.concatenate([
        jnp.concatenate([m_q[:, 0], zeros], axis=-1),
        jnp.concatenate([zeros, m_q[:, 1]], axis=-1)], axis=1).reshape(D_MODEL, MOBA_HEADS * LANES)
    w_in_perm = jnp.concatenate([c_q, c_kv, m_q_bd, m_k, m_v, k_r], axis=1).astype(BF16)

    uq = w_uq.reshape(Q_LORA, MLA_HEADS, QK_NOPE + QK_ROPE)
    w_uqn = jnp.pad(uq[:, :, :QK_NOPE], ((0, 0), (0, 0), (0, LANES - QK_NOPE))).reshape(Q_LORA, MLA_HEADS * LANES)
    w_uqr = jnp.concatenate([uq[:, :, QK_NOPE:QK_NOPE + HALF_ROPE].reshape(Q_LORA, LANES),
                             uq[:, :, QK_NOPE + HALF_ROPE:].reshape(Q_LORA, LANES)], axis=1)
    uk = jnp.transpose(w_uk, (1, 2, 0))
    uk = jnp.pad(uk, ((0, 0), (0, LANES - QK_NOPE), (0, 0)))
    uv = jnp.transpose(w_uv, (1, 0, 2))
    eye = jnp.eye(MLA_HEADS, dtype=w_uv.dtype)
    uv_pad = (uv[:, :, None, :] * eye[:, None, :, None]).reshape(MLA_HEADS, KV_LORA, BRANCH_WIDTH)
    return {
        "w_in": w_in_perm, "w_gate": gate.astype(BF16), "g_q": g_q[None], "g_kv": g_kv[None], "g_attn": g_attn[None], "g_ffn": g_ffn[None],
        "w_uqn": w_uqn.astype(BF16), "w_uqr": w_uqr.astype(BF16), "w_uk": uk.astype(BF16),
        "w_uv": uv_pad.astype(BF16), "w_branch": w_branch.astype(BF16), "w_o": w_o.astype(BF16),
        "w_gu": w_gu.astype(BF16), "w_down": w_down.astype(BF16),
    }


def _rope_tables(pos):
    inv = ROPE_THETA ** (-jnp.arange(HALF_ROPE, dtype=F32) / HALF_ROPE)
    ang = pos.astype(F32)[:, None] * inv[None, :]
    reps = LANES // HALF_ROPE
    return jnp.tile(jnp.cos(ang), (1, reps)), jnp.tile(jnp.sin(ang), (1, reps))


def _slope_rows(tokens_per_head, width):
    slopes = 2.0 ** (-8.0 * jnp.arange(1, MOBA_HEADS + 1, dtype=F32) / MOBA_HEADS)
    return jnp.broadcast_to(jnp.repeat(slopes, tokens_per_head)[:, None], (MOBA_HEADS * tokens_per_head, width))


def kernel(x_prompt, x_sample, cache_ckv, cache_krope, cache_k, cache_v, page_table, w_in, g_q, g_kv, w_uq, w_uk, w_uv, w_branch, w_o, g_attn, g_ffn, w_gu, w_down, g_final):
    batch, seq, _ = x_prompt.shape
    dec_batch, dec_seq, _ = x_sample.shape
    depth, n_phys = cache_k.shape[:2]
    past_len = page_table.shape[1] * PAGE_SIZE
    n_p, n_s = batch * seq, dec_batch * dec_seq
    tm_p = min(512, seq)
    tm_s = min(512, n_s)

    cache_krope_t = jnp.transpose(cache_krope, (0, 1, 3, 2))
    cache_kt = jnp.transpose(cache_k, (0, 1, 3, 4, 2)).reshape(depth, n_phys, MOBA_KV_WIDTH, PAGE_SIZE)
    cache_vt = jnp.transpose(cache_v, (0, 1, 3, 4, 2)).reshape(depth, n_phys, MOBA_KV_WIDTH, PAGE_SIZE)

    cos_p, sin_p = _rope_tables(jnp.arange(seq, dtype=jnp.int32))
    pos_s = past_len + jnp.arange(dec_seq, dtype=jnp.int32)
    cos_s, sin_s = _rope_tables(jnp.tile(pos_s, tm_s // dec_seq))
    slope_p = _slope_rows(MOBA_BLOCK, 1)
    slope_s = _slope_rows(dec_seq, LANES)
    g_fin = g_final[None]

    hp = x_prompt.reshape(n_p, D_MODEL)
    hs = x_sample.reshape(n_s, D_MODEL)
    rows_p, rows_s = [], []
    for l in range(depth):
        w = _prep_layer_weights(w_in[l], g_q[l], g_kv[l], w_uq[l], w_uk[l], w_uv[l], w_branch[l], w_o[l],
                                g_attn[l], g_ffn[l], w_gu[l], w_down[l])
        final = l == depth - 1

        ckv, kr, mk, mv, qf, qm, kf, mkv, kmean = _proj(hp, cos_p, sin_p, w, prompt=True, tm=tm_p)
        o_a = _mla_prompt(qf, kf, w["w_uv"], batch=batch, seq=seq, tq=min(256, seq), tk=min(256, seq))
        o_b = _moba_prompt(qm, mkv, kmean, slope_p, batch=batch, seq=seq)
        hp = _ffn(hp, o_a, o_b, w, g_fin, final=final, tm=tm_p)
        rows_p.append((ckv, kr, mk, mv))

        ckv, kr, mk, mv, qf, qm = _proj(hs, cos_s, sin_s, w, prompt=False, tm=tm_s)
        o_lat = _mla_sample(page_table, qf, ckv, kr, cache_ckv, cache_krope_t, l,
                            dec_batch=dec_batch, dec_seq=dec_seq)
        o_a = _uv_sample(o_lat, w["w_uv"])
        o_b = _moba_sample(page_table, qm, mk, mv, slope_s, cache_kt, cache_vt, l,
                           dec_batch=dec_batch, dec_seq=dec_seq)
        hs = _ffn(hs, o_a, o_b, w, g_fin, final=final, tm=tm_s)
        rows_s.append((ckv, kr, mk, mv))

    def stack(rows, i, shape):
        return jnp.stack([r[i] for r in rows]).reshape((depth,) + shape)

    kv_shape = (MOBA_KV_HEADS, MOBA_HEAD_DIM)
    return (hp.reshape(batch, seq, D_MODEL),
            hs.reshape(dec_batch, dec_seq, D_MODEL),
            stack(rows_p, 0, (batch, seq, KV_LORA)),
            stack(rows_p, 1, (batch, seq, QK_ROPE)),
            stack(rows_p, 2, (batch, seq) + kv_shape),
            stack(rows_p, 3, (batch, seq) + kv_shape),
            stack(rows_s, 0, (dec_batch, dec_seq, KV_LORA)),
            stack(rows_s, 1, (dec_batch, dec_seq, QK_ROPE)),
            stack(rows_s, 2, (dec_batch, dec_seq) + kv_shape),
            stack(rows_s, 3, (dec_batch, dec_seq) + kv_shape))
```

```python
import functools
import math
import operator

import jax
import jax.numpy as jnp
from jax import lax
from jax.experimental import pallas as pl
from jax.experimental.pallas import tpu as pltpu

F32 = jnp.float32
BF16 = jnp.bfloat16

D_MODEL = 1024
PAGE_SIZE = 128
MLA_HEADS = 8
Q_LORA = 384
KV_LORA = 256
QK_NOPE = 64
QK_ROPE = 32
V_HEAD = 64
ROPE_THETA = 10000.0
MOBA_HEADS = 8
MOBA_KV_HEADS = 2
MOBA_GROUP = MOBA_HEADS // MOBA_KV_HEADS
MOBA_HEAD_DIM = 64
MOBA_BLOCK = 256
MOBA_TOPK = 3
BRANCH_WIDTH = 512
FFN_HIDDEN = 2816
RMS_EPS = 1e-6

LANES = 128
QK_PAD = KV_LORA + LANES
MOBA_KV_WIDTH = MOBA_KV_HEADS * MOBA_HEAD_DIM
HALF_ROPE = QK_ROPE // 2
PAGES_PER_BLOCK = MOBA_BLOCK // PAGE_SIZE

_OFF_CQ = 0
_OFF_CKV = _OFF_CQ + Q_LORA
_OFF_MQ = _OFF_CKV + KV_LORA
_OFF_MK = _OFF_MQ + MOBA_HEADS * LANES
_OFF_MV = _OFF_MK + MOBA_KV_WIDTH
_OFF_KR = _OFF_MV + MOBA_KV_WIDTH
_D_IN_PERM = _OFF_KR + QK_ROPE

VMEM_LIMIT = 56 * 1024 * 1024
MLA_CHUNK_PAGES = 16
DMA_ISSUE_UNROLL = 4
NEG_INF = float("-inf")
LOG2_E = 1.4426950408889634


def _rms(x, g):
    return x * lax.rsqrt(jnp.mean(x * x, axis=-1, keepdims=True) + RMS_EPS) * g


def _dot(a, b):
    return jnp.dot(a, b, preferred_element_type=F32)


def _dot_nt(a, b):
    return lax.dot_general(a, b, (((1,), (1,)), ((), ())), preferred_element_type=F32)


def _const_spec(shape):
    return pl.BlockSpec(shape, lambda *_: (0,) * len(shape), pipeline_mode=pl.Buffered(1))


def _lane_slabs(x):
    return [x[:, j * LANES:(j + 1) * LANES] for j in range(x.shape[1] // LANES)]


def _slab_max(m, s):
    return functools.reduce(jnp.maximum, _lane_slabs(s), m)


def _row_max(m_slab):
    return jnp.broadcast_to(jnp.max(m_slab, axis=1, keepdims=True), m_slab.shape)


def _proj_kernel(x_ref, cos_ref, sin_ref, g_attn_ref, w_in_ref, g_q_ref, g_kv_ref,
                 w_uqn_ref, w_uqr_ref, w_uk_ref, *out_refs, prompt):
    if prompt:
        (ckv_ref, kr_ref, mk_ref, mv_ref, qf_ref, qm_ref,
         kf_ref, mkv_ref, kmean_ref) = out_refs
    else:
        ckv_ref, kr_ref, mk_ref, mv_ref, qf_ref, qm_ref = out_refs
    tm = x_ref.shape[0]
    hn = _rms(x_ref[...], g_attn_ref[...]).astype(BF16)

    def seg(lo, hi):
        return _dot(hn, w_in_ref[:, lo:hi])

    cos = cos_ref[...]
    sin = sin_ref[...]

    ckv = _rms(seg(_OFF_CKV, _OFF_MQ), g_kv_ref[...])
    ckv_ref[...] = ckv
    kr = seg(_OFF_KR, _D_IN_PERM)
    kr_swapped = jnp.concatenate([kr[:, HALF_ROPE:], kr[:, :HALF_ROPE]], axis=1)
    lane32 = lax.broadcasted_iota(jnp.int32, (tm, QK_ROPE), 1)
    sin_signed = jnp.where(lane32 < HALF_ROPE, -sin[:, :QK_ROPE], sin[:, :QK_ROPE])
    kr = kr * cos[:, :QK_ROPE] + kr_swapped * sin_signed
    kr_ref[...] = kr

    mk = seg(_OFF_MK, _OFF_MV)
    mv = seg(_OFF_MV, _OFF_KR)
    mk_ref[...] = mk
    mv_ref[...] = mv

    mq = seg(_OFF_MQ, _OFF_MK)
    for h in range(MOBA_HEADS):
        qm_ref[h] = mq[:, h * LANES:(h + 1) * LANES].astype(qm_ref.dtype)

    cq = _rms(seg(_OFF_CQ, _OFF_CKV), g_q_ref[...]).astype(BF16)
    q_nope = _dot(cq, w_uqn_ref[...]).astype(BF16)
    q_rot = _dot(cq, w_uqr_ref[...])
    r1, r2 = q_rot[:, :LANES], q_rot[:, LANES:]
    o1 = r1 * cos - r2 * sin
    o2 = r2 * cos + r1 * sin
    lane = lax.broadcasted_iota(jnp.int32, (tm, LANES), 1)
    for h in range(MLA_HEADS):
        q_lat = _dot(q_nope[:, h * LANES:(h + 1) * LANES], w_uk_ref[h])
        shift_a = (LANES - HALF_ROPE * h) % LANES
        shift_b = (HALF_ROPE - HALF_ROPE * h) % LANES
        a = o1 if shift_a == 0 else pltpu.roll(o1, shift_a, axis=1)
        b = o2 if shift_b == 0 else pltpu.roll(o2, shift_b, axis=1)
        rope = jnp.where(lane < HALF_ROPE, a, jnp.where(lane < QK_ROPE, b, 0.0))
        qf_ref[h, :, :KV_LORA] = q_lat.astype(qf_ref.dtype)
        qf_ref[h, :, KV_LORA:] = rope.astype(qf_ref.dtype)

    if prompt:
        kf_ref[:, :KV_LORA] = ckv.astype(BF16)
        kr_pad = jnp.concatenate([kr, jnp.zeros((tm, LANES - QK_ROPE), F32)], axis=1)
        kf_ref[:, KV_LORA:] = kr_pad.astype(BF16)
        mkv_ref[:, :MOBA_KV_WIDTH] = mk.astype(BF16)
        mkv_ref[:, MOBA_KV_WIDTH:] = mv.astype(BF16)
        for j in range(tm // MOBA_BLOCK):
            kmean_ref[j] = jnp.mean(mk[j * MOBA_BLOCK:(j + 1) * MOBA_BLOCK], axis=0, keepdims=True)


def _proj(x, cos_tab, sin_tab, w, *, prompt, tm):
    n = x.shape[0]
    n_tab = cos_tab.shape[0] // tm
    grid = (n // tm,)
    row = lambda i: (i, 0)
    q_dtype = BF16 if prompt else F32
    out_shape = [
        jax.ShapeDtypeStruct((n, KV_LORA), F32),
        jax.ShapeDtypeStruct((n, QK_ROPE), F32),
        jax.ShapeDtypeStruct((n, MOBA_KV_WIDTH), F32),
        jax.ShapeDtypeStruct((n, MOBA_KV_WIDTH), F32),
        jax.ShapeDtypeStruct((MLA_HEADS, n, QK_PAD), q_dtype),
        jax.ShapeDtypeStruct((MOBA_HEADS, n, MOBA_KV_WIDTH), F32),
    ]
    out_specs = [
        pl.BlockSpec((tm, KV_LORA), row),
        pl.BlockSpec((tm, QK_ROPE), row),
        pl.BlockSpec((tm, MOBA_KV_WIDTH), row),
        pl.BlockSpec((tm, MOBA_KV_WIDTH), row),
        pl.BlockSpec((MLA_HEADS, tm, QK_PAD), lambda i: (0, i, 0)),
        pl.BlockSpec((MOBA_HEADS, tm, MOBA_KV_WIDTH), lambda i: (0, i, 0)),
    ]
    if prompt:
        nblk = tm // MOBA_BLOCK
        out_shape += [
            jax.ShapeDtypeStruct((n, QK_PAD), BF16),
            jax.ShapeDtypeStruct((n, 2 * MOBA_KV_WIDTH), BF16),
            jax.ShapeDtypeStruct((n // MOBA_BLOCK, 1, MOBA_KV_WIDTH), F32),
        ]
        out_specs += [
            pl.BlockSpec((tm, QK_PAD), row),
            pl.BlockSpec((tm, 2 * MOBA_KV_WIDTH), row),
            pl.BlockSpec((nblk, 1, MOBA_KV_WIDTH), lambda i: (i, 0, 0)),
        ]
    tab = lambda i: (i % n_tab, 0)
    in_specs = [
        pl.BlockSpec((tm, D_MODEL), row),
        pl.BlockSpec((tm, LANES), tab),
        pl.BlockSpec((tm, LANES), tab),
        _const_spec((1, D_MODEL)),
        _const_spec((D_MODEL, _D_IN_PERM)),
        _const_spec((1, Q_LORA)),
        _const_spec((1, KV_LORA)),
        _const_spec((Q_LORA, MLA_HEADS * LANES)),
        _const_spec((Q_LORA, 2 * LANES)),
        _const_spec((MLA_HEADS, LANES, KV_LORA)),
    ]
    return pl.pallas_call(
        functools.partial(_proj_kernel, prompt=prompt),
        grid=grid, in_specs=in_specs, out_specs=out_specs, out_shape=out_shape,
        compiler_params=pltpu.CompilerParams(
            dimension_semantics=("arbitrary",), vmem_limit_bytes=VMEM_LIMIT),
        name="proj_prompt" if prompt else "proj_sample",
    )(x, cos_tab, sin_tab, w["g_attn"], w["w_in"], w["g_q"], w["g_kv"],
      w["w_uqn"], w["w_uqr"], w["w_uk"])


def _exp_scaled(x, scale):
    return jnp.exp(x) if scale == 1.0 else jnp.exp2(x * (scale * LOG2_E))


def _exp_accumulate(s, m, scale, v, l_ref, acc_ref):
    ps = [_exp_scaled(sl - m, scale) for sl in _lane_slabs(s)]
    l_ref[...] += functools.reduce(operator.add, ps)
    acc_ref[...] += _dot(jnp.concatenate(ps, axis=1).astype(BF16), v)


def _mla_prompt_kernel(q_ref, k_ref, wuv_ref, o_ref, s_ref, m_ref, l_ref, acc_ref, *, tq, tk, scale):
    qi = pl.program_id(1)
    rows = MLA_HEADS * tq
    q = q_ref[...].reshape(rows, QK_PAD)
    last = (qi * tq + tq - 1) // tk

    def k_chunk(kc):
        return k_ref[pl.ds(pl.multiple_of(kc * tk, tk), tk), :]

    def scores(kc, masked):
        s = _dot_nt(q, k_chunk(kc))
        if masked:
            q_pos = qi * tq + lax.broadcasted_iota(jnp.int32, (MLA_HEADS, tq, tk), 1).reshape(rows, tk)
            k_pos = kc * tk + lax.broadcasted_iota(jnp.int32, (rows, tk), 1)
            s = jnp.where(k_pos <= q_pos, s, NEG_INF)
        s_ref[kc] = s
        m_ref[...] = _slab_max(m_ref[...], s)

    m_ref[...] = jnp.full(m_ref.shape, NEG_INF, F32)

    def score_body(kc, carry):
        scores(kc, False)
        return carry

    lax.fori_loop(0, last, score_body, 0)
    scores(last, True)

    m_ref[...] = _row_max(m_ref[...])
    l_ref[...] = jnp.zeros(l_ref.shape, F32)
    acc_ref[...] = jnp.zeros(acc_ref.shape, F32)

    def value_body(kc, carry):
        _exp_accumulate(s_ref[kc], m_ref[...], scale, k_chunk(kc)[:, :KV_LORA], l_ref, acc_ref)
        return carry

    lax.fori_loop(0, last + 1, value_body, 0)

    o_lat = (acc_ref[...] / jnp.sum(l_ref[...], axis=1, keepdims=True)).astype(BF16)
    o = _dot(o_lat[:tq], wuv_ref[0])
    for h in range(1, MLA_HEADS):
        o = o + _dot(o_lat[h * tq:(h + 1) * tq], wuv_ref[h])
    o_ref[...] = o.astype(o_ref.dtype)


def _mla_prompt(qf, kf, w_uv, *, batch, seq, tq, tk):
    nq = seq // tq
    rows = MLA_HEADS * tq
    qf4 = qf.reshape(MLA_HEADS, batch, seq, QK_PAD)
    kf3 = kf.reshape(batch, seq, QK_PAD)
    scale = (QK_NOPE + QK_ROPE) ** -0.5
    return pl.pallas_call(
        functools.partial(_mla_prompt_kernel, tq=tq, tk=tk, scale=scale),
        grid=(batch, nq),
        in_specs=[
            pl.BlockSpec((MLA_HEADS, None, tq, QK_PAD), lambda b, i: (0, b, i, 0)),
            pl.BlockSpec((None, seq, QK_PAD), lambda b, i: (b, 0, 0)),
            _const_spec((MLA_HEADS, KV_LORA, BRANCH_WIDTH)),
        ],
        out_specs=pl.BlockSpec((tq, BRANCH_WIDTH), lambda b, i: (b * nq + i, 0)),
        out_shape=jax.ShapeDtypeStruct((batch * seq, BRANCH_WIDTH), BF16),
        scratch_shapes=[pltpu.VMEM((seq // tk, rows, tk), F32), pltpu.VMEM((rows, LANES), F32),
                        pltpu.VMEM((rows, LANES), F32), pltpu.VMEM((rows, KV_LORA), F32)],
        compiler_params=pltpu.CompilerParams(
            dimension_semantics=("arbitrary", "arbitrary"), vmem_limit_bytes=VMEM_LIMIT),
        name="mla_prompt",
    )(qf4, kf3, w_uv)


def _gate_scores(qf, kmean, kmean_is_transposed):
    mm = _dot if kmean_is_transposed else _dot_nt
    q_hi = qf.astype(BF16)
    q_lo = (qf - q_hi.astype(F32)).astype(BF16)
    k_hi = kmean.astype(BF16)
    k_lo = (kmean - k_hi.astype(F32)).astype(BF16)
    return mm(q_hi, k_hi) + (mm(q_hi, k_lo) + mm(q_lo, k_hi))


def _prescaled_query(qf, scale):
    if math.frexp(scale)[0] == 0.5:
        return (qf * scale).astype(BF16), 1.0
    return qf.astype(BF16), scale


def _times(x, factor):
    return x if factor == 1.0 else x * factor


def _topk_lanes(gate, valid, n_sel):
    nb = gate.shape[1]
    lane = lax.broadcasted_iota(jnp.int32, gate.shape, 1)
    g = jnp.where(valid, gate, NEG_INF)
    picked = jnp.zeros(gate.shape, jnp.bool_)
    for _ in range(n_sel):
        cur = jnp.where(picked, NEG_INF, g)
        best = jnp.max(cur, axis=1, keepdims=True)
        cand = jnp.logical_and(cur == best, jnp.logical_not(picked))
        idx = jnp.min(jnp.where(cand, lane, nb), axis=1, keepdims=True)
        picked = jnp.logical_or(picked, lane == idx)
    return jnp.logical_and(picked, valid)


def _moba_prompt_kernel(q_ref, kv_ref, kmean_ref, slope_ref, o_ref, bias_ref, s_ref, m_ref, l_ref, acc_ref,
                        *, nb):
    tq = MOBA_BLOCK
    qi = pl.program_id(1)
    rows = MOBA_HEADS * tq
    own_slot = nb - 1
    qf = q_ref[...].reshape(rows, MOBA_KV_WIDTH)
    q, scale = _prescaled_query(qf, MOBA_HEAD_DIM ** -0.5)
    slope = slope_ref[...]
    t_q = lax.broadcasted_iota(jnp.int32, (MOBA_HEADS, tq, 1), 1).reshape(rows, 1)
    off_k = lax.broadcasted_iota(jnp.int32, (1, tq), 1)

    blk = lax.broadcasted_iota(jnp.int32, (rows, nb), 1)
    n_sel = min(MOBA_TOPK, nb)

    @pl.when(qi <= n_sel)
    def _():
        bias_ref[...] = jnp.where(blk < qi, 0.0, NEG_INF)

    @pl.when(qi > n_sel)
    def _():
        gate = _gate_scores(qf, kmean_ref[...].reshape(nb, MOBA_KV_WIDTH), False)
        bias_ref[...] = jnp.where(_topk_lanes(gate, blk < qi, n_sel), 0.0, NEG_INF)

    dist_own = t_q - off_k
    alibi_own = -slope * dist_own.astype(F32)

    def block_rows(j):
        return pl.ds(j * tq if isinstance(j, int) else pl.multiple_of(j * tq, tq), tq)

    def keys(j):
        return kv_ref[block_rows(j), :MOBA_KV_WIDTH]

    def values(j):
        return kv_ref[block_rows(j), MOBA_KV_WIDTH:]

    def qk(j):
        return _times(_dot_nt(q, keys(j)), scale)

    s = jnp.where(dist_own >= 0, qk(qi) + alibi_own, NEG_INF)
    s_ref[own_slot] = s
    m_ref[...] = _slab_max(jnp.full(m_ref.shape, NEG_INF, F32), s)
    for j in range(nb - 1):
        @pl.when(j < qi)
        def _():
            shift = bias_ref[:, j:j + 1] - slope * ((qi - j) * tq).astype(F32)
            sj = qk(j) + alibi_own + shift
            s_ref[j] = sj
            m_ref[...] = _slab_max(m_ref[...], sj)

    m_ref[...] = _row_max(m_ref[...])
    l_ref[...] = jnp.zeros(l_ref.shape, F32)
    acc_ref[...] = jnp.zeros(acc_ref.shape, F32)
    _exp_accumulate(s_ref[own_slot], m_ref[...], 1.0, values(qi), l_ref, acc_ref)
    for j in range(nb - 1):
        @pl.when(j < qi)
        def _():
            _exp_accumulate(s_ref[j], m_ref[...], 1.0, values(j), l_ref, acc_ref)

    o = acc_ref[...] / jnp.sum(l_ref[...], axis=1, keepdims=True)
    pieces = []
    for h in range(MOBA_HEADS):
        g = h // MOBA_GROUP
        pieces.append(o[h * tq:(h + 1) * tq, g * MOBA_HEAD_DIM:(g + 1) * MOBA_HEAD_DIM])
    o_ref[...] = jnp.concatenate(pieces, axis=1).astype(o_ref.dtype)


def _moba_prompt(qm, mkv, kmean, slope_rows, *, batch, seq):
    tq = MOBA_BLOCK
    nb = seq // tq
    rows = MOBA_HEADS * tq
    qm4 = qm.reshape(MOBA_HEADS, batch, seq, MOBA_KV_WIDTH)
    mkv3 = mkv.reshape(batch, seq, 2 * MOBA_KV_WIDTH)
    kmean4 = kmean.reshape(batch, nb, 1, MOBA_KV_WIDTH)
    return pl.pallas_call(
        functools.partial(_moba_prompt_kernel, nb=nb),
        grid=(batch, nb),
        in_specs=[
            pl.BlockSpec((MOBA_HEADS, None, tq, MOBA_KV_WIDTH), lambda b, i: (0, b, i, 0)),
            pl.BlockSpec((None, seq, 2 * MOBA_KV_WIDTH), lambda b, i: (b, 0, 0)),
            pl.BlockSpec((None, nb, 1, MOBA_KV_WIDTH), lambda b, i: (b, 0, 0, 0)),
            _const_spec((rows, 1)),
        ],
        out_specs=pl.BlockSpec((tq, BRANCH_WIDTH), lambda b, i: (b * nb + i, 0)),
        out_shape=jax.ShapeDtypeStruct((batch * seq, BRANCH_WIDTH), BF16),
        scratch_shapes=[pltpu.VMEM((rows, nb), F32), pltpu.VMEM((nb, rows, tq), F32),
                        pltpu.VMEM((rows, LANES), F32), pltpu.VMEM((rows, LANES), F32),
                        pltpu.VMEM((rows, MOBA_KV_WIDTH), F32)],
        compiler_params=pltpu.CompilerParams(
            dimension_semantics=("arbitrary", "arbitrary"), vmem_limit_bytes=VMEM_LIMIT),
        name="moba_prompt",
    )(qm4, mkv3, kmean4, slope_rows)


def _page_copies(layer, hbm_refs, bufs, sems, page, slot, p):
    return [pltpu.make_async_copy(hbm.at[layer, page], buf.at[slot, p], sems.at[i, slot])
            for i, (hbm, buf) in enumerate(zip(hbm_refs, bufs))]


def _page_pipeline(pt_ref, layer, n_pages, hbm_refs, bufs, sems):
    b = pl.program_id(0)
    slot = b % 2
    copies = functools.partial(_page_copies, layer, hbm_refs, bufs, sems)

    def start_fetch(request, into):
        def body(p, carry):
            for cp in copies(pt_ref[request, p], into, p):
                cp.start()
            return carry
        lax.fori_loop(0, n_pages, body, 0, unroll=DMA_ISSUE_UNROLL)

    @pl.when(b == 0)
    def _():
        start_fetch(0, 0)

    @pl.when(b + 1 < pl.num_programs(0))
    def _():
        start_fetch(b + 1, 1 - slot)

    for p in range(n_pages):
        for cp in copies(0, slot, p):
            cp.wait()
    return slot


def _mla_sample_kernel(pt_ref, q_ref, nckv_ref, nkr_ref, ckv_hbm, krt_hbm, o_ref,
                       ckv_buf, krt_buf, sems, kb_ref, s_ref, m_ref, l_ref, acc_ref,
                       *, layer, dec_seq, n_pages, scale):
    slot = _page_pipeline(pt_ref, layer, n_pages, (ckv_hbm, krt_hbm), (ckv_buf, krt_buf), sems)
    rows = MLA_HEADS * dec_seq
    ch = min(MLA_CHUNK_PAGES, n_pages)
    ck = ch * PAGE_SIZE
    q = q_ref[...].reshape(rows, QK_PAD)
    q_lat = q[:, :KV_LORA].astype(BF16)
    q_rope = q[:, KV_LORA:KV_LORA + QK_ROPE].astype(BF16)

    kn = nckv_ref[...].astype(BF16)
    s_new = _dot_nt(q_lat, kn) + _dot_nt(q_rope, nkr_ref[...].astype(BF16))
    t_q = lax.broadcasted_iota(jnp.int32, (MLA_HEADS, dec_seq, dec_seq), 1).reshape(rows, dec_seq)
    t_k = lax.broadcasted_iota(jnp.int32, (rows, dec_seq), 1)
    s_new = jnp.where(t_k <= t_q, s_new, NEG_INF)
    m_new = jnp.max(s_new, axis=1, keepdims=True)
    p_new = _exp_scaled(s_new - m_new, scale)
    l_new = jnp.sum(p_new, axis=1, keepdims=True)
    acc_new = _dot(p_new.astype(BF16), kn)

    n_chunks = n_pages // ch
    for c in range(n_chunks):
        kb = ckv_buf[slot, c * ch:(c + 1) * ch].reshape(ck, KV_LORA).astype(BF16)
        krt = jnp.concatenate([krt_buf[slot, c * ch + i] for i in range(ch)], axis=1).astype(BF16)
        kb_ref[c] = kb
        s_ref[c] = _dot_nt(q_lat, kb) + _dot(q_rope, krt)

    def softmax(c):
        s = s_ref[c]
        mc = _row_max(_slab_max(jnp.full((rows, LANES), NEG_INF, F32), s))
        ps = [_exp_scaled(sl - mc, scale) for sl in _lane_slabs(s)]
        m_ref[c] = mc
        l_ref[c] = functools.reduce(operator.add, ps)
        return jnp.concatenate(ps, axis=1).astype(BF16)

    nxt = softmax(0)
    for c in range(n_chunks):
        p = nxt
        if c + 1 < n_chunks:
            nxt = softmax(c + 1)
        acc_ref[c] = _dot(p, kb_ref[c])

    m = functools.reduce(jnp.maximum, [m_ref[c] for c in range(n_chunks)],
                         jnp.broadcast_to(m_new, (rows, LANES)))
    l_lanes = jnp.zeros((rows, LANES), F32)
    acc = [jnp.zeros((rows, LANES), F32)] * (KV_LORA // LANES)
    for c in range(n_chunks):
        w = _exp_scaled(m_ref[c] - m, scale)
        l_lanes = l_lanes + w * l_ref[c]
        acc = [a + w * sl for a, sl in zip(acc, _lane_slabs(acc_ref[c]))]
    w_new = _exp_scaled(m_new - m[:, :1], scale)
    l = jnp.sum(l_lanes, axis=1, keepdims=True) + w_new * l_new
    o = (jnp.concatenate(acc, axis=1) + w_new * acc_new) / l
    o_ref[...] = o.reshape(MLA_HEADS, dec_seq, KV_LORA)


def _mla_sample(page_table, qf, new_ckv, new_kr, cache_ckv, cache_krope_t, layer, *, dec_batch, dec_seq):
    n_pages = page_table.shape[1]
    chunk_pages = min(MLA_CHUNK_PAGES, n_pages)
    n_chunks = n_pages // chunk_pages
    chunk_keys = chunk_pages * PAGE_SIZE
    rows = MLA_HEADS * dec_seq
    scale = (QK_NOPE + QK_ROPE) ** -0.5
    in_specs = [
        pl.BlockSpec((MLA_HEADS, dec_seq, QK_PAD), lambda b, pt: (0, b, 0)),
        pl.BlockSpec((dec_seq, KV_LORA), lambda b, pt: (b, 0)),
        pl.BlockSpec((dec_seq, QK_ROPE), lambda b, pt: (b, 0)),
        pl.BlockSpec(memory_space=pl.ANY),
        pl.BlockSpec(memory_space=pl.ANY),
    ]
    return pl.pallas_call(
        functools.partial(_mla_sample_kernel, layer=layer, dec_seq=dec_seq, n_pages=n_pages, scale=scale),
        grid_spec=pltpu.PrefetchScalarGridSpec(
            num_scalar_prefetch=1, grid=(dec_batch,), in_specs=in_specs,
            out_specs=pl.BlockSpec((MLA_HEADS, dec_seq, KV_LORA), lambda b, pt: (0, b, 0)),
            scratch_shapes=[pltpu.VMEM((2, n_pages, PAGE_SIZE, KV_LORA), F32),
                            pltpu.VMEM((2, n_pages, QK_ROPE, PAGE_SIZE), F32),
                            pltpu.SemaphoreType.DMA((2, 2)),
                            pltpu.VMEM((n_chunks, chunk_keys, KV_LORA), BF16),
                            pltpu.VMEM((n_chunks, rows, chunk_keys), F32),
                            pltpu.VMEM((n_chunks, rows, LANES), F32),
                            pltpu.VMEM((n_chunks, rows, LANES), F32),
                            pltpu.VMEM((n_chunks, rows, KV_LORA), F32)]),
        out_shape=jax.ShapeDtypeStruct((MLA_HEADS, dec_batch * dec_seq, KV_LORA), F32),
        compiler_params=pltpu.CompilerParams(
            dimension_semantics=("arbitrary",), vmem_limit_bytes=VMEM_LIMIT),
        name="mla_sample",
    )(page_table, qf, new_ckv, new_kr, cache_ckv, cache_krope_t)


def _uv_kernel(o_lat_ref, wuv_ref, o_ref):
    o = _dot(o_lat_ref[0].astype(BF16), wuv_ref[0])
    for h in range(1, MLA_HEADS):
        o = o + _dot(o_lat_ref[h].astype(BF16), wuv_ref[h])
    o_ref[...] = o


def _uv_sample(o_lat, w_uv):
    n = o_lat.shape[1]
    return pl.pallas_call(
        _uv_kernel,
        out_shape=jax.ShapeDtypeStruct((n, BRANCH_WIDTH), F32),
        compiler_params=pltpu.CompilerParams(vmem_limit_bytes=VMEM_LIMIT),
        name="uv_sample",
    )(o_lat, w_uv)


def _moba_sample_kernel(pt_ref, q_ref, nk_ref, nv_ref, slope_ref, kt_hbm, vt_hbm, o_ref,
                        kt_buf, vt_buf, sems, s_ref, m_ref, l_ref, acc_ref,
                        *, layer, dec_seq, n_pages, past_len):
    slot = _page_pipeline(pt_ref, layer, n_pages, (kt_hbm, vt_hbm), (kt_buf, vt_buf), sems)
    rows = MOBA_HEADS * dec_seq
    nb_past = n_pages // PAGES_PER_BLOCK
    wide = (rows, LANES)
    qf = q_ref[...].reshape(rows, MOBA_KV_WIDTH)
    q, scale = _prescaled_query(qf, MOBA_HEAD_DIM ** -0.5)
    slope = slope_ref[...]
    t_q = past_len + lax.broadcasted_iota(jnp.int32, (MOBA_HEADS, dec_seq, LANES), 1).reshape(wide)
    off_k = lax.broadcasted_iota(jnp.int32, (1, MOBA_BLOCK), 1).astype(F32)
    alibi_off = slope[:, :1] * off_k
    lane = lax.broadcasted_iota(jnp.int32, wide, 1)
    lane_km = lax.broadcasted_iota(jnp.int32, (MOBA_KV_WIDTH, LANES), 1)

    def block_t(buf, j):
        return jnp.concatenate([buf[slot, PAGES_PER_BLOCK * j + i] for i in range(PAGES_PER_BLOCK)], axis=1)

    km = jnp.zeros((MOBA_KV_WIDTH, LANES), F32)
    for j in range(nb_past):
        kt = block_t(kt_buf, j)
        km = jnp.where(lane_km == j, jnp.sum(kt, axis=1, keepdims=True) / MOBA_BLOCK, km)
        s_ref[j] = _dot(q, kt.astype(BF16))

    gate = _gate_scores(qf, km, True)
    sel = _topk_lanes(gate, lane < nb_past, min(MOBA_TOPK, nb_past))

    def softmax(j):
        shift = -slope * (t_q - j * MOBA_BLOCK).astype(F32)
        raw = s_ref[j]
        s = [sl + shift for sl in _lane_slabs(_times(raw, scale) + alibi_off)]
        mj = _row_max(functools.reduce(jnp.maximum, s))
        ps = [jnp.exp(sl - mj) for sl in s]
        m_ref[j] = mj
        l_ref[j] = jnp.broadcast_to(jnp.sum(functools.reduce(operator.add, ps), axis=1, keepdims=True), wide)
        return jnp.concatenate(ps, axis=1).astype(BF16)

    nxt = softmax(0)
    for j in range(nb_past):
        p = nxt
        if j + 1 < nb_past:
            nxt = softmax(j + 1)
        acc_ref[j] = _dot_nt(block_t(vt_buf, j).astype(BF16), p)

    kn = nk_ref[...].astype(BF16)
    t_k = lax.broadcasted_iota(jnp.int32, (1, dec_seq), 1)
    dist = (t_q[:, :1] - past_len) - t_k
    s = _times(_dot_nt(q, kn), scale) - slope[:, :1] * dist.astype(F32)
    s = jnp.where(dist >= 0, s, NEG_INF)
    m_own = jnp.max(s, axis=1, keepdims=True)
    p = jnp.exp(s - m_own)
    l_own = jnp.sum(p, axis=1, keepdims=True)
    acc_own = _dot(p.astype(BF16), nv_ref[...].astype(BF16))

    m_blk = jnp.zeros(wide, F32)
    l_blk = jnp.zeros(wide, F32)
    for j in range(nb_past):
        m_blk = jnp.where(lane == j, m_ref[j], m_blk)
        l_blk = jnp.where(lane == j, l_ref[j], l_blk)
    m_tot = jnp.maximum(m_own, jnp.max(jnp.where(sel, m_blk, NEG_INF), axis=1, keepdims=True))
    w_blk = jnp.where(sel, jnp.exp(m_blk - m_tot), 0.0)
    w_own = jnp.exp(m_own - m_tot)
    l_tot = jnp.sum(w_blk * l_blk, axis=1, keepdims=True) + w_own * l_own
    w_t = jnp.concatenate([w_blk, jnp.zeros((LANES - rows, LANES), F32)], axis=0).T[:, :rows]
    acc_t = jnp.zeros((MOBA_KV_WIDTH, rows), F32)
    for j in range(nb_past):
        acc_t = acc_t + w_t[j:j + 1, :] * acc_ref[j]
    acc_t = jnp.concatenate([acc_t, jnp.zeros((MOBA_KV_WIDTH, LANES - rows), F32)], axis=1)
    o = (acc_t.T[:rows, :] + w_own * acc_own) / l_tot
    pieces = []
    for h in range(MOBA_HEADS):
        g = h // MOBA_GROUP
        pieces.append(o[h * dec_seq:(h + 1) * dec_seq, g * MOBA_HEAD_DIM:(g + 1) * MOBA_HEAD_DIM])
    o_ref[...] = jnp.concatenate(pieces, axis=1)


def _moba_sample(page_table, qm, new_k, new_v, slope_rows, cache_kt, cache_vt, layer, *, dec_batch, dec_seq):
    n_pages = page_table.shape[1]
    past_len = n_pages * PAGE_SIZE
    nb_past = past_len // MOBA_BLOCK
    rows = MOBA_HEADS * dec_seq
    in_specs = [
        pl.BlockSpec((MOBA_HEADS, dec_seq, MOBA_KV_WIDTH), lambda b, pt: (0, b, 0)),
        pl.BlockSpec((dec_seq, MOBA_KV_WIDTH), lambda b, pt: (b, 0)),
        pl.BlockSpec((dec_seq, MOBA_KV_WIDTH), lambda b, pt: (b, 0)),
        pl.BlockSpec((rows, LANES), lambda b, pt: (0, 0)),
        pl.BlockSpec(memory_space=pl.ANY),
        pl.BlockSpec(memory_space=pl.ANY),
    ]
    page_buf = pltpu.VMEM((2, n_pages, MOBA_KV_WIDTH, PAGE_SIZE), F32)
    stat = pltpu.VMEM((nb_past, rows, LANES), F32)
    return pl.pallas_call(
        functools.partial(_moba_sample_kernel, layer=layer, dec_seq=dec_seq, n_pages=n_pages,
                          past_len=past_len),
        grid_spec=pltpu.PrefetchScalarGridSpec(
            num_scalar_prefetch=1, grid=(dec_batch,), in_specs=in_specs,
            out_specs=pl.BlockSpec((dec_seq, BRANCH_WIDTH), lambda b, pt: (b, 0)),
            scratch_shapes=[page_buf, page_buf, pltpu.SemaphoreType.DMA((2, 2)),
                            pltpu.VMEM((nb_past, rows, MOBA_BLOCK), F32), stat, stat,
                            pltpu.VMEM((nb_past, MOBA_KV_WIDTH, rows), F32)]),
        out_shape=jax.ShapeDtypeStruct((dec_batch * dec_seq, BRANCH_WIDTH), F32),
        compiler_params=pltpu.CompilerParams(
            dimension_semantics=("arbitrary",), vmem_limit_bytes=VMEM_LIMIT),
        name="moba_sample",
    )(page_table, qm, new_k, new_v, slope_rows, cache_kt, cache_vt)


def _ffn_kernel(x_ref, oa_ref, ob_ref, g_attn_ref, wgate_ref, wbr_ref, wo_ref, g_ffn_ref, wgu_ref, wdn_ref,
                g_fin_ref, out_ref, *, final, n_chunks):
    x = x_ref[...]
    xn = _rms(x, g_attn_ref[...]).astype(BF16)
    merged = (jax.nn.sigmoid(_dot(xn, wgate_ref[:, :D_MODEL])) * _dot(oa_ref[...].astype(BF16), wbr_ref[0])
              + jax.nn.sigmoid(_dot(xn, wgate_ref[:, D_MODEL:])) * _dot(ob_ref[...].astype(BF16), wbr_ref[1]))
    h = x + _dot(merged.astype(BF16), wo_ref[...])
    hn = _rms(h, g_ffn_ref[...]).astype(BF16)
    cw = FFN_HIDDEN // n_chunks
    acc = h
    for c in range(n_chunks):
        a = _dot(hn, wgu_ref[:, c * cw:(c + 1) * cw])
        u = _dot(hn, wgu_ref[:, FFN_HIDDEN + c * cw:FFN_HIDDEN + (c + 1) * cw])
        act = (jax.nn.silu(a) * u).astype(BF16)
        acc = acc + _dot(act, wdn_ref[c * cw:(c + 1) * cw, :])
    out_ref[...] = _rms(acc, g_fin_ref[...]) if final else acc


def _ffn(x, o_a, o_b, w, g_final, *, final, tm, n_chunks=2):
    n = x.shape[0]
    row = lambda i: (i, 0)
    return pl.pallas_call(
        functools.partial(_ffn_kernel, final=final, n_chunks=n_chunks),
        grid=(n // tm,),
        in_specs=[
            pl.BlockSpec((tm, D_MODEL), row),
            pl.BlockSpec((tm, BRANCH_WIDTH), row),
            pl.BlockSpec((tm, BRANCH_WIDTH), row),
            _const_spec((1, D_MODEL)),
            _const_spec((D_MODEL, 2 * D_MODEL)),
            _const_spec((2, BRANCH_WIDTH, D_MODEL)),
            _const_spec((D_MODEL, D_MODEL)),
            _const_spec((1, D_MODEL)),
            _const_spec((D_MODEL, 2 * FFN_HIDDEN)),
            _const_spec((FFN_HIDDEN, D_MODEL)),
            _const_spec((1, D_MODEL)),
        ],
        out_specs=pl.BlockSpec((tm, D_MODEL), row),
        out_shape=jax.ShapeDtypeStruct((n, D_MODEL), F32),
        compiler_params=pltpu.CompilerParams(
            dimension_semantics=("arbitrary",), vmem_limit_bytes=VMEM_LIMIT),
        name="ffn_final" if final else "ffn",
    )(x, o_a, o_b, w["g_attn"], w["w_gate"], w["w_branch"], w["w_o"], w["g_ffn"], w["w_gu"], w["w_down"],
      g_final)


def _prep_layer_weights(w_in, g_q, g_kv, w_uq, w_uk, w_uv, w_branch, w_o, g_attn, g_ffn, w_gu, w_down):
    s = [0, Q_LORA, Q_LORA + KV_LORA, Q_LORA + KV_LORA + QK_ROPE]
    s.append(s[-1] + MOBA_HEADS * MOBA_HEAD_DIM)
    s.append(s[-1] + MOBA_KV_WIDTH)
    s.append(s[-1] + MOBA_KV_WIDTH)
    c_q, c_kv, k_r, m_q, m_k, m_v, gate = (w_in[:, s[0]:s[1]], w_in[:, s[1]:s[2]], w_in[:, s[2]:s[3]],
                                           w_in[:, s[3]:s[4]], w_in[:, s[4]:s[5]], w_in[:, s[5]:s[6]],
                                           w_in[:, s[6]:])
    m_q = m_q.reshape(D_MODEL, MOBA_KV_HEADS, MOBA_GROUP, MOBA_HEAD_DIM)
    zeros = jnp.zeros_like(m_q[:, 0])
    m_q_bd = jnp.concatenate([
        jnp.concatenate([m_q[:, 0], zeros], axis=-1),
        jnp.concatenate([zeros, m_q[:, 1]], axis=-1)], axis=1).reshape(D_MODEL, MOBA_HEADS * LANES)
    w_in_perm = jnp.concatenate([c_q, c_kv, m_q_bd, m_k, m_v, k_r], axis=1).astype(BF16)

    uq = w_uq.reshape(Q_LORA, MLA_HEADS, QK_NOPE + QK_ROPE)
    w_uqn = jnp.pad(uq[:, :, :QK_NOPE], ((0, 0), (0, 0), (0, LANES - QK_NOPE))).reshape(Q_LORA, MLA_HEADS * LANES)
    w_uqr = jnp.concatenate([uq[:, :, QK_NOPE:QK_NOPE + HALF_ROPE].reshape(Q_LORA, LANES),
                             uq[:, :, QK_NOPE + HALF_ROPE:].reshape(Q_LORA, LANES)], axis=1)
    uk = jnp.transpose(w_uk, (1, 2, 0))
    uk = jnp.pad(uk, ((0, 0), (0, LANES - QK_NOPE), (0, 0)))
    uv = jnp.transpose(w_uv, (1, 0, 2))
    eye = jnp.eye(MLA_HEADS, dtype=w_uv.dtype)
    uv_pad = (uv[:, :, None, :] * eye[:, None, :, None]).reshape(MLA_HEADS, KV_LORA, BRANCH_WIDTH)
    return {
        "w_in": w_in_perm, "w_gate": gate.astype(BF16), "g_q": g_q[None], "g_kv": g_kv[None], "g_attn": g_attn[None], "g_ffn": g_ffn[None],
        "w_uqn": w_uqn.astype(BF16), "w_uqr": w_uqr.astype(BF16), "w_uk": uk.astype(BF16),
        "w_uv": uv_pad.astype(BF16), "w_branch": w_branch.astype(BF16), "w_o": w_o.astype(BF16),
        "w_gu": w_gu.astype(BF16), "w_down": w_down.astype(BF16),
    }


def _rope_tables(pos):
    inv = ROPE_THETA ** (-jnp.arange(HALF_ROPE, dtype=F32) / HALF_ROPE)
    ang = pos.astype(F32)[:, None] * inv[None, :]
    reps = LANES // HALF_ROPE
    return jnp.tile(jnp.cos(ang), (1, reps)), jnp.tile(jnp.sin(ang), (1, reps))


def _slope_rows(tokens_per_head, width):
    slopes = 2.0 ** (-8.0 * jnp.arange(1, MOBA_HEADS + 1, dtype=F32) / MOBA_HEADS)
    return jnp.broadcast_to(jnp.repeat(slopes, tokens_per_head)[:, None], (MOBA_HEADS * tokens_per_head, width))


def kernel(x_prompt, x_sample, cache_ckv, cache_krope, cache_k, cache_v, page_table, w_in, g_q, g_kv, w_uq, w_uk, w_uv, w_branch, w_o, g_attn, g_ffn, w_gu, w_down, g_final):
    batch, seq, _ = x_prompt.shape
    dec_batch, dec_seq, _ = x_sample.shape
    depth, n_phys = cache_k.shape[:2]
    past_len = page_table.shape[1] * PAGE_SIZE
    n_p, n_s = batch * seq, dec_batch * dec_seq
    tm_p = min(512, seq)
    tm_s = min(512, n_s)

    cache_krope_t = jnp.transpose(cache_krope, (0, 1, 3, 2))
    cache_kt = jnp.transpose(cache_k, (0, 1, 3, 4, 2)).reshape(depth, n_phys, MOBA_KV_WIDTH, PAGE_SIZE)
    cache_vt = jnp.transpose(cache_v, (0, 1, 3, 4, 2)).reshape(depth, n_phys, MOBA_KV_WIDTH, PAGE_SIZE)

    cos_p, sin_p = _rope_tables(jnp.arange(seq, dtype=jnp.int32))
    pos_s = past_len + jnp.arange(dec_seq, dtype=jnp.int32)
    cos_s, sin_s = _rope_tables(jnp.tile(pos_s, tm_s // dec_seq))
    slope_p = _slope_rows(MOBA_BLOCK, 1)
    slope_s = _slope_rows(dec_seq, LANES)
    g_fin = g_final[None]

    hp = x_prompt.reshape(n_p, D_MODEL)
    hs = x_sample.reshape(n_s, D_MODEL)
    rows_p, rows_s = [], []
    for l in range(depth):
        w = _prep_layer_weights(w_in[l], g_q[l], g_kv[l], w_uq[l], w_uk[l], w_uv[l], w_branch[l], w_o[l],
                                g_attn[l], g_ffn[l], w_gu[l], w_down[l])
        final = l == depth - 1

        ckv, kr, mk, mv, qf, qm, kf, mkv, kmean = _proj(hp, cos_p, sin_p, w, prompt=True, tm=tm_p)
        o_a = _mla_prompt(qf, kf, w["w_uv"], batch=batch, seq=seq, tq=min(256, seq), tk=min(256, seq))
        o_b = _moba_prompt(qm, mkv, kmean, slope_p, batch=batch, seq=seq)
        hp = _ffn(hp, o_a, o_b, w, g_fin, final=final, tm=tm_p)
        rows_p.append((ckv, kr, mk, mv))

        ckv, kr, mk, mv, qf, qm = _proj(hs, cos_s, sin_s, w, prompt=False, tm=tm_s)
        o_lat = _mla_sample(page_table, qf, ckv, kr, cache_ckv, cache_krope_t, l,
                            dec_batch=dec_batch, dec_seq=dec_seq)
        o_a = _uv_sample(o_lat, w["w_uv"])
        o_b = _moba_sample(page_table, qm, mk, mv, slope_s, cache_kt, cache_vt, l,
                           dec_batch=dec_batch, dec_seq=dec_seq)
        hs = _ffn(hs, o_a, o_b, w, g_fin, final=final, tm=tm_s)
        rows_s.append((ckv, kr, mk, mv))

    def stack(rows, i, shape):
        return jnp.stack([r[i] for r in rows]).reshape((depth,) + shape)

    kv_shape = (MOBA_KV_HEADS, MOBA_HEAD_DIM)
    return (hp.reshape(batch, seq, D_MODEL),
            hs.reshape(dec_batch, dec_seq, D_MODEL),
            stack(rows_p, 0, (batch, seq, KV_LORA)),
            stack(rows_p, 1, (batch, seq, QK_ROPE)),
            stack(rows_p, 2, (batch, seq) + kv_shape),
            stack(rows_p, 3, (batch, seq) + kv_shape),
            stack(rows_s, 0, (dec_batch, dec_seq, KV_LORA)),
            stack(rows_s, 1, (dec_batch, dec_seq, QK_ROPE)),
            stack(rows_s, 2, (dec_batch, dec_seq) + kv_shape),
            stack(rows_s, 3, (dec_batch, dec_seq) + kv_shape))
```

```python
import functools
import math
import operator

import jax
import jax.numpy as jnp
from jax import lax
from jax.experimental import pallas as pl
from jax.experimental.pallas import tpu as pltpu

F32 = jnp.float32
BF16 = jnp.bfloat16

D_MODEL = 1024
PAGE_SIZE = 128
MLA_HEADS = 8
Q_LORA = 384
KV_LORA = 256
QK_NOPE = 64
QK_ROPE = 32
V_HEAD = 64
ROPE_THETA = 10000.0
MOBA_HEADS = 8
MOBA_KV_HEADS = 2
MOBA_GROUP = MOBA_HEADS // MOBA_KV_HEADS
MOBA_HEAD_DIM = 64
MOBA_BLOCK = 256
MOBA_TOPK = 3
BRANCH_WIDTH = 512
FFN_HIDDEN = 2816
RMS_EPS = 1e-6

LANES = 128
QK_PAD = KV_LORA + LANES
MOBA_KV_WIDTH = MOBA_KV_HEADS * MOBA_HEAD_DIM
HALF_ROPE = QK_ROPE // 2
PAGES_PER_BLOCK = MOBA_BLOCK // PAGE_SIZE

_OFF_CQ = 0
_OFF_CKV = _OFF_CQ + Q_LORA
_OFF_MQ = _OFF_CKV + KV_LORA
_OFF_MK = _OFF_MQ + MOBA_HEADS * LANES
_OFF_MV = _OFF_MK + MOBA_KV_WIDTH
_OFF_KR = _OFF_MV + MOBA_KV_WIDTH
_D_IN_PERM = _OFF_KR + QK_ROPE

VMEM_LIMIT = 56 * 1024 * 1024
MLA_CHUNK_PAGES = 16
DMA_ISSUE_UNROLL = 4
NEG_INF = float("-inf")
LOG2_E = 1.4426950408889634


def _rms(x, g):
    return x * lax.rsqrt(jnp.mean(x * x, axis=-1, keepdims=True) + RMS_EPS) * g


def _dot(a, b):
    return jnp.dot(a, b, preferred_element_type=F32)


def _dot_nt(a, b):
    return lax.dot_general(a, b, (((1,), (1,)), ((), ())), preferred_element_type=F32)


def _const_spec(shape):
    return pl.BlockSpec(shape, lambda *_: (0,) * len(shape), pipeline_mode=pl.Buffered(1))


def _lane_slabs(x):
    return [x[:, j * LANES:(j + 1) * LANES] for j in range(x.shape[1] // LANES)]


def _slab_max(m, s):
    return functools.reduce(jnp.maximum, _lane_slabs(s), m)


def _row_max(m_slab):
    return jnp.broadcast_to(jnp.max(m_slab, axis=1, keepdims=True), m_slab.shape)


def _proj_kernel(x_ref, cos_ref, sin_ref, g_attn_ref, w_in_ref, g_q_ref, g_kv_ref,
                 w_uqn_ref, w_uqr_ref, w_uk_ref, *out_refs, prompt):
    if prompt:
        (ckv_ref, kr_ref, mk_ref, mv_ref, qf_ref, qm_ref,
         kf_ref, mkv_ref, kmean_ref) = out_refs
    else:
        ckv_ref, kr_ref, mk_ref, mv_ref, qf_ref, qm_ref = out_refs
    tm = x_ref.shape[0]
    hn = _rms(x_ref[...], g_attn_ref[...]).astype(BF16)

    def seg(lo, hi):
        return _dot(hn, w_in_ref[:, lo:hi])

    cos = cos_ref[...]
    sin = sin_ref[...]

    ckv = _rms(seg(_OFF_CKV, _OFF_MQ), g_kv_ref[...])
    ckv_ref[...] = ckv
    kr = seg(_OFF_KR, _D_IN_PERM)
    kr_swapped = jnp.concatenate([kr[:, HALF_ROPE:], kr[:, :HALF_ROPE]], axis=1)
    lane32 = lax.broadcasted_iota(jnp.int32, (tm, QK_ROPE), 1)
    sin_signed = jnp.where(lane32 < HALF_ROPE, -sin[:, :QK_ROPE], sin[:, :QK_ROPE])
    kr = kr * cos[:, :QK_ROPE] + kr_swapped * sin_signed
    kr_ref[...] = kr

    mk = seg(_OFF_MK, _OFF_MV)
    mv = seg(_OFF_MV, _OFF_KR)
    mk_ref[...] = mk
    mv_ref[...] = mv

    mq = seg(_OFF_MQ, _OFF_MK)
    for h in range(MOBA_HEADS):
        qm_ref[h] = mq[:, h * LANES:(h + 1) * LANES].astype(qm_ref.dtype)

    cq = _rms(seg(_OFF_CQ, _OFF_CKV), g_q_ref[...]).astype(BF16)
    q_nope = _dot(cq, w_uqn_ref[...]).astype(BF16)
    q_rot = _dot(cq, w_uqr_ref[...])
    r1, r2 = q_rot[:, :LANES], q_rot[:, LANES:]
    o1 = r1 * cos - r2 * sin
    o2 = r2 * cos + r1 * sin
    lane = lax.broadcasted_iota(jnp.int32, (tm, LANES), 1)
    for h in range(MLA_HEADS):
        q_lat = _dot(q_nope[:, h * LANES:(h + 1) * LANES], w_uk_ref[h])
        shift_a = (LANES - HALF_ROPE * h) % LANES
        shift_b = (HALF_ROPE - HALF_ROPE * h) % LANES
        a = o1 if shift_a == 0 else pltpu.roll(o1, shift_a, axis=1)
        b = o2 if shift_b == 0 else pltpu.roll(o2, shift_b, axis=1)
        rope = jnp.where(lane < HALF_ROPE, a, jnp.where(lane < QK_ROPE, b, 0.0))
        qf_ref[h, :, :KV_LORA] = q_lat.astype(qf_ref.dtype)
        qf_ref[h, :, KV_LORA:] = rope.astype(qf_ref.dtype)

    if prompt:
        kf_ref[:, :KV_LORA] = ckv.astype(BF16)
        kr_pad = jnp.concatenate([kr, jnp.zeros((tm, LANES - QK_ROPE), F32)], axis=1)
        kf_ref[:, KV_LORA:] = kr_pad.astype(BF16)
        mkv_ref[:, :MOBA_KV_WIDTH] = mk.astype(BF16)
        mkv_ref[:, MOBA_KV_WIDTH:] = mv.astype(BF16)
        for j in range(tm // MOBA_BLOCK):
            kmean_ref[j] = jnp.mean(mk[j * MOBA_BLOCK:(j + 1) * MOBA_BLOCK], axis=0, keepdims=True)


def _proj(x, cos_tab, sin_tab, w, *, prompt, tm):
    n = x.shape[0]
    n_tab = cos_tab.shape[0] // tm
    grid = (n // tm,)
    row = lambda i: (i, 0)
    q_dtype = BF16 if prompt else F32
    out_shape = [
        jax.ShapeDtypeStruct((n, KV_LORA), F32),
        jax.ShapeDtypeStruct((n, QK_ROPE), F32),
        jax.ShapeDtypeStruct((n, MOBA_KV_WIDTH), F32),
        jax.ShapeDtypeStruct((n, MOBA_KV_WIDTH), F32),
        jax.ShapeDtypeStruct((MLA_HEADS, n, QK_PAD), q_dtype),
        jax.ShapeDtypeStruct((MOBA_HEADS, n, MOBA_KV_WIDTH), F32),
    ]
    out_specs = [
        pl.BlockSpec((tm, KV_LORA), row),
        pl.BlockSpec((tm, QK_ROPE), row),
        pl.BlockSpec((tm, MOBA_KV_WIDTH), row),
        pl.BlockSpec((tm, MOBA_KV_WIDTH), row),
        pl.BlockSpec((MLA_HEADS, tm, QK_PAD), lambda i: (0, i, 0)),
        pl.BlockSpec((MOBA_HEADS, tm, MOBA_KV_WIDTH), lambda i: (0, i, 0)),
    ]
    if prompt:
        nblk = tm // MOBA_BLOCK
        out_shape += [
            jax.ShapeDtypeStruct((n, QK_PAD), BF16),
            jax.ShapeDtypeStruct((n, 2 * MOBA_KV_WIDTH), BF16),
            jax.ShapeDtypeStruct((n // MOBA_BLOCK, 1, MOBA_KV_WIDTH), F32),
        ]
        out_specs += [
            pl.BlockSpec((tm, QK_PAD), row),
            pl.BlockSpec((tm, 2 * MOBA_KV_WIDTH), row),
            pl.BlockSpec((nblk, 1, MOBA_KV_WIDTH), lambda i: (i, 0, 0)),
        ]
    tab = lambda i: (i % n_tab, 0)
    in_specs = [
        pl.BlockSpec((tm, D_MODEL), row),
        pl.BlockSpec((tm, LANES), tab),
        pl.BlockSpec((tm, LANES), tab),
        _const_spec((1, D_MODEL)),
        _const_spec((D_MODEL, _D_IN_PERM)),
        _const_spec((1, Q_LORA)),
        _const_spec((1, KV_LORA)),
        _const_spec((Q_LORA, MLA_HEADS * LANES)),
        _const_spec((Q_LORA, 2 * LANES)),
        _const_spec((MLA_HEADS, LANES, KV_LORA)),
    ]
    return pl.pallas_call(
        functools.partial(_proj_kernel, prompt=prompt),
        grid=grid, in_specs=in_specs, out_specs=out_specs, out_shape=out_shape,
        compiler_params=pltpu.CompilerParams(
            dimension_semantics=("arbitrary",), vmem_limit_bytes=VMEM_LIMIT),
        name="proj_prompt" if prompt else "proj_sample",
    )(x, cos_tab, sin_tab, w["g_attn"], w["w_in"], w["g_q"], w["g_kv"],
      w["w_uqn"], w["w_uqr"], w["w_uk"])


def _exp_scaled(x, scale):
    return jnp.exp(x) if scale == 1.0 else jnp.exp2(x * (scale * LOG2_E))


def _exp_accumulate(s, m, scale, v, l_ref, acc_ref):
    ps = [_exp_scaled(sl - m, scale) for sl in _lane_slabs(s)]
    l_ref[...] += functools.reduce(operator.add, ps)
    acc_ref[...] += _dot(jnp.concatenate(ps, axis=1).astype(BF16), v)


def _mla_prompt_kernel(q_ref, k_ref, wuv_ref, o_ref, s_ref, m_ref, l_ref, acc_ref, *, tq, tk, scale):
    qi = pl.program_id(1)
    rows = MLA_HEADS * tq
    q = q_ref[...].reshape(rows, QK_PAD)
    last = (qi * tq + tq - 1) // tk

    def k_chunk(kc):
        return k_ref[pl.ds(pl.multiple_of(kc * tk, tk), tk), :]

    def scores(kc, masked):
        s = _dot_nt(q, k_chunk(kc))
        if masked:
            q_pos = qi * tq + lax.broadcasted_iota(jnp.int32, (MLA_HEADS, tq, tk), 1).reshape(rows, tk)
            k_pos = kc * tk + lax.broadcasted_iota(jnp.int32, (rows, tk), 1)
            s = jnp.where(k_pos <= q_pos, s, NEG_INF)
        s_ref[kc] = s
        m_ref[...] = _slab_max(m_ref[...], s)

    m_ref[...] = jnp.full(m_ref.shape, NEG_INF, F32)

    def score_body(kc, carry):
        scores(kc, False)
        return carry

    lax.fori_loop(0, last, score_body, 0)
    scores(last, True)

    m_ref[...] = _row_max(m_ref[...])
    l_ref[...] = jnp.zeros(l_ref.shape, F32)
    acc_ref[...] = jnp.zeros(acc_ref.shape, F32)

    def value_body(kc, carry):
        _exp_accumulate(s_ref[kc], m_ref[...], scale, k_chunk(kc)[:, :KV_LORA], l_ref, acc_ref)
        return carry

    lax.fori_loop(0, last + 1, value_body, 0)

    o_lat = (acc_ref[...] / jnp.sum(l_ref[...], axis=1, keepdims=True)).astype(BF16)
    o = _dot(o_lat[:tq], wuv_ref[0])
    for h in range(1, MLA_HEADS):
        o = o + _dot(o_lat[h * tq:(h + 1) * tq], wuv_ref[h])
    o_ref[...] = o.astype(o_ref.dtype)


def _mla_prompt(qf, kf, w_uv, *, batch, seq, tq, tk):
    nq = seq // tq
    rows = MLA_HEADS * tq
    qf4 = qf.reshape(MLA_HEADS, batch, seq, QK_PAD)
    kf3 = kf.reshape(batch, seq, QK_PAD)
    scale = (QK_NOPE + QK_ROPE) ** -0.5
    return pl.pallas_call(
        functools.partial(_mla_prompt_kernel, tq=tq, tk=tk, scale=scale),
        grid=(batch, nq),
        in_specs=[
            pl.BlockSpec((MLA_HEADS, None, tq, QK_PAD), lambda b, i: (0, b, i, 0)),
            pl.BlockSpec((None, seq, QK_PAD), lambda b, i: (b, 0, 0)),
            _const_spec((MLA_HEADS, KV_LORA, BRANCH_WIDTH)),
        ],
        out_specs=pl.BlockSpec((tq, BRANCH_WIDTH), lambda b, i: (b * nq + i, 0)),
        out_shape=jax.ShapeDtypeStruct((batch * seq, BRANCH_WIDTH), BF16),
        scratch_shapes=[pltpu.VMEM((seq // tk, rows, tk), F32), pltpu.VMEM((rows, LANES), F32),
                        pltpu.VMEM((rows, LANES), F32), pltpu.VMEM((rows, KV_LORA), F32)],
        compiler_params=pltpu.CompilerParams(
            dimension_semantics=("arbitrary", "arbitrary"), vmem_limit_bytes=VMEM_LIMIT),
        name="mla_prompt",
    )(qf4, kf3, w_uv)


def _hi_lo(x):
    hi = x.astype(BF16)
    return hi, (x - hi.astype(F32)).astype(BF16)


def _gate_scores(qf, kmean, kmean_is_transposed):
    mm = _dot if kmean_is_transposed else _dot_nt
    q_hi, q_lo = _hi_lo(qf)
    k_hi, k_lo = _hi_lo(kmean)
    return mm(q_hi, k_hi) + (mm(q_hi, k_lo) + mm(q_lo, k_hi))


def _gate_scores_t(qf, kmean):
    q_hi, q_lo = _hi_lo(qf)
    k_hi, k_lo = _hi_lo(kmean)
    return _dot_nt(jnp.concatenate([k_hi, k_lo, k_hi], axis=1), jnp.concatenate([q_hi, q_hi, q_lo], axis=1))


def _prescaled_query(qf, scale):
    if math.frexp(scale)[0] == 0.5:
        return (qf * scale).astype(BF16), 1.0
    return qf.astype(BF16), scale


def _times(x, factor):
    return x if factor == 1.0 else x * factor


def _topk_blocks(gate, valid, n_sel, axis):
    nb = gate.shape[axis]
    blk = lax.broadcasted_iota(jnp.int32, gate.shape, axis).astype(F32)
    g = jnp.where(valid, gate, NEG_INF)
    picked = jnp.zeros(gate.shape, jnp.bool_)
    for _ in range(n_sel):
        cur = jnp.where(picked, NEG_INF, g)
        best = jnp.max(cur, axis=axis, keepdims=True)
        cand = jnp.logical_and(cur == best, jnp.logical_not(picked))
        idx = jnp.min(jnp.where(cand, blk, float(nb)), axis=axis, keepdims=True)
        picked = jnp.logical_or(picked, blk == idx)
    return jnp.logical_and(picked, valid)


def _moba_prompt_kernel(q_ref, kv_ref, kmean_ref, slope_ref, slope_t_ref, o_ref,
                        choice_ref, shift_ref, s_ref, m_ref, l_ref, acc_ref, *, nb):
    tq = MOBA_BLOCK
    qi = pl.program_id(1)
    rows = MOBA_HEADS * tq
    own_slot = nb - 1
    qf = q_ref[...].reshape(rows, MOBA_KV_WIDTH)
    q, scale = _prescaled_query(qf, MOBA_HEAD_DIM ** -0.5)
    slope = slope_ref[...]
    t_q = lax.broadcasted_iota(jnp.int32, (MOBA_HEADS, tq, 1), 1).reshape(rows, 1)
    off_k = lax.broadcasted_iota(jnp.int32, (1, tq), 1)

    blk_t = lax.broadcasted_iota(jnp.int32, (nb, rows), 0)
    n_sel = min(MOBA_TOPK, nb)

    @pl.when(qi <= n_sel)
    def _():
        choice_ref[...] = jnp.where(blk_t < qi, 0.0, NEG_INF)

    @pl.when(qi > n_sel)
    def _():
        gate_t = _gate_scores_t(qf, kmean_ref[...].reshape(nb, MOBA_KV_WIDTH))
        choice_ref[...] = jnp.where(_topk_blocks(gate_t, blk_t < qi, n_sel, 0), 0.0, NEG_INF)

    shift_t = choice_ref[...] - slope_t_ref[...] * ((qi - blk_t) * tq).astype(F32)
    shift_t = jnp.concatenate([shift_t, jnp.zeros((LANES - nb, rows), F32)], axis=0)
    shift_ref[...] = shift_t.T[:, :nb]
    dist_own = t_q - off_k
    alibi_own = -slope * dist_own.astype(F32)

    def block_rows(j):
        return pl.ds(j * tq if isinstance(j, int) else pl.multiple_of(j * tq, tq), tq)

    def keys(j):
        return kv_ref[block_rows(j), :MOBA_KV_WIDTH]

    def values(j):
        return kv_ref[block_rows(j), MOBA_KV_WIDTH:]

    def qk(j):
        return _times(_dot_nt(q, keys(j)), scale)

    s = jnp.where(dist_own >= 0, qk(qi) + alibi_own, NEG_INF)
    s_ref[own_slot] = s
    m_ref[...] = _slab_max(jnp.full(m_ref.shape, NEG_INF, F32), s)
    for j in range(nb - 1):
        @pl.when(j < qi)
        def _():
            sj = qk(j) + alibi_own + shift_ref[:, j:j + 1]
            s_ref[j] = sj
            m_ref[...] = _slab_max(m_ref[...], sj)

    m_ref[...] = _row_max(m_ref[...])
    l_ref[...] = jnp.zeros(l_ref.shape, F32)
    acc_ref[...] = jnp.zeros(acc_ref.shape, F32)
    _exp_accumulate(s_ref[own_slot], m_ref[...], 1.0, values(qi), l_ref, acc_ref)
    for j in range(nb - 1):
        @pl.when(j < qi)
        def _():
            _exp_accumulate(s_ref[j], m_ref[...], 1.0, values(j), l_ref, acc_ref)

    o = acc_ref[...] / jnp.sum(l_ref[...], axis=1, keepdims=True)
    pieces = []
    for h in range(MOBA_HEADS):
        g = h // MOBA_GROUP
        pieces.append(o[h * tq:(h + 1) * tq, g * MOBA_HEAD_DIM:(g + 1) * MOBA_HEAD_DIM])
    o_ref[...] = jnp.concatenate(pieces, axis=1).astype(o_ref.dtype)


def _moba_prompt(qm, mkv, kmean, slope_rows, *, batch, seq):
    tq = MOBA_BLOCK
    nb = seq // tq
    rows = MOBA_HEADS * tq
    qm4 = qm.reshape(MOBA_HEADS, batch, seq, MOBA_KV_WIDTH)
    mkv3 = mkv.reshape(batch, seq, 2 * MOBA_KV_WIDTH)
    kmean4 = kmean.reshape(batch, nb, 1, MOBA_KV_WIDTH)
    return pl.pallas_call(
        functools.partial(_moba_prompt_kernel, nb=nb),
        grid=(batch, nb),
        in_specs=[
            pl.BlockSpec((MOBA_HEADS, None, tq, MOBA_KV_WIDTH), lambda b, i: (0, b, i, 0)),
            pl.BlockSpec((None, seq, 2 * MOBA_KV_WIDTH), lambda b, i: (b, 0, 0)),
            pl.BlockSpec((None, nb, 1, MOBA_KV_WIDTH), lambda b, i: (b, 0, 0, 0)),
            _const_spec((rows, 1)),
            _const_spec((1, rows)),
        ],
        out_specs=pl.BlockSpec((tq, BRANCH_WIDTH), lambda b, i: (b * nb + i, 0)),
        out_shape=jax.ShapeDtypeStruct((batch * seq, BRANCH_WIDTH), BF16),
        scratch_shapes=[pltpu.VMEM((nb, rows), F32), pltpu.VMEM((rows, nb), F32),
                        pltpu.VMEM((nb, rows, tq), F32),
                        pltpu.VMEM((rows, LANES), F32), pltpu.VMEM((rows, LANES), F32),
                        pltpu.VMEM((rows, MOBA_KV_WIDTH), F32)],
        compiler_params=pltpu.CompilerParams(
            dimension_semantics=("arbitrary", "arbitrary"), vmem_limit_bytes=VMEM_LIMIT),
        name="moba_prompt",
    )(qm4, mkv3, kmean4, slope_rows, slope_rows.reshape(1, rows))


def _page_copies(layer, hbm_refs, bufs, sems, page, slot, p):
    return [pltpu.make_async_copy(hbm.at[layer, page], buf.at[slot, p], sems.at[i, slot])
            for i, (hbm, buf) in enumerate(zip(hbm_refs, bufs))]


def _page_pipeline(pt_ref, layer, n_pages, hbm_refs, bufs, sems):
    b = pl.program_id(0)
    slot = b % 2
    copies = functools.partial(_page_copies, layer, hbm_refs, bufs, sems)

    def start_fetch(request, into):
        def body(p, carry):
            for cp in copies(pt_ref[request, p], into, p):
                cp.start()
            return carry
        lax.fori_loop(0, n_pages, body, 0, unroll=DMA_ISSUE_UNROLL)

    @pl.when(b == 0)
    def _():
        start_fetch(0, 0)

    @pl.when(b + 1 < pl.num_programs(0))
    def _():
        start_fetch(b + 1, 1 - slot)

    for p in range(n_pages):
        for cp in copies(0, slot, p):
            cp.wait()
    return slot


def _mla_sample_kernel(pt_ref, q_ref, nckv_ref, nkr_ref, ckv_hbm, krt_hbm, o_ref,
                       ckv_buf, krt_buf, sems, kb_ref, s_ref, m_ref, l_ref, acc_ref,
                       *, layer, dec_seq, n_pages, scale):
    slot = _page_pipeline(pt_ref, layer, n_pages, (ckv_hbm, krt_hbm), (ckv_buf, krt_buf), sems)
    rows = MLA_HEADS * dec_seq
    ch = min(MLA_CHUNK_PAGES, n_pages)
    ck = ch * PAGE_SIZE
    q = q_ref[...].reshape(rows, QK_PAD)
    q_lat = q[:, :KV_LORA].astype(BF16)
    q_rope = q[:, KV_LORA:KV_LORA + QK_ROPE].astype(BF16)

    kn = nckv_ref[...].astype(BF16)
    s_new = _dot_nt(q_lat, kn) + _dot_nt(q_rope, nkr_ref[...].astype(BF16))
    t_q = lax.broadcasted_iota(jnp.int32, (MLA_HEADS, dec_seq, dec_seq), 1).reshape(rows, dec_seq)
    t_k = lax.broadcasted_iota(jnp.int32, (rows, dec_seq), 1)
    s_new = jnp.where(t_k <= t_q, s_new, NEG_INF)
    m_new = jnp.max(s_new, axis=1, keepdims=True)
    p_new = _exp_scaled(s_new - m_new, scale)
    l_new = jnp.sum(p_new, axis=1, keepdims=True)
    acc_new = _dot(p_new.astype(BF16), kn)

    n_chunks = n_pages // ch
    for c in range(n_chunks):
        kb = ckv_buf[slot, c * ch:(c + 1) * ch].reshape(ck, KV_LORA).astype(BF16)
        krt = jnp.concatenate([krt_buf[slot, c * ch + i] for i in range(ch)], axis=1).astype(BF16)
        kb_ref[c] = kb
        s_ref[c] = _dot_nt(q_lat, kb) + _dot(q_rope, krt)

    def softmax(c):
        s = s_ref[c]
        mc = _row_max(_slab_max(jnp.full((rows, LANES), NEG_INF, F32), s))
        ps = [_exp_scaled(sl - mc, scale) for sl in _lane_slabs(s)]
        m_ref[c] = mc
        l_ref[c] = functools.reduce(operator.add, ps)
        return jnp.concatenate(ps, axis=1).astype(BF16)

    nxt = softmax(0)
    for c in range(n_chunks):
        p = nxt
        if c + 1 < n_chunks:
            nxt = softmax(c + 1)
        acc_ref[c] = _dot(p, kb_ref[c])

    m = functools.reduce(jnp.maximum, [m_ref[c] for c in range(n_chunks)],
                         jnp.broadcast_to(m_new, (rows, LANES)))
    l_lanes = jnp.zeros((rows, LANES), F32)
    acc = [jnp.zeros((rows, LANES), F32)] * (KV_LORA // LANES)
    for c in range(n_chunks):
        w = _exp_scaled(m_ref[c] - m, scale)
        l_lanes = l_lanes + w * l_ref[c]
        acc = [a + w * sl for a, sl in zip(acc, _lane_slabs(acc_ref[c]))]
    w_new = _exp_scaled(m_new - m[:, :1], scale)
    l = jnp.sum(l_lanes, axis=1, keepdims=True) + w_new * l_new
    o = (jnp.concatenate(acc, axis=1) + w_new * acc_new) / l
    o_ref[...] = o.reshape(MLA_HEADS, dec_seq, KV_LORA)


def _mla_sample(page_table, qf, new_ckv, new_kr, cache_ckv, cache_krope_t, layer, *, dec_batch, dec_seq):
    n_pages = page_table.shape[1]
    chunk_pages = min(MLA_CHUNK_PAGES, n_pages)
    n_chunks = n_pages // chunk_pages
    chunk_keys = chunk_pages * PAGE_SIZE
    rows = MLA_HEADS * dec_seq
    scale = (QK_NOPE + QK_ROPE) ** -0.5
    in_specs = [
        pl.BlockSpec((MLA_HEADS, dec_seq, QK_PAD), lambda b, pt: (0, b, 0)),
        pl.BlockSpec((dec_seq, KV_LORA), lambda b, pt: (b, 0)),
        pl.BlockSpec((dec_seq, QK_ROPE), lambda b, pt: (b, 0)),
        pl.BlockSpec(memory_space=pl.ANY),
        pl.BlockSpec(memory_space=pl.ANY),
    ]
    return pl.pallas_call(
        functools.partial(_mla_sample_kernel, layer=layer, dec_seq=dec_seq, n_pages=n_pages, scale=scale),
        grid_spec=pltpu.PrefetchScalarGridSpec(
            num_scalar_prefetch=1, grid=(dec_batch,), in_specs=in_specs,
            out_specs=pl.BlockSpec((MLA_HEADS, dec_seq, KV_LORA), lambda b, pt: (0, b, 0)),
            scratch_shapes=[pltpu.VMEM((2, n_pages, PAGE_SIZE, KV_LORA), F32),
                            pltpu.VMEM((2, n_pages, QK_ROPE, PAGE_SIZE), F32),
                            pltpu.SemaphoreType.DMA((2, 2)),
                            pltpu.VMEM((n_chunks, chunk_keys, KV_LORA), BF16),
                            pltpu.VMEM((n_chunks, rows, chunk_keys), F32),
                            pltpu.VMEM((n_chunks, rows, LANES), F32),
                            pltpu.VMEM((n_chunks, rows, LANES), F32),
                            pltpu.VMEM((n_chunks, rows, KV_LORA), F32)]),
        out_shape=jax.ShapeDtypeStruct((MLA_HEADS, dec_batch * dec_seq, KV_LORA), F32),
        compiler_params=pltpu.CompilerParams(
            dimension_semantics=("arbitrary",), vmem_limit_bytes=VMEM_LIMIT),
        name="mla_sample",
    )(page_table, qf, new_ckv, new_kr, cache_ckv, cache_krope_t)


def _uv_kernel(o_lat_ref, wuv_ref, o_ref):
    o = _dot(o_lat_ref[0].astype(BF16), wuv_ref[0])
    for h in range(1, MLA_HEADS):
        o = o + _dot(o_lat_ref[h].astype(BF16), wuv_ref[h])
    o_ref[...] = o


def _uv_sample(o_lat, w_uv):
    n = o_lat.shape[1]
    return pl.pallas_call(
        _uv_kernel,
        out_shape=jax.ShapeDtypeStruct((n, BRANCH_WIDTH), F32),
        compiler_params=pltpu.CompilerParams(vmem_limit_bytes=VMEM_LIMIT),
        name="uv_sample",
    )(o_lat, w_uv)


def _moba_sample_kernel(pt_ref, q_ref, nk_ref, nv_ref, slope_ref, kt_hbm, vt_hbm, o_ref,
                        kt_buf, vt_buf, sems, s_ref, m_ref, l_ref, acc_ref,
                        *, layer, dec_seq, n_pages, past_len):
    slot = _page_pipeline(pt_ref, layer, n_pages, (kt_hbm, vt_hbm), (kt_buf, vt_buf), sems)
    rows = MOBA_HEADS * dec_seq
    nb_past = n_pages // PAGES_PER_BLOCK
    wide = (rows, LANES)
    qf = q_ref[...].reshape(rows, MOBA_KV_WIDTH)
    q, scale = _prescaled_query(qf, MOBA_HEAD_DIM ** -0.5)
    slope = slope_ref[...]
    t_q = past_len + lax.broadcasted_iota(jnp.int32, (MOBA_HEADS, dec_seq, LANES), 1).reshape(wide)
    off_k = lax.broadcasted_iota(jnp.int32, (1, MOBA_BLOCK), 1).astype(F32)
    alibi_off = slope[:, :1] * off_k
    lane = lax.broadcasted_iota(jnp.int32, wide, 1)
    lane_km = lax.broadcasted_iota(jnp.int32, (MOBA_KV_WIDTH, LANES), 1)

    def block_t(buf, j):
        return jnp.concatenate([buf[slot, PAGES_PER_BLOCK * j + i] for i in range(PAGES_PER_BLOCK)], axis=1)

    km = jnp.zeros((MOBA_KV_WIDTH, LANES), F32)
    for j in range(nb_past):
        kt = block_t(kt_buf, j)
        km = jnp.where(lane_km == j, jnp.sum(kt, axis=1, keepdims=True) / MOBA_BLOCK, km)
        s_ref[j] = _dot(q, kt.astype(BF16))

    gate = _gate_scores(qf, km, True)
    sel = _topk_blocks(gate, lane < nb_past, min(MOBA_TOPK, nb_past), 1)

    def softmax(j):
        shift = -slope * (t_q - j * MOBA_BLOCK).astype(F32)
        raw = s_ref[j]
        s = [sl + shift for sl in _lane_slabs(_times(raw, scale) + alibi_off)]
        mj = _row_max(functools.reduce(jnp.maximum, s))
        ps = [jnp.exp(sl - mj) for sl in s]
        m_ref[j] = mj
        l_ref[j] = jnp.broadcast_to(jnp.sum(functools.reduce(operator.add, ps), axis=1, keepdims=True), wide)
        return jnp.concatenate(ps, axis=1).astype(BF16)

    nxt = softmax(0)
    for j in range(nb_past):
        p = nxt
        if j + 1 < nb_past:
            nxt = softmax(j + 1)
        acc_ref[j] = _dot_nt(block_t(vt_buf, j).astype(BF16), p)

    kn = nk_ref[...].astype(BF16)
    t_k = lax.broadcasted_iota(jnp.int32, (1, dec_seq), 1)
    dist = (t_q[:, :1] - past_len) - t_k
    s = _times(_dot_nt(q, kn), scale) - slope[:, :1] * dist.astype(F32)
    s = jnp.where(dist >= 0, s, NEG_INF)
    m_own = jnp.max(s, axis=1, keepdims=True)
    p = jnp.exp(s - m_own)
    l_own = jnp.sum(p, axis=1, keepdims=True)
    acc_own = _dot(p.astype(BF16), nv_ref[...].astype(BF16))

    m_blk = jnp.zeros(wide, F32)
    l_blk = jnp.zeros(wide, F32)
    for j in range(nb_past):
        m_blk = jnp.where(lane == j, m_ref[j], m_blk)
        l_blk = jnp.where(lane == j, l_ref[j], l_blk)
    m_tot = jnp.maximum(m_own, jnp.max(jnp.where(sel, m_blk, NEG_INF), axis=1, keepdims=True))
    w_blk = jnp.where(sel, jnp.exp(m_blk - m_tot), 0.0)
    w_own = jnp.exp(m_own - m_tot)
    l_tot = jnp.sum(w_blk * l_blk, axis=1, keepdims=True) + w_own * l_own
    w_t = jnp.concatenate([w_blk, jnp.zeros((LANES - rows, LANES), F32)], axis=0).T[:, :rows]
    acc_t = jnp.zeros((MOBA_KV_WIDTH, rows), F32)
    for j in range(nb_past):
        acc_t = acc_t + w_t[j:j + 1, :] * acc_ref[j]
    acc_t = jnp.concatenate([acc_t, jnp.zeros((MOBA_KV_WIDTH, LANES - rows), F32)], axis=1)
    o = (acc_t.T[:rows, :] + w_own * acc_own) / l_tot
    pieces = []
    for h in range(MOBA_HEADS):
        g = h // MOBA_GROUP
        pieces.append(o[h * dec_seq:(h + 1) * dec_seq, g * MOBA_HEAD_DIM:(g + 1) * MOBA_HEAD_DIM])
    o_ref[...] = jnp.concatenate(pieces, axis=1)


def _moba_sample(page_table, qm, new_k, new_v, slope_rows, cache_kt, cache_vt, layer, *, dec_batch, dec_seq):
    n_pages = page_table.shape[1]
    past_len = n_pages * PAGE_SIZE
    nb_past = past_len // MOBA_BLOCK
    rows = MOBA_HEADS * dec_seq
    in_specs = [
        pl.BlockSpec((MOBA_HEADS, dec_seq, MOBA_KV_WIDTH), lambda b, pt: (0, b, 0)),
        pl.BlockSpec((dec_seq, MOBA_KV_WIDTH), lambda b, pt: (b, 0)),
        pl.BlockSpec((dec_seq, MOBA_KV_WIDTH), lambda b, pt: (b, 0)),
        pl.BlockSpec((rows, LANES), lambda b, pt: (0, 0)),
        pl.BlockSpec(memory_space=pl.ANY),
        pl.BlockSpec(memory_space=pl.ANY),
    ]
    page_buf = pltpu.VMEM((2, n_pages, MOBA_KV_WIDTH, PAGE_SIZE), F32)
    stat = pltpu.VMEM((nb_past, rows, LANES), F32)
    return pl.pallas_call(
        functools.partial(_moba_sample_kernel, layer=layer, dec_seq=dec_seq, n_pages=n_pages,
                          past_len=past_len),
        grid_spec=pltpu.PrefetchScalarGridSpec(
            num_scalar_prefetch=1, grid=(dec_batch,), in_specs=in_specs,
            out_specs=pl.BlockSpec((dec_seq, BRANCH_WIDTH), lambda b, pt: (b, 0)),
            scratch_shapes=[page_buf, page_buf, pltpu.SemaphoreType.DMA((2, 2)),
                            pltpu.VMEM((nb_past, rows, MOBA_BLOCK), F32), stat, stat,
                            pltpu.VMEM((nb_past, MOBA_KV_WIDTH, rows), F32)]),
        out_shape=jax.ShapeDtypeStruct((dec_batch * dec_seq, BRANCH_WIDTH), F32),
        compiler_params=pltpu.CompilerParams(
            dimension_semantics=("arbitrary",), vmem_limit_bytes=VMEM_LIMIT),
        name="moba_sample",
    )(page_table, qm, new_k, new_v, slope_rows, cache_kt, cache_vt)


def _ffn_kernel(x_ref, oa_ref, ob_ref, g_attn_ref, wgate_ref, wbr_ref, wo_ref, g_ffn_ref, wgu_ref, wdn_ref,
                g_fin_ref, out_ref, *, final, n_chunks):
    x = x_ref[...]
    xn = _rms(x, g_attn_ref[...]).astype(BF16)
    merged = (jax.nn.sigmoid(_dot(xn, wgate_ref[:, :D_MODEL])) * _dot(oa_ref[...].astype(BF16), wbr_ref[0])
              + jax.nn.sigmoid(_dot(xn, wgate_ref[:, D_MODEL:])) * _dot(ob_ref[...].astype(BF16), wbr_ref[1]))
    h = x + _dot(merged.astype(BF16), wo_ref[...])
    hn = _rms(h, g_ffn_ref[...]).astype(BF16)
    cw = FFN_HIDDEN // n_chunks
    acc = h
    for c in range(n_chunks):
        a = _dot(hn, wgu_ref[:, c * cw:(c + 1) * cw])
        u = _dot(hn, wgu_ref[:, FFN_HIDDEN + c * cw:FFN_HIDDEN + (c + 1) * cw])
        act = (jax.nn.silu(a) * u).astype(BF16)
        acc = acc + _dot(act, wdn_ref[c * cw:(c + 1) * cw, :])
    out_ref[...] = _rms(acc, g_fin_ref[...]) if final else acc


def _ffn(x, o_a, o_b, w, g_final, *, final, tm, n_chunks=2):
    n = x.shape[0]
    row = lambda i: (i, 0)
    return pl.pallas_call(
        functools.partial(_ffn_kernel, final=final, n_chunks=n_chunks),
        grid=(n // tm,),
        in_specs=[
            pl.BlockSpec((tm, D_MODEL), row),
            pl.BlockSpec((tm, BRANCH_WIDTH), row),
            pl.BlockSpec((tm, BRANCH_WIDTH), row),
            _const_spec((1, D_MODEL)),
            _const_spec((D_MODEL, 2 * D_MODEL)),
            _const_spec((2, BRANCH_WIDTH, D_MODEL)),
            _const_spec((D_MODEL, D_MODEL)),
            _const_spec((1, D_MODEL)),
            _const_spec((D_MODEL, 2 * FFN_HIDDEN)),
            _const_spec((FFN_HIDDEN, D_MODEL)),
            _const_spec((1, D_MODEL)),
        ],
        out_specs=pl.BlockSpec((tm, D_MODEL), row),
        out_shape=jax.ShapeDtypeStruct((n, D_MODEL), F32),
        compiler_params=pltpu.CompilerParams(
            dimension_semantics=("arbitrary",), vmem_limit_bytes=VMEM_LIMIT),
        name="ffn_final" if final else "ffn",
    )(x, o_a, o_b, w["g_attn"], w["w_gate"], w["w_branch"], w["w_o"], w["g_ffn"], w["w_gu"], w["w_down"],
      g_final)


def _prep_layer_weights(w_in, g_q, g_kv, w_uq, w_uk, w_uv, w_branch, w_o, g_attn, g_ffn, w_gu, w_down):
    s = [0, Q_LORA, Q_LORA + KV_LORA, Q_LORA + KV_LORA + QK_ROPE]
    s.append(s[-1] + MOBA_HEADS * MOBA_HEAD_DIM)
    s.append(s[-1] + MOBA_KV_WIDTH)
    s.append(s[-1] + MOBA_KV_WIDTH)
    c_q, c_kv, k_r, m_q, m_k, m_v, gate = (w_in[:, s[0]:s[1]], w_in[:, s[1]:s[2]], w_in[:, s[2]:s[3]],
                                           w_in[:, s[3]:s[4]], w_in[:, s[4]:s[5]], w_in[:, s[5]:s[6]],
                                           w_in[:, s[6]:])
    m_q = m_q.reshape(D_MODEL, MOBA_KV_HEADS, MOBA_GROUP, MOBA_HEAD_DIM)
    zeros = jnp.zeros_like(m_q[:, 0])
    m_q_bd = jnp.concatenate([
        jnp.concatenate([m_q[:, 0], zeros], axis=-1),
        jnp.concatenate([zeros, m_q[:, 1]], axis=-1)], axis=1).reshape(D_MODEL, MOBA_HEADS * LANES)
    w_in_perm = jnp.concatenate([c_q, c_kv, m_q_bd, m_k, m_v, k_r], axis=1).astype(BF16)

    uq = w_uq.reshape(Q_LORA, MLA_HEADS, QK_NOPE + QK_ROPE)
    w_uqn = jnp.pad(uq[:, :, :QK_NOPE], ((0, 0), (0, 0), (0, LANES - QK_NOPE))).reshape(Q_LORA, MLA_HEADS * LANES)
    w_uqr = jnp.concatenate([uq[:, :, QK_NOPE:QK_NOPE + HALF_ROPE].reshape(Q_LORA, LANES),
                             uq[:, :, QK_NOPE + HALF_ROPE:].reshape(Q_LORA, LANES)], axis=1)
    uk = jnp.transpose(w_uk, (1, 2, 0))
    uk = jnp.pad(uk, ((0, 0), (0, LANES - QK_NOPE), (0, 0)))
    uv = jnp.transpose(w_uv, (1, 0, 2))
    eye = jnp.eye(MLA_HEADS, dtype=w_uv.dtype)
    uv_pad = (uv[:, :, None, :] * eye[:, None, :, None]).reshape(MLA_HEADS, KV_LORA, BRANCH_WIDTH)
    return {
        "w_in": w_in_perm, "w_gate": gate.astype(BF16), "g_q": g_q[None], "g_kv": g_kv[None], "g_attn": g_attn[None], "g_ffn": g_ffn[None],
        "w_uqn": w_uqn.astype(BF16), "w_uqr": w_uqr.astype(BF16), "w_uk": uk.astype(BF16),
        "w_uv": uv_pad.astype(BF16), "w_branch": w_branch.astype(BF16), "w_o": w_o.astype(BF16),
        "w_gu": w_gu.astype(BF16), "w_down": w_down.astype(BF16),
    }


def _rope_tables(pos):
    inv = ROPE_THETA ** (-jnp.arange(HALF_ROPE, dtype=F32) / HALF_ROPE)
    ang = pos.astype(F32)[:, None] * inv[None, :]
    reps = LANES // HALF_ROPE
    return jnp.tile(jnp.cos(ang), (1, reps)), jnp.tile(jnp.sin(ang), (1, reps))


def _slope_rows(tokens_per_head, width):
    slopes = 2.0 ** (-8.0 * jnp.arange(1, MOBA_HEADS + 1, dtype=F32) / MOBA_HEADS)
    return jnp.broadcast_to(jnp.repeat(slopes, tokens_per_head)[:, None], (MOBA_HEADS * tokens_per_head, width))


def kernel(x_prompt, x_sample, cache_ckv, cache_krope, cache_k, cache_v, page_table, w_in, g_q, g_kv, w_uq, w_uk, w_uv, w_branch, w_o, g_attn, g_ffn, w_gu, w_down, g_final):
    batch, seq, _ = x_prompt.shape
    dec_batch, dec_seq, _ = x_sample.shape
    depth, n_phys = cache_k.shape[:2]
    past_len = page_table.shape[1] * PAGE_SIZE
    n_p, n_s = batch * seq, dec_batch * dec_seq
    tm_p = min(512, seq)
    tm_s = min(512, n_s)

    cache_krope_t = jnp.transpose(cache_krope, (0, 1, 3, 2))
    cache_kt = jnp.transpose(cache_k, (0, 1, 3, 4, 2)).reshape(depth, n_phys, MOBA_KV_WIDTH, PAGE_SIZE)
    cache_vt = jnp.transpose(cache_v, (0, 1, 3, 4, 2)).reshape(depth, n_phys, MOBA_KV_WIDTH, PAGE_SIZE)

    cos_p, sin_p = _rope_tables(jnp.arange(seq, dtype=jnp.int32))
    pos_s = past_len + jnp.arange(dec_seq, dtype=jnp.int32)
    cos_s, sin_s = _rope_tables(jnp.tile(pos_s, tm_s // dec_seq))
    slope_p = _slope_rows(MOBA_BLOCK, 1)
    slope_s = _slope_rows(dec_seq, LANES)
    g_fin = g_final[None]

    hp = x_prompt.reshape(n_p, D_MODEL)
    hs = x_sample.reshape(n_s, D_MODEL)
    rows_p, rows_s = [], []
    for l in range(depth):
        w = _prep_layer_weights(w_in[l], g_q[l], g_kv[l], w_uq[l], w_uk[l], w_uv[l], w_branch[l], w_o[l],
                                g_attn[l], g_ffn[l], w_gu[l], w_down[l])
        final = l == depth - 1

        ckv, kr, mk, mv, qf, qm, kf, mkv, kmean = _proj(hp, cos_p, sin_p, w, prompt=True, tm=tm_p)
        o_a = _mla_prompt(qf, kf, w["w_uv"], batch=batch, seq=seq, tq=min(256, seq), tk=min(256, seq))
        o_b = _moba_prompt(qm, mkv, kmean, slope_p, batch=batch, seq=seq)
        hp = _ffn(hp, o_a, o_b, w, g_fin, final=final, tm=tm_p)
        rows_p.append((ckv, kr, mk, mv))

        ckv, kr, mk, mv, qf, qm = _proj(hs, cos_s, sin_s, w, prompt=False, tm=tm_s)
        o_lat = _mla_sample(page_table, qf, ckv, kr, cache_ckv, cache_krope_t, l,
                            dec_batch=dec_batch, dec_seq=dec_seq)
        o_a = _uv_sample(o_lat, w["w_uv"])
        o_b = _moba_sample(page_table, qm, mk, mv, slope_s, cache_kt, cache_vt, l,
                           dec_batch=dec_batch, dec_seq=dec_seq)
        hs = _ffn(hs, o_a, o_b, w, g_fin, final=final, tm=tm_s)
        rows_s.append((ckv, kr, mk, mv))

    def stack(rows, i, shape):
        return jnp.stack([r[i] for r in rows]).reshape((depth,) + shape)

    kv_shape = (MOBA_KV_HEADS, MOBA_HEAD_DIM)
    return (hp.reshape(batch, seq, D_MODEL),
            hs.reshape(dec_batch, dec_seq, D_MODEL),
            stack(rows_p, 0, (batch, seq, KV_LORA)),
            stack(rows_p, 1, (batch, seq, QK_ROPE)),
            stack(rows_p, 2, (batch, seq) + kv_shape),
            stack(rows_p, 3, (batch, seq) + kv_shape),
            stack(rows_s, 0, (dec_batch, dec_seq, KV_LORA)),
            stack(rows_s, 1, (dec_batch, dec_seq, QK_ROPE)),
            stack(rows_s, 2, (dec_batch, dec_seq) + kv_shape),
            stack(rows_s, 3, (dec_batch, dec_seq) + kv_shape))
```

```python
import functools
import math
import operator

import jax
import jax.numpy as jnp
from jax import lax
from jax.experimental import pallas as pl
from jax.experimental.pallas import tpu as pltpu

F32 = jnp.float32
BF16 = jnp.bfloat16

D_MODEL = 1024
PAGE_SIZE = 128
MLA_HEADS = 8
Q_LORA = 384
KV_LORA = 256
QK_NOPE = 64
QK_ROPE = 32
ROPE_THETA = 10000.0
MOBA_HEADS = 8
MOBA_KV_HEADS = 2
MOBA_GROUP = MOBA_HEADS // MOBA_KV_HEADS
MOBA_HEAD_DIM = 64
MOBA_BLOCK = 256
MOBA_TOPK = 3
BRANCH_WIDTH = 512
FFN_HIDDEN = 2816
RMS_EPS = 1e-6

LANES = 128
QK_PAD = KV_LORA + LANES
MOBA_KV_WIDTH = MOBA_KV_HEADS * MOBA_HEAD_DIM
HALF_ROPE = QK_ROPE // 2
PAGES_PER_BLOCK = MOBA_BLOCK // PAGE_SIZE

_OFF_CQ = 0
_OFF_KR = _OFF_CQ + Q_LORA
_OFF_CKV = _OFF_CQ + 4 * LANES
_OFF_MQ = _OFF_CKV + KV_LORA
_OFF_MK = _OFF_MQ + MOBA_HEADS * MOBA_HEAD_DIM
_OFF_MV = _OFF_MK + MOBA_KV_WIDTH
_D_IN_PERM = _OFF_MV + MOBA_KV_WIDTH

VMEM_LIMIT = 56 * 1024 * 1024
MLA_CHUNK_PAGES = 16
DMA_ISSUE_UNROLL = 4
NEG_INF = float("-inf")
LOG2_E = 1.4426950408889634


def _rms(x, g):
    return x * lax.rsqrt(jnp.mean(x * x, axis=-1, keepdims=True) + RMS_EPS) * g


def _dot(a, b):
    return jnp.dot(a, b, preferred_element_type=F32)


def _dot_nt(a, b):
    return lax.dot_general(a, b, (((1,), (1,)), ((), ())), preferred_element_type=F32)


def _const_spec(shape):
    return pl.BlockSpec(shape, lambda *_: (0,) * len(shape), pipeline_mode=pl.Buffered(1))


def _lane_slabs(x):
    return [x[:, j * LANES:(j + 1) * LANES] for j in range(x.shape[1] // LANES)]


def _slab_max(m, s):
    return functools.reduce(jnp.maximum, _lane_slabs(s), m)


def _row_max(m_slab):
    return jnp.broadcast_to(jnp.max(m_slab, axis=1, keepdims=True), m_slab.shape)


def _proj_kernel(x_ref, cos_ref, sin_ref, g_attn_ref, w_in_ref, g_q_ref, g_kv_ref,
                 w_uqn_ref, w_uqr_ref, w_uk_ref, *out_refs, prompt):
    if prompt:
        (ckv_ref, kr_ref, mk_ref, mv_ref, qf_ref, qm_ref,
         kf_ref, mkv_ref, kmean_ref) = out_refs
    else:
        ckv_ref, kr_ref, mk_ref, mv_ref, qf_ref, qm_ref = out_refs
    tm = x_ref.shape[0]
    hn = _rms(x_ref[...], g_attn_ref[...]).astype(BF16)

    def seg(lo, hi):
        return _dot(hn, w_in_ref[:, lo:hi])

    cos = cos_ref[...]
    sin = sin_ref[...]

    ckv = _rms(seg(_OFF_CKV, _OFF_MQ), g_kv_ref[...])
    ckv_ref[...] = ckv
    cq_kr = seg(_OFF_CQ, _OFF_CKV)
    kr = cq_kr[:, _OFF_KR:_OFF_KR + QK_ROPE]
    kr_swapped = jnp.concatenate([kr[:, HALF_ROPE:], kr[:, :HALF_ROPE]], axis=1)
    lane32 = lax.broadcasted_iota(jnp.int32, (tm, QK_ROPE), 1)
    sin_signed = jnp.where(lane32 < HALF_ROPE, -sin[:, :QK_ROPE], sin[:, :QK_ROPE])
    kr = kr * cos[:, :QK_ROPE] + kr_swapped * sin_signed
    kr_ref[...] = kr

    mk = seg(_OFF_MK, _OFF_MV)
    mv = seg(_OFF_MV, _D_IN_PERM)
    mk_ref[...] = mk
    mv_ref[...] = mv

    mq = seg(_OFF_MQ, _OFF_MK)
    lower = lax.broadcasted_iota(jnp.int32, (tm, LANES), 1) < MOBA_HEAD_DIM
    for k in range(MOBA_GROUP):
        slab = mq[:, k * LANES:(k + 1) * LANES]
        qm_ref[k] = jnp.where(lower, slab, 0.0).astype(qm_ref.dtype)
        qm_ref[MOBA_GROUP + k] = jnp.where(lower, 0.0, slab).astype(qm_ref.dtype)

    cq = _rms(cq_kr[:, :Q_LORA], g_q_ref[...]).astype(BF16)
    q_nope = _dot(cq, w_uqn_ref[...]).astype(BF16)
    q_rot = _dot(cq, w_uqr_ref[...])
    r1, r2 = q_rot[:, :LANES], q_rot[:, LANES:]
    o1 = r1 * cos - r2 * sin
    o2 = r2 * cos + r1 * sin
    lane = lax.broadcasted_iota(jnp.int32, (tm, LANES), 1)
    for h in range(MLA_HEADS):
        q_lat = _dot(q_nope[:, h * LANES:(h + 1) * LANES], w_uk_ref[h])
        shift_a = (LANES - HALF_ROPE * h) % LANES
        shift_b = (HALF_ROPE - HALF_ROPE * h) % LANES
        a = o1 if shift_a == 0 else pltpu.roll(o1, shift_a, axis=1)
        b = o2 if shift_b == 0 else pltpu.roll(o2, shift_b, axis=1)
        rope = jnp.where(lane < HALF_ROPE, a, jnp.where(lane < QK_ROPE, b, 0.0))
        qf_ref[h, :, :KV_LORA] = q_lat.astype(qf_ref.dtype)
        qf_ref[h, :, KV_LORA:] = rope.astype(qf_ref.dtype)

    if prompt:
        kf_ref[:, :KV_LORA] = ckv.astype(BF16)
        kr_pad = jnp.concatenate([kr, jnp.zeros((tm, LANES - QK_ROPE), F32)], axis=1)
        kf_ref[:, KV_LORA:] = kr_pad.astype(BF16)
        mkv_ref[:, :MOBA_KV_WIDTH] = mk.astype(BF16)
        mkv_ref[:, MOBA_KV_WIDTH:] = mv.astype(BF16)
        for j in range(tm // MOBA_BLOCK):
            kmean_ref[j] = jnp.mean(mk[j * MOBA_BLOCK:(j + 1) * MOBA_BLOCK], axis=0, keepdims=True)


def _proj(x, cos_tab, sin_tab, w, *, prompt, tm):
    n = x.shape[0]
    n_tab = cos_tab.shape[0] // tm
    grid = (n // tm,)
    row = lambda i: (i, 0)
    q_dtype = BF16 if prompt else F32
    out_shape = [
        jax.ShapeDtypeStruct((n, KV_LORA), F32),
        jax.ShapeDtypeStruct((n, QK_ROPE), F32),
        jax.ShapeDtypeStruct((n, MOBA_KV_WIDTH), F32),
        jax.ShapeDtypeStruct((n, MOBA_KV_WIDTH), F32),
        jax.ShapeDtypeStruct((MLA_HEADS, n, QK_PAD), q_dtype),
        jax.ShapeDtypeStruct((MOBA_HEADS, n, MOBA_KV_WIDTH), F32),
    ]
    out_specs = [
        pl.BlockSpec((tm, KV_LORA), row),
        pl.BlockSpec((tm, QK_ROPE), row),
        pl.BlockSpec((tm, MOBA_KV_WIDTH), row),
        pl.BlockSpec((tm, MOBA_KV_WIDTH), row),
        pl.BlockSpec((MLA_HEADS, tm, QK_PAD), lambda i: (0, i, 0)),
        pl.BlockSpec((MOBA_HEADS, tm, MOBA_KV_WIDTH), lambda i: (0, i, 0)),
    ]
    if prompt:
        nblk = tm // MOBA_BLOCK
        out_shape += [
            jax.ShapeDtypeStruct((n, QK_PAD), BF16),
            jax.ShapeDtypeStruct((n, 2 * MOBA_KV_WIDTH), BF16),
            jax.ShapeDtypeStruct((n // MOBA_BLOCK, 1, MOBA_KV_WIDTH), F32),
        ]
        out_specs += [
            pl.BlockSpec((tm, QK_PAD), row),
            pl.BlockSpec((tm, 2 * MOBA_KV_WIDTH), row),
            pl.BlockSpec((nblk, 1, MOBA_KV_WIDTH), lambda i: (i, 0, 0)),
        ]
    tab = lambda i: (i % n_tab, 0)
    in_specs = [
        pl.BlockSpec((tm, D_MODEL), row),
        pl.BlockSpec((tm, LANES), tab),
        pl.BlockSpec((tm, LANES), tab),
        _const_spec((1, D_MODEL)),
        _const_spec((D_MODEL, _D_IN_PERM)),
        _const_spec((1, Q_LORA)),
        _const_spec((1, KV_LORA)),
        _const_spec((Q_LORA, MLA_HEADS * LANES)),
        _const_spec((Q_LORA, 2 * LANES)),
        _const_spec((MLA_HEADS, LANES, KV_LORA)),
    ]
    return pl.pallas_call(
        functools.partial(_proj_kernel, prompt=prompt),
        grid=grid, in_specs=in_specs, out_specs=out_specs, out_shape=out_shape,
        compiler_params=pltpu.CompilerParams(
            dimension_semantics=("arbitrary",), vmem_limit_bytes=VMEM_LIMIT),
        name="proj_prompt" if prompt else "proj_sample",
    )(x, cos_tab, sin_tab, w["g_attn"], w["w_in"], w["g_q"], w["g_kv"],
      w["w_uqn"], w["w_uqr"], w["w_uk"])


def _exp_scaled(x, scale):
    return jnp.exp(x) if scale == 1.0 else jnp.exp2(x * (scale * LOG2_E))


def _exp_accumulate(s, m, scale, v, l_ref, acc_ref):
    ps = [_exp_scaled(sl - m, scale) for sl in _lane_slabs(s)]
    l_ref[...] += functools.reduce(operator.add, ps)
    acc_ref[...] += _dot(jnp.concatenate(ps, axis=1).astype(BF16), v)


def _mla_prompt_kernel(q_ref, k_ref, wuv_ref, o_ref, s_ref, m_ref, l_ref, acc_ref, *, tq, tk, scale):
    qi = pl.program_id(1)
    rows = MLA_HEADS * tq
    q = q_ref[...].reshape(rows, QK_PAD)
    last = (qi * tq + tq - 1) // tk

    def k_chunk(kc):
        return k_ref[pl.ds(pl.multiple_of(kc * tk, tk), tk), :]

    def scores(kc, masked):
        s = _dot_nt(q, k_chunk(kc))
        if masked:
            q_pos = qi * tq + lax.broadcasted_iota(jnp.int32, (MLA_HEADS, tq, tk), 1).reshape(rows, tk)
            k_pos = kc * tk + lax.broadcasted_iota(jnp.int32, (rows, tk), 1)
            s = jnp.where(k_pos <= q_pos, s, NEG_INF)
        s_ref[kc] = s
        m_ref[...] = _slab_max(m_ref[...], s)

    m_ref[...] = jnp.full(m_ref.shape, NEG_INF, F32)

    def score_body(kc, carry):
        scores(kc, False)
        return carry

    lax.fori_loop(0, last, score_body, 0)
    scores(last, True)

    m_ref[...] = _row_max(m_ref[...])
    l_ref[...] = jnp.zeros(l_ref.shape, F32)
    acc_ref[...] = jnp.zeros(acc_ref.shape, F32)

    def value_body(kc, carry):
        _exp_accumulate(s_ref[kc], m_ref[...], scale, k_chunk(kc)[:, :KV_LORA], l_ref, acc_ref)
        return carry

    lax.fori_loop(0, last + 1, value_body, 0)

    o_lat = (acc_ref[...] / jnp.sum(l_ref[...], axis=1, keepdims=True)).astype(BF16)
    o = _dot(o_lat[:tq], wuv_ref[0])
    for h in range(1, MLA_HEADS):
        o = o + _dot(o_lat[h * tq:(h + 1) * tq], wuv_ref[h])
    o_ref[...] = o.astype(o_ref.dtype)


def _mla_prompt(qf, kf, w_uv, *, batch, seq, tq, tk):
    nq = seq // tq
    rows = MLA_HEADS * tq
    qf4 = qf.reshape(MLA_HEADS, batch, seq, QK_PAD)
    kf3 = kf.reshape(batch, seq, QK_PAD)
    scale = (QK_NOPE + QK_ROPE) ** -0.5
    return pl.pallas_call(
        functools.partial(_mla_prompt_kernel, tq=tq, tk=tk, scale=scale),
        grid=(batch, nq),
        in_specs=[
            pl.BlockSpec((MLA_HEADS, None, tq, QK_PAD), lambda b, i: (0, b, i, 0)),
            pl.BlockSpec((None, seq, QK_PAD), lambda b, i: (b, 0, 0)),
            _const_spec((MLA_HEADS, KV_LORA, BRANCH_WIDTH)),
        ],
        out_specs=pl.BlockSpec((tq, BRANCH_WIDTH), lambda b, i: (b * nq + i, 0)),
        out_shape=jax.ShapeDtypeStruct((batch * seq, BRANCH_WIDTH), BF16),
        scratch_shapes=[pltpu.VMEM((seq // tk, rows, tk), F32), pltpu.VMEM((rows, LANES), F32),
                        pltpu.VMEM((rows, LANES), F32), pltpu.VMEM((rows, KV_LORA), F32)],
        compiler_params=pltpu.CompilerParams(
            dimension_semantics=("arbitrary", "arbitrary"), vmem_limit_bytes=VMEM_LIMIT),
        name="mla_prompt",
    )(qf4, kf3, w_uv)


def _hi_lo(x):
    hi = x.astype(BF16)
    return hi, (x - hi.astype(F32)).astype(BF16)


def _gate_scores(qf, kmean, kmean_is_transposed):
    mm = _dot if kmean_is_transposed else _dot_nt
    q_hi, q_lo = _hi_lo(qf)
    k_hi, k_lo = _hi_lo(kmean)
    return mm(q_hi, k_hi) + (mm(q_hi, k_lo) + mm(q_lo, k_hi))


def _gate_scores_t(qf, kmean):
    q_hi, q_lo = _hi_lo(qf)
    k_hi, k_lo = _hi_lo(kmean)
    return _dot_nt(jnp.concatenate([k_hi, k_lo, k_hi], axis=1), jnp.concatenate([q_hi, q_hi, q_lo], axis=1))


def _prescaled_query(qf, scale):
    if math.frexp(scale)[0] == 0.5:
        return (qf * scale).astype(BF16), 1.0
    return qf.astype(BF16), scale


def _times(x, factor):
    return x if factor == 1.0 else x * factor


def _topk_blocks(gate, valid, n_sel, axis):
    nb = gate.shape[axis]
    blk = lax.broadcasted_iota(jnp.int32, gate.shape, axis).astype(F32)
    g = jnp.where(valid, gate, NEG_INF)
    picked = jnp.zeros(gate.shape, jnp.bool_)
    for _ in range(n_sel):
        cur = jnp.where(picked, NEG_INF, g)
        best = jnp.max(cur, axis=axis, keepdims=True)
        cand = jnp.logical_and(cur == best, jnp.logical_not(picked))
        idx = jnp.min(jnp.where(cand, blk, float(nb)), axis=axis, keepdims=True)
        picked = jnp.logical_or(picked, blk == idx)
    return jnp.logical_and(picked, valid)


def _moba_prompt_kernel(q_ref, kv_ref, kmean_ref, slope_ref, slope_t_ref, o_ref,
                        choice_ref, shift_ref, s_ref, m_ref, l_ref, acc_ref, *, nb):
    tq = MOBA_BLOCK
    qi = pl.program_id(1)
    rows = MOBA_HEADS * tq
    own_slot = nb - 1
    qf = q_ref[...].reshape(rows, MOBA_KV_WIDTH)
    q, scale = _prescaled_query(qf, MOBA_HEAD_DIM ** -0.5)
    slope = slope_ref[...]
    t_q = lax.broadcasted_iota(jnp.int32, (MOBA_HEADS, tq, 1), 1).reshape(rows, 1)
    off_k = lax.broadcasted_iota(jnp.int32, (1, tq), 1)

    blk_t = lax.broadcasted_iota(jnp.int32, (nb, rows), 0)
    n_sel = min(MOBA_TOPK, nb)

    @pl.when(qi <= n_sel)
    def _():
        choice_ref[...] = jnp.where(blk_t < qi, 0.0, NEG_INF)

    @pl.when(qi > n_sel)
    def _():
        gate_t = _gate_scores_t(qf, kmean_ref[...].reshape(nb, MOBA_KV_WIDTH))
        choice_ref[...] = jnp.where(_topk_blocks(gate_t, blk_t < qi, n_sel, 0), 0.0, NEG_INF)

    shift_t = choice_ref[...] - slope_t_ref[...] * ((qi - blk_t) * tq).astype(F32)
    shift_t = jnp.concatenate([shift_t, jnp.zeros((LANES - nb, rows), F32)], axis=0)
    shift_ref[...] = shift_t.T[:, :nb]
    dist_own = t_q - off_k
    alibi_own = -slope * dist_own.astype(F32)

    def block_rows(j):
        return pl.ds(j * tq if isinstance(j, int) else pl.multiple_of(j * tq, tq), tq)

    def keys(j):
        return kv_ref[block_rows(j), :MOBA_KV_WIDTH]

    def values(j):
        return kv_ref[block_rows(j), MOBA_KV_WIDTH:]

    def qk(j):
        return _times(_dot_nt(q, keys(j)), scale)

    s = jnp.where(dist_own >= 0, qk(qi) + alibi_own, NEG_INF)
    s_ref[own_slot] = s
    m_ref[...] = _slab_max(jnp.full(m_ref.shape, NEG_INF, F32), s)
    for j in range(nb - 1):
        @pl.when(j < qi)
        def _():
            sj = qk(j) + alibi_own + shift_ref[:, j:j + 1]
            s_ref[j] = sj
            m_ref[...] = _slab_max(m_ref[...], sj)

    m_ref[...] = _row_max(m_ref[...])
    l_ref[...] = jnp.zeros(l_ref.shape, F32)
    acc_ref[...] = jnp.zeros(acc_ref.shape, F32)
    _exp_accumulate(s_ref[own_slot], m_ref[...], 1.0, values(qi), l_ref, acc_ref)
    for j in range(nb - 1):
        @pl.when(j < qi)
        def _():
            _exp_accumulate(s_ref[j], m_ref[...], 1.0, values(j), l_ref, acc_ref)

    o = acc_ref[...] / jnp.sum(l_ref[...], axis=1, keepdims=True)
    pieces = []
    for h in range(MOBA_HEADS):
        g = h // MOBA_GROUP
        pieces.append(o[h * tq:(h + 1) * tq, g * MOBA_HEAD_DIM:(g + 1) * MOBA_HEAD_DIM])
    o_ref[...] = jnp.concatenate(pieces, axis=1).astype(o_ref.dtype)


def _moba_prompt(qm, mkv, kmean, slope_rows, *, batch, seq):
    tq = MOBA_BLOCK
    nb = seq // tq
    rows = MOBA_HEADS * tq
    qm4 = qm.reshape(MOBA_HEADS, batch, seq, MOBA_KV_WIDTH)
    mkv3 = mkv.reshape(batch, seq, 2 * MOBA_KV_WIDTH)
    kmean4 = kmean.reshape(batch, nb, 1, MOBA_KV_WIDTH)
    return pl.pallas_call(
        functools.partial(_moba_prompt_kernel, nb=nb),
        grid=(batch, nb),
        in_specs=[
            pl.BlockSpec((MOBA_HEADS, None, tq, MOBA_KV_WIDTH), lambda b, i: (0, b, i, 0)),
            pl.BlockSpec((None, seq, 2 * MOBA_KV_WIDTH), lambda b, i: (b, 0, 0)),
            pl.BlockSpec((None, nb, 1, MOBA_KV_WIDTH), lambda b, i: (b, 0, 0, 0)),
            _const_spec((rows, 1)),
            _const_spec((1, rows)),
        ],
        out_specs=pl.BlockSpec((tq, BRANCH_WIDTH), lambda b, i: (b * nb + i, 0)),
        out_shape=jax.ShapeDtypeStruct((batch * seq, BRANCH_WIDTH), BF16),
        scratch_shapes=[pltpu.VMEM((nb, rows), F32), pltpu.VMEM((rows, nb), F32),
                        pltpu.VMEM((nb, rows, tq), F32),
                        pltpu.VMEM((rows, LANES), F32), pltpu.VMEM((rows, LANES), F32),
                        pltpu.VMEM((rows, MOBA_KV_WIDTH), F32)],
        compiler_params=pltpu.CompilerParams(
            dimension_semantics=("arbitrary", "arbitrary"), vmem_limit_bytes=VMEM_LIMIT),
        name="moba_prompt",
    )(qm4, mkv3, kmean4, slope_rows, slope_rows.reshape(1, rows))


def _page_copies(layer, hbm_refs, bufs, sems, page, slot, p):
    return [pltpu.make_async_copy(hbm.at[layer, page], buf.at[slot, p], sems.at[i, slot])
            for i, (hbm, buf) in enumerate(zip(hbm_refs, bufs))]


def _page_pipeline(pt_ref, layer, n_pages, hbm_refs, bufs, sems):
    b = pl.program_id(0)
    slot = b % 2
    copies = functools.partial(_page_copies, layer, hbm_refs, bufs, sems)

    def start_fetch(request, into):
        def body(p, carry):
            for cp in copies(pt_ref[request, p], into, p):
                cp.start()
            return carry
        lax.fori_loop(0, n_pages, body, 0, unroll=DMA_ISSUE_UNROLL)

    @pl.when(b == 0)
    def _():
        start_fetch(0, 0)

    @pl.when(b + 1 < pl.num_programs(0))
    def _():
        start_fetch(b + 1, 1 - slot)

    for p in range(n_pages):
        for cp in copies(0, slot, p):
            cp.wait()
    return slot


def _mla_sample_kernel(pt_ref, q_ref, nckv_ref, nkr_ref, ckv_hbm, krt_hbm, o_ref,
                       ckv_buf, krt_buf, sems, kb_ref, s_ref, m_ref, l_ref, acc_ref,
                       *, layer, dec_seq, n_pages, scale):
    slot = _page_pipeline(pt_ref, layer, n_pages, (ckv_hbm, krt_hbm), (ckv_buf, krt_buf), sems)
    rows = MLA_HEADS * dec_seq
    ch = min(MLA_CHUNK_PAGES, n_pages)
    ck = ch * PAGE_SIZE
    q = q_ref[...].reshape(rows, QK_PAD)
    q_lat = q[:, :KV_LORA].astype(BF16)
    q_rope = q[:, KV_LORA:KV_LORA + QK_ROPE].astype(BF16)

    kn = nckv_ref[...].astype(BF16)
    s_new = _dot_nt(q_lat, kn) + _dot_nt(q_rope, nkr_ref[...].astype(BF16))
    t_q = lax.broadcasted_iota(jnp.int32, (MLA_HEADS, dec_seq, dec_seq), 1).reshape(rows, dec_seq)
    t_k = lax.broadcasted_iota(jnp.int32, (rows, dec_seq), 1)
    s_new = jnp.where(t_k <= t_q, s_new, NEG_INF)
    m_new = jnp.max(s_new, axis=1, keepdims=True)
    p_new = _exp_scaled(s_new - m_new, scale)
    l_new = jnp.sum(p_new, axis=1, keepdims=True)
    acc_new = _dot(p_new.astype(BF16), kn)

    n_chunks = n_pages // ch
    for c in range(n_chunks):
        kb = ckv_buf[slot, c * ch:(c + 1) * ch].reshape(ck, KV_LORA).astype(BF16)
        krt = jnp.concatenate([krt_buf[slot, c * ch + i] for i in range(ch)], axis=1).astype(BF16)
        kb_ref[c] = kb
        s_ref[c] = _dot_nt(q_lat, kb) + _dot(q_rope, krt)

    def softmax(c):
        s = s_ref[c]
        mc = _row_max(_slab_max(jnp.full((rows, LANES), NEG_INF, F32), s))
        ps = [_exp_scaled(sl - mc, scale) for sl in _lane_slabs(s)]
        m_ref[c] = mc
        l_ref[c] = functools.reduce(operator.add, ps)
        return jnp.concatenate(ps, axis=1).astype(BF16)

    nxt = softmax(0)
    for c in range(n_chunks):
        p = nxt
        if c + 1 < n_chunks:
            nxt = softmax(c + 1)
        acc_ref[c] = _dot(p, kb_ref[c])

    m = functools.reduce(jnp.maximum, [m_ref[c] for c in range(n_chunks)],
                         jnp.broadcast_to(m_new, (rows, LANES)))
    l_lanes = jnp.zeros((rows, LANES), F32)
    acc = [jnp.zeros((rows, LANES), F32)] * (KV_LORA // LANES)
    for c in range(n_chunks):
        w = _exp_scaled(m_ref[c] - m, scale)
        l_lanes = l_lanes + w * l_ref[c]
        acc = [a + w * sl for a, sl in zip(acc, _lane_slabs(acc_ref[c]))]
    w_new = _exp_scaled(m_new - m[:, :1], scale)
    l = jnp.sum(l_lanes, axis=1, keepdims=True) + w_new * l_new
    o = (jnp.concatenate(acc, axis=1) + w_new * acc_new) / l
    o_ref[...] = o.reshape(MLA_HEADS, dec_seq, KV_LORA)


def _mla_sample(page_table, qf, new_ckv, new_kr, cache_ckv, cache_krope_t, layer, *, dec_batch, dec_seq):
    n_pages = page_table.shape[1]
    chunk_pages = min(MLA_CHUNK_PAGES, n_pages)
    n_chunks = n_pages // chunk_pages
    chunk_keys = chunk_pages * PAGE_SIZE
    rows = MLA_HEADS * dec_seq
    scale = (QK_NOPE + QK_ROPE) ** -0.5
    in_specs = [
        pl.BlockSpec((MLA_HEADS, dec_seq, QK_PAD), lambda b, pt: (0, b, 0)),
        pl.BlockSpec((dec_seq, KV_LORA), lambda b, pt: (b, 0)),
        pl.BlockSpec((dec_seq, QK_ROPE), lambda b, pt: (b, 0)),
        pl.BlockSpec(memory_space=pl.ANY),
        pl.BlockSpec(memory_space=pl.ANY),
    ]
    return pl.pallas_call(
        functools.partial(_mla_sample_kernel, layer=layer, dec_seq=dec_seq, n_pages=n_pages, scale=scale),
        grid_spec=pltpu.PrefetchScalarGridSpec(
            num_scalar_prefetch=1, grid=(dec_batch,), in_specs=in_specs,
            out_specs=pl.BlockSpec((MLA_HEADS, dec_seq, KV_LORA), lambda b, pt: (0, b, 0)),
            scratch_shapes=[pltpu.VMEM((2, n_pages, PAGE_SIZE, KV_LORA), F32),
                            pltpu.VMEM((2, n_pages, QK_ROPE, PAGE_SIZE), F32),
                            pltpu.SemaphoreType.DMA((2, 2)),
                            pltpu.VMEM((n_chunks, chunk_keys, KV_LORA), BF16),
                            pltpu.VMEM((n_chunks, rows, chunk_keys), F32),
                            pltpu.VMEM((n_chunks, rows, LANES), F32),
                            pltpu.VMEM((n_chunks, rows, LANES), F32),
                            pltpu.VMEM((n_chunks, rows, KV_LORA), F32)]),
        out_shape=jax.ShapeDtypeStruct((MLA_HEADS, dec_batch * dec_seq, KV_LORA), F32),
        compiler_params=pltpu.CompilerParams(
            dimension_semantics=("arbitrary",), vmem_limit_bytes=VMEM_LIMIT),
        name="mla_sample",
    )(page_table, qf, new_ckv, new_kr, cache_ckv, cache_krope_t)


def _uv_kernel(o_lat_ref, wuv_ref, o_ref):
    o = _dot(o_lat_ref[0].astype(BF16), wuv_ref[0])
    for h in range(1, MLA_HEADS):
        o = o + _dot(o_lat_ref[h].astype(BF16), wuv_ref[h])
    o_ref[...] = o


def _uv_sample(o_lat, w_uv):
    n = o_lat.shape[1]
    return pl.pallas_call(
        _uv_kernel,
        out_shape=jax.ShapeDtypeStruct((n, BRANCH_WIDTH), F32),
        compiler_params=pltpu.CompilerParams(vmem_limit_bytes=VMEM_LIMIT),
        name="uv_sample",
    )(o_lat, w_uv)


def _moba_sample_kernel(pt_ref, q_ref, nk_ref, nv_ref, slope_ref, kt_hbm, vt_hbm, o_ref,
                        kt_buf, vt_buf, sems, s_ref, m_ref, l_ref, acc_ref,
                        *, layer, dec_seq, n_pages, past_len):
    slot = _page_pipeline(pt_ref, layer, n_pages, (kt_hbm, vt_hbm), (kt_buf, vt_buf), sems)
    rows = MOBA_HEADS * dec_seq
    nb_past = n_pages // PAGES_PER_BLOCK
    wide = (rows, LANES)
    qf = q_ref[...].reshape(rows, MOBA_KV_WIDTH)
    q, scale = _prescaled_query(qf, MOBA_HEAD_DIM ** -0.5)
    slope = slope_ref[...]
    t_q = past_len + lax.broadcasted_iota(jnp.int32, (MOBA_HEADS, dec_seq, LANES), 1).reshape(wide)
    off_k = lax.broadcasted_iota(jnp.int32, (1, MOBA_BLOCK), 1).astype(F32)
    alibi_off = slope[:, :1] * off_k
    lane = lax.broadcasted_iota(jnp.int32, wide, 1)
    lane_km = lax.broadcasted_iota(jnp.int32, (MOBA_KV_WIDTH, LANES), 1)

    def block_t(buf, j):
        return jnp.concatenate([buf[slot, PAGES_PER_BLOCK * j + i] for i in range(PAGES_PER_BLOCK)], axis=1)

    km = jnp.zeros((MOBA_KV_WIDTH, LANES), F32)
    for j in range(nb_past):
        kt = block_t(kt_buf, j)
        km = jnp.where(lane_km == j, jnp.sum(kt, axis=1, keepdims=True) / MOBA_BLOCK, km)
        s_ref[j] = _dot(q, kt.astype(BF16))

    gate = _gate_scores(qf, km, True)
    sel = _topk_blocks(gate, lane < nb_past, min(MOBA_TOPK, nb_past), 1)

    def softmax(j):
        shift = -slope * (t_q - j * MOBA_BLOCK).astype(F32)
        raw = s_ref[j]
        s = [sl + shift for sl in _lane_slabs(_times(raw, scale) + alibi_off)]
        mj = _row_max(functools.reduce(jnp.maximum, s))
        ps = [jnp.exp(sl - mj) for sl in s]
        m_ref[j] = mj
        l_ref[j] = jnp.broadcast_to(jnp.sum(functools.reduce(operator.add, ps), axis=1, keepdims=True), wide)
        return jnp.concatenate(ps, axis=1).astype(BF16)

    nxt = softmax(0)
    for j in range(nb_past):
        p = nxt
        if j + 1 < nb_past:
            nxt = softmax(j + 1)
        acc_ref[j] = _dot_nt(block_t(vt_buf, j).astype(BF16), p)

    kn = nk_ref[...].astype(BF16)
    t_k = lax.broadcasted_iota(jnp.int32, (1, dec_seq), 1)
    dist = (t_q[:, :1] - past_len) - t_k
    s = _times(_dot_nt(q, kn), scale) - slope[:, :1] * dist.astype(F32)
    s = jnp.where(dist >= 0, s, NEG_INF)
    m_own = jnp.max(s, axis=1, keepdims=True)
    p = jnp.exp(s - m_own)
    l_own = jnp.sum(p, axis=1, keepdims=True)
    acc_own = _dot(p.astype(BF16), nv_ref[...].astype(BF16))

    m_blk = jnp.zeros(wide, F32)
    l_blk = jnp.zeros(wide, F32)
    for j in range(nb_past):
        m_blk = jnp.where(lane == j, m_ref[j], m_blk)
        l_blk = jnp.where(lane == j, l_ref[j], l_blk)
    m_tot = jnp.maximum(m_own, jnp.max(jnp.where(sel, m_blk, NEG_INF), axis=1, keepdims=True))
    w_blk = jnp.where(sel, jnp.exp(m_blk - m_tot), 0.0)
    w_own = jnp.exp(m_own - m_tot)
    l_tot = jnp.sum(w_blk * l_blk, axis=1, keepdims=True) + w_own * l_own
    w_t = jnp.concatenate([w_blk, jnp.zeros((LANES - rows, LANES), F32)], axis=0).T[:, :rows]
    acc_t = jnp.zeros((MOBA_KV_WIDTH, rows), F32)
    for j in range(nb_past):
        acc_t = acc_t + w_t[j:j + 1, :] * acc_ref[j]
    acc_t = jnp.concatenate([acc_t, jnp.zeros((MOBA_KV_WIDTH, LANES - rows), F32)], axis=1)
    o = (acc_t.T[:rows, :] + w_own * acc_own) / l_tot
    pieces = []
    for h in range(MOBA_HEADS):
        g = h // MOBA_GROUP
        pieces.append(o[h * dec_seq:(h + 1) * dec_seq, g * MOBA_HEAD_DIM:(g + 1) * MOBA_HEAD_DIM])
    o_ref[...] = jnp.concatenate(pieces, axis=1)


def _moba_sample(page_table, qm, new_k, new_v, slope_rows, cache_kt, cache_vt, layer, *, dec_batch, dec_seq):
    n_pages = page_table.shape[1]
    past_len = n_pages * PAGE_SIZE
    nb_past = past_len // MOBA_BLOCK
    rows = MOBA_HEADS * dec_seq
    in_specs = [
        pl.BlockSpec((MOBA_HEADS, dec_seq, MOBA_KV_WIDTH), lambda b, pt: (0, b, 0)),
        pl.BlockSpec((dec_seq, MOBA_KV_WIDTH), lambda b, pt: (b, 0)),
        pl.BlockSpec((dec_seq, MOBA_KV_WIDTH), lambda b, pt: (b, 0)),
        pl.BlockSpec((rows, LANES), lambda b, pt: (0, 0)),
        pl.BlockSpec(memory_space=pl.ANY),
        pl.BlockSpec(memory_space=pl.ANY),
    ]
    page_buf = pltpu.VMEM((2, n_pages, MOBA_KV_WIDTH, PAGE_SIZE), F32)
    stat = pltpu.VMEM((nb_past, rows, LANES), F32)
    return pl.pallas_call(
        functools.partial(_moba_sample_kernel, layer=layer, dec_seq=dec_seq, n_pages=n_pages,
                          past_len=past_len),
        grid_spec=pltpu.PrefetchScalarGridSpec(
            num_scalar_prefetch=1, grid=(dec_batch,), in_specs=in_specs,
            out_specs=pl.BlockSpec((dec_seq, BRANCH_WIDTH), lambda b, pt: (b, 0)),
            scratch_shapes=[page_buf, page_buf, pltpu.SemaphoreType.DMA((2, 2)),
                            pltpu.VMEM((nb_past, rows, MOBA_BLOCK), F32), stat, stat,
                            pltpu.VMEM((nb_past, MOBA_KV_WIDTH, rows), F32)]),
        out_shape=jax.ShapeDtypeStruct((dec_batch * dec_seq, BRANCH_WIDTH), F32),
        compiler_params=pltpu.CompilerParams(
            dimension_semantics=("arbitrary",), vmem_limit_bytes=VMEM_LIMIT),
        name="moba_sample",
    )(page_table, qm, new_k, new_v, slope_rows, cache_kt, cache_vt)


def _ffn_kernel(x_ref, oa_ref, ob_ref, g_attn_ref, wgate_ref, wbr_ref, wo_ref, g_ffn_ref, wgu_ref, wdn_ref,
                g_fin_ref, out_ref, *, final, n_chunks):
    x = x_ref[...]
    xn = _rms(x, g_attn_ref[...]).astype(BF16)
    merged = (jax.nn.sigmoid(_dot(xn, wgate_ref[:, :D_MODEL])) * _dot(oa_ref[...].astype(BF16), wbr_ref[0])
              + jax.nn.sigmoid(_dot(xn, wgate_ref[:, D_MODEL:])) * _dot(ob_ref[...].astype(BF16), wbr_ref[1]))
    h = x + _dot(merged.astype(BF16), wo_ref[...])
    hn = _rms(h, g_ffn_ref[...]).astype(BF16)
    cw = FFN_HIDDEN // n_chunks
    acc = h
    for c in range(n_chunks):
        a = _dot(hn, wgu_ref[:, c * cw:(c + 1) * cw])
        u = _dot(hn, wgu_ref[:, FFN_HIDDEN + c * cw:FFN_HIDDEN + (c + 1) * cw])
        act = (jax.nn.silu(a) * u).astype(BF16)
        acc = acc + _dot(act, wdn_ref[c * cw:(c + 1) * cw, :])
    out_ref[...] = _rms(acc, g_fin_ref[...]) if final else acc


def _ffn(x, o_a, o_b, w, stacked, layer, g_final, *, final, tm, n_chunks=2):
    n = x.shape[0]
    row = lambda i: (i, 0)

    def layer_spec(shape):
        return pl.BlockSpec((None,) + shape, lambda *_: (layer,) + (0,) * len(shape),
                            pipeline_mode=pl.Buffered(1))

    return pl.pallas_call(
        functools.partial(_ffn_kernel, final=final, n_chunks=n_chunks),
        grid=(n // tm,),
        in_specs=[
            pl.BlockSpec((tm, D_MODEL), row),
            pl.BlockSpec((tm, BRANCH_WIDTH), row),
            pl.BlockSpec((tm, BRANCH_WIDTH), row),
            _const_spec((1, D_MODEL)),
            _const_spec((D_MODEL, 2 * D_MODEL)),
            layer_spec((2, BRANCH_WIDTH, D_MODEL)),
            layer_spec((D_MODEL, D_MODEL)),
            _const_spec((1, D_MODEL)),
            layer_spec((D_MODEL, 2 * FFN_HIDDEN)),
            layer_spec((FFN_HIDDEN, D_MODEL)),
            _const_spec((1, D_MODEL)),
        ],
        out_specs=pl.BlockSpec((tm, D_MODEL), row),
        out_shape=jax.ShapeDtypeStruct((n, D_MODEL), F32),
        compiler_params=pltpu.CompilerParams(
            dimension_semantics=("arbitrary",), vmem_limit_bytes=VMEM_LIMIT),
        name="ffn_final" if final else "ffn",
    )(x, o_a, o_b, w["g_attn"], w["w_gate"], stacked["w_branch"], stacked["w_o"], w["g_ffn"],
      stacked["w_gu"], stacked["w_down"], g_final)


def _prep_layer_weights(w_in, g_q, g_kv, w_uq, w_uk, w_uv, g_attn, g_ffn):
    s = [0, Q_LORA, Q_LORA + KV_LORA, Q_LORA + KV_LORA + QK_ROPE]
    s.append(s[-1] + MOBA_HEADS * MOBA_HEAD_DIM)
    s.append(s[-1] + MOBA_KV_WIDTH)
    s.append(s[-1] + MOBA_KV_WIDTH)
    c_q, c_kv, k_r, m_q, m_k, m_v, gate = (w_in[:, s[0]:s[1]], w_in[:, s[1]:s[2]], w_in[:, s[2]:s[3]],
                                           w_in[:, s[3]:s[4]], w_in[:, s[4]:s[5]], w_in[:, s[5]:s[6]],
                                           w_in[:, s[6]:])
    m_q = m_q.reshape(D_MODEL, MOBA_KV_HEADS, MOBA_GROUP, MOBA_HEAD_DIM)
    m_q_paired = jnp.transpose(m_q, (0, 2, 1, 3)).reshape(D_MODEL, MOBA_HEADS * MOBA_HEAD_DIM)
    unused = jnp.zeros((D_MODEL, _OFF_CKV - _OFF_KR - QK_ROPE), w_in.dtype)
    w_in_perm = jnp.concatenate([c_q, k_r, unused, c_kv, m_q_paired, m_k, m_v], axis=1).astype(BF16)

    uq = w_uq.reshape(Q_LORA, MLA_HEADS, QK_NOPE + QK_ROPE)
    w_uqn = jnp.pad(uq[:, :, :QK_NOPE], ((0, 0), (0, 0), (0, LANES - QK_NOPE))).reshape(Q_LORA, MLA_HEADS * LANES)
    w_uqr = jnp.concatenate([uq[:, :, QK_NOPE:QK_NOPE + HALF_ROPE].reshape(Q_LORA, LANES),
                             uq[:, :, QK_NOPE + HALF_ROPE:].reshape(Q_LORA, LANES)], axis=1)
    uk = jnp.transpose(w_uk, (1, 2, 0))
    uk = jnp.pad(uk, ((0, 0), (0, LANES - QK_NOPE), (0, 0)))
    uv = jnp.transpose(w_uv, (1, 0, 2))
    eye = jnp.eye(MLA_HEADS, dtype=w_uv.dtype)
    uv_pad = (uv[:, :, None, :] * eye[:, None, :, None]).reshape(MLA_HEADS, KV_LORA, BRANCH_WIDTH)
    return {
        "w_in": w_in_perm, "w_gate": gate.astype(BF16), "g_q": g_q[None], "g_kv": g_kv[None],
        "g_attn": g_attn[None], "g_ffn": g_ffn[None],
        "w_uqn": w_uqn.astype(BF16), "w_uqr": w_uqr.astype(BF16), "w_uk": uk.astype(BF16),
        "w_uv": uv_pad.astype(BF16),
    }


def _rope_tables(pos):
    inv = ROPE_THETA ** (-jnp.arange(HALF_ROPE, dtype=F32) / HALF_ROPE)
    ang = pos.astype(F32)[:, None] * inv[None, :]
    reps = LANES // HALF_ROPE
    return jnp.tile(jnp.cos(ang), (1, reps)), jnp.tile(jnp.sin(ang), (1, reps))


def _slope_rows(tokens_per_head, width):
    slopes = 2.0 ** (-8.0 * jnp.arange(1, MOBA_HEADS + 1, dtype=F32) / MOBA_HEADS)
    return jnp.broadcast_to(jnp.repeat(slopes, tokens_per_head)[:, None], (MOBA_HEADS * tokens_per_head, width))


def kernel(x_prompt, x_sample, cache_ckv, cache_krope, cache_k, cache_v, page_table, w_in, g_q, g_kv, w_uq, w_uk, w_uv, w_branch, w_o, g_attn, g_ffn, w_gu, w_down, g_final):
    batch, seq, _ = x_prompt.shape
    dec_batch, dec_seq, _ = x_sample.shape
    depth, n_phys = cache_k.shape[:2]
    past_len = page_table.shape[1] * PAGE_SIZE
    n_p, n_s = batch * seq, dec_batch * dec_seq
    tm_p = min(512, seq)
    tm_s = min(512, n_s)

    cache_krope_t = jnp.transpose(cache_krope, (0, 1, 3, 2))
    cache_kt = jnp.transpose(cache_k, (0, 1, 3, 4, 2)).reshape(depth, n_phys, MOBA_KV_WIDTH, PAGE_SIZE)
    cache_vt = jnp.transpose(cache_v, (0, 1, 3, 4, 2)).reshape(depth, n_phys, MOBA_KV_WIDTH, PAGE_SIZE)

    cos_p, sin_p = _rope_tables(jnp.arange(seq, dtype=jnp.int32))
    pos_s = past_len + jnp.arange(dec_seq, dtype=jnp.int32)
    cos_s, sin_s = _rope_tables(jnp.tile(pos_s, tm_s // dec_seq))
    slope_p = _slope_rows(MOBA_BLOCK, 1)
    slope_s = _slope_rows(dec_seq, LANES)
    g_fin = g_final[None]
    stacked = {"w_branch": w_branch.astype(BF16), "w_o": w_o.astype(BF16),
               "w_gu": w_gu.astype(BF16), "w_down": w_down.astype(BF16)}

    hp = x_prompt.reshape(n_p, D_MODEL)
    hs = x_sample.reshape(n_s, D_MODEL)
    rows_p, rows_s = [], []
    for l in range(depth):
        w = _prep_layer_weights(w_in[l], g_q[l], g_kv[l], w_uq[l], w_uk[l], w_uv[l], g_attn[l], g_ffn[l])
        final = l == depth - 1

        ckv, kr, mk, mv, qf, qm, kf, mkv, kmean = _proj(hp, cos_p, sin_p, w, prompt=True, tm=tm_p)
        o_a = _mla_prompt(qf, kf, w["w_uv"], batch=batch, seq=seq, tq=min(256, seq), tk=min(256, seq))
        o_b = _moba_prompt(qm, mkv, kmean, slope_p, batch=batch, seq=seq)
        hp = _ffn(hp, o_a, o_b, w, stacked, l, g_fin, final=final, tm=tm_p)
        rows_p.append((ckv, kr, mk, mv))

        ckv, kr, mk, mv, qf, qm = _proj(hs, cos_s, sin_s, w, prompt=False, tm=tm_s)
        o_lat = _mla_sample(page_table, qf, ckv, kr, cache_ckv, cache_krope_t, l,
                            dec_batch=dec_batch, dec_seq=dec_seq)
        o_a = _uv_sample(o_lat, w["w_uv"])
        o_b = _moba_sample(page_table, qm, mk, mv, slope_s, cache_kt, cache_vt, l,
                           dec_batch=dec_batch, dec_seq=dec_seq)
        hs = _ffn(hs, o_a, o_b, w, stacked, l, g_fin, final=final, tm=tm_s)
        rows_s.append((ckv, kr, mk, mv))

    def stack(rows, i, shape):
        return jnp.stack([r[i] for r in rows]).reshape((depth,) + shape)

    kv_shape = (MOBA_KV_HEADS, MOBA_HEAD_DIM)
    return (hp.reshape(batch, seq, D_MODEL),
            hs.reshape(dec_batch, dec_seq, D_MODEL),
            stack(rows_p, 0, (batch, seq, KV_LORA)),
            stack(rows_p, 1, (batch, seq, QK_ROPE)),
            stack(rows_p, 2, (batch, seq) + kv_shape),
            stack(rows_p, 3, (batch, seq) + kv_shape),
            stack(rows_s, 0, (dec_batch, dec_seq, KV_LORA)),
            stack(rows_s, 1, (dec_batch, dec_seq, QK_ROPE)),
            stack(rows_s, 2, (dec_batch, dec_seq) + kv_shape),
            stack(rows_s, 3, (dec_batch, dec_seq) + kv_shape))
```

```python
import functools
import math
import operator

import jax
import jax.numpy as jnp
from jax import lax
from jax.experimental import pallas as pl
from jax.experimental.pallas import tpu as pltpu

F32 = jnp.float32
BF16 = jnp.bfloat16

D_MODEL = 1024
PAGE_SIZE = 128
MLA_HEADS = 8
Q_LORA = 384
KV_LORA = 256
QK_NOPE = 64
QK_ROPE = 32
ROPE_THETA = 10000.0
MOBA_HEADS = 8
MOBA_KV_HEADS = 2
MOBA_GROUP = MOBA_HEADS // MOBA_KV_HEADS
MOBA_HEAD_DIM = 64
MOBA_BLOCK = 256
MOBA_TOPK = 3
BRANCH_WIDTH = 512
FFN_HIDDEN = 2816
RMS_EPS = 1e-6

LANES = 128
QK_PAD = KV_LORA + LANES
MOBA_KV_WIDTH = MOBA_KV_HEADS * MOBA_HEAD_DIM
HALF_ROPE = QK_ROPE // 2
PAGES_PER_BLOCK = MOBA_BLOCK // PAGE_SIZE

_OFF_CQ = 0
_OFF_KR = _OFF_CQ + Q_LORA
_OFF_CKV = -(-(_OFF_KR + QK_ROPE) // LANES) * LANES
_OFF_MQ = _OFF_CKV + KV_LORA
_OFF_MK = _OFF_MQ + MOBA_HEADS * MOBA_HEAD_DIM
_OFF_MV = _OFF_MK + MOBA_KV_WIDTH
_D_IN_PERM = _OFF_MV + MOBA_KV_WIDTH

VMEM_LIMIT = 56 * 1024 * 1024
MLA_CHUNK_PAGES = 16
DMA_ISSUE_UNROLL = 4
NEG_INF = float("-inf")
LOG2_E = 1.4426950408889634


def _rms(x, g):
    return x * lax.rsqrt(jnp.mean(x * x, axis=-1, keepdims=True) + RMS_EPS) * g


def _dot(a, b):
    return jnp.dot(a, b, preferred_element_type=F32)


def _dot_nt(a, b):
    return lax.dot_general(a, b, (((1,), (1,)), ((), ())), preferred_element_type=F32)


def _const_spec(shape):
    return pl.BlockSpec(shape, lambda *_: (0,) * len(shape), pipeline_mode=pl.Buffered(1))


def _lane_slabs(x):
    return [x[:, j * LANES:(j + 1) * LANES] for j in range(x.shape[1] // LANES)]


def _slab_max(m, s):
    return functools.reduce(jnp.maximum, _lane_slabs(s), m)


def _row_max(m_slab):
    return jnp.broadcast_to(jnp.max(m_slab, axis=1, keepdims=True), m_slab.shape)


def _proj_kernel(x_ref, cos_ref, sin_ref, g_attn_ref, w_in_ref, g_q_ref, g_kv_ref,
                 w_uqn_ref, w_uqr_ref, w_uk_ref, *out_refs, prompt):
    if prompt:
        (ckv_ref, kr_ref, mk_ref, mv_ref, qf_ref, qm_ref,
         kf_ref, mkv_ref, kmean_ref) = out_refs
    else:
        ckv_ref, kr_ref, mk_ref, mv_ref, qf_ref, qm_ref = out_refs
    tm = x_ref.shape[0]
    hn = _rms(x_ref[...], g_attn_ref[...]).astype(BF16)

    def seg(lo, hi):
        return _dot(hn, w_in_ref[:, lo:hi])

    cos = cos_ref[...]
    sin = sin_ref[...]

    ckv = _rms(seg(_OFF_CKV, _OFF_MQ), g_kv_ref[...])
    ckv_ref[...] = ckv
    cq_kr = seg(_OFF_CQ, _OFF_CKV)
    kr = cq_kr[:, _OFF_KR:_OFF_KR + QK_ROPE]
    kr_swapped = jnp.concatenate([kr[:, HALF_ROPE:], kr[:, :HALF_ROPE]], axis=1)
    lane32 = lax.broadcasted_iota(jnp.int32, (tm, QK_ROPE), 1)
    sin_signed = jnp.where(lane32 < HALF_ROPE, -sin[:, :QK_ROPE], sin[:, :QK_ROPE])
    kr = kr * cos[:, :QK_ROPE] + kr_swapped * sin_signed
    kr_ref[...] = kr

    mk = seg(_OFF_MK, _OFF_MV)
    mv = seg(_OFF_MV, _D_IN_PERM)
    mk_ref[...] = mk
    mv_ref[...] = mv

    mq = seg(_OFF_MQ, _OFF_MK)
    lower = lax.broadcasted_iota(jnp.int32, (tm, LANES), 1) < MOBA_HEAD_DIM
    for k in range(MOBA_GROUP):
        slab = mq[:, k * LANES:(k + 1) * LANES]
        qm_ref[k] = jnp.where(lower, slab, 0.0).astype(qm_ref.dtype)
        qm_ref[MOBA_GROUP + k] = jnp.where(lower, 0.0, slab).astype(qm_ref.dtype)

    cq = _rms(cq_kr[:, :Q_LORA], g_q_ref[...]).astype(BF16)
    q_nope = _dot(cq, w_uqn_ref[...]).astype(BF16)
    q_rot = _dot(cq, w_uqr_ref[...])
    r1, r2 = q_rot[:, :LANES], q_rot[:, LANES:]
    o1 = r1 * cos - r2 * sin
    o2 = r2 * cos + r1 * sin
    lane = lax.broadcasted_iota(jnp.int32, (tm, LANES), 1)
    for h in range(MLA_HEADS):
        q_lat = _dot(q_nope[:, h * LANES:(h + 1) * LANES], w_uk_ref[h])
        shift_a = (LANES - HALF_ROPE * h) % LANES
        shift_b = (HALF_ROPE - HALF_ROPE * h) % LANES
        a = o1 if shift_a == 0 else pltpu.roll(o1, shift_a, axis=1)
        b = o2 if shift_b == 0 else pltpu.roll(o2, shift_b, axis=1)
        rope = jnp.where(lane < HALF_ROPE, a, jnp.where(lane < QK_ROPE, b, 0.0))
        qf_ref[h, :, :KV_LORA] = q_lat.astype(qf_ref.dtype)
        qf_ref[h, :, KV_LORA:] = rope.astype(qf_ref.dtype)

    if prompt:
        kf_ref[:, :KV_LORA] = ckv.astype(BF16)
        kr_pad = jnp.concatenate([kr, jnp.zeros((tm, LANES - QK_ROPE), F32)], axis=1)
        kf_ref[:, KV_LORA:] = kr_pad.astype(BF16)
        mkv_ref[:, :MOBA_KV_WIDTH] = mk.astype(BF16)
        mkv_ref[:, MOBA_KV_WIDTH:] = mv.astype(BF16)
        for j in range(tm // MOBA_BLOCK):
            kmean_ref[j] = jnp.mean(mk[j * MOBA_BLOCK:(j + 1) * MOBA_BLOCK], axis=0, keepdims=True)


def _proj(x, cos_tab, sin_tab, w, *, prompt, tm):
    n = x.shape[0]
    n_tab = cos_tab.shape[0] // tm
    grid = (n // tm,)
    row = lambda i: (i, 0)
    q_dtype = BF16 if prompt else F32
    out_shape = [
        jax.ShapeDtypeStruct((n, KV_LORA), F32),
        jax.ShapeDtypeStruct((n, QK_ROPE), F32),
        jax.ShapeDtypeStruct((n, MOBA_KV_WIDTH), F32),
        jax.ShapeDtypeStruct((n, MOBA_KV_WIDTH), F32),
        jax.ShapeDtypeStruct((MLA_HEADS, n, QK_PAD), q_dtype),
        jax.ShapeDtypeStruct((MOBA_HEADS, n, MOBA_KV_WIDTH), F32),
    ]
    out_specs = [
        pl.BlockSpec((tm, KV_LORA), row),
        pl.BlockSpec((tm, QK_ROPE), row),
        pl.BlockSpec((tm, MOBA_KV_WIDTH), row),
        pl.BlockSpec((tm, MOBA_KV_WIDTH), row),
        pl.BlockSpec((MLA_HEADS, tm, QK_PAD), lambda i: (0, i, 0)),
        pl.BlockSpec((MOBA_HEADS, tm, MOBA_KV_WIDTH), lambda i: (0, i, 0)),
    ]
    if prompt:
        nblk = tm // MOBA_BLOCK
        out_shape += [
            jax.ShapeDtypeStruct((n, QK_PAD), BF16),
            jax.ShapeDtypeStruct((n, 2 * MOBA_KV_WIDTH), BF16),
            jax.ShapeDtypeStruct((n // MOBA_BLOCK, 1, MOBA_KV_WIDTH), F32),
        ]
        out_specs += [
            pl.BlockSpec((tm, QK_PAD), row),
            pl.BlockSpec((tm, 2 * MOBA_KV_WIDTH), row),
            pl.BlockSpec((nblk, 1, MOBA_KV_WIDTH), lambda i: (i, 0, 0)),
        ]
    tab = lambda i: (i % n_tab, 0)
    in_specs = [
        pl.BlockSpec((tm, D_MODEL), row),
        pl.BlockSpec((tm, LANES), tab),
        pl.BlockSpec((tm, LANES), tab),
        _const_spec((1, D_MODEL)),
        _const_spec((D_MODEL, _D_IN_PERM)),
        _const_spec((1, Q_LORA)),
        _const_spec((1, KV_LORA)),
        _const_spec((Q_LORA, MLA_HEADS * LANES)),
        _const_spec((Q_LORA, 2 * LANES)),
        _const_spec((MLA_HEADS, LANES, KV_LORA)),
    ]
    return pl.pallas_call(
        functools.partial(_proj_kernel, prompt=prompt),
        grid=grid, in_specs=in_specs, out_specs=out_specs, out_shape=out_shape,
        compiler_params=pltpu.CompilerParams(
            dimension_semantics=("arbitrary",), vmem_limit_bytes=VMEM_LIMIT),
        name="proj_prompt" if prompt else "proj_sample",
    )(x, cos_tab, sin_tab, w["g_attn"], w["w_in"], w["g_q"], w["g_kv"],
      w["w_uqn"], w["w_uqr"], w["w_uk"])


def _exp_scaled(x, scale):
    return jnp.exp(x) if scale == 1.0 else jnp.exp2(x * (scale * LOG2_E))


def _exp_accumulate(s, m, scale, v, l_ref, acc_ref):
    ps = [_exp_scaled(sl - m, scale) for sl in _lane_slabs(s)]
    l_ref[...] += functools.reduce(operator.add, ps)
    acc_ref[...] += _dot(jnp.concatenate(ps, axis=1).astype(BF16), v)


def _mla_prompt_kernel(q_ref, k_ref, wuv_ref, o_ref, s_ref, m_ref, l_ref, acc_ref, *, tq, tk, scale):
    qi = pl.program_id(1)
    rows = MLA_HEADS * tq
    q = q_ref[...].reshape(rows, QK_PAD)
    last = (qi * tq + tq - 1) // tk

    def k_chunk(kc):
        return k_ref[pl.ds(pl.multiple_of(kc * tk, tk), tk), :]

    def scores(kc, masked):
        s = _dot_nt(q, k_chunk(kc))
        if masked:
            q_pos = qi * tq + lax.broadcasted_iota(jnp.int32, (MLA_HEADS, tq, tk), 1).reshape(rows, tk)
            k_pos = kc * tk + lax.broadcasted_iota(jnp.int32, (rows, tk), 1)
            s = jnp.where(k_pos <= q_pos, s, NEG_INF)
        s_ref[kc] = s
        m_ref[...] = _slab_max(m_ref[...], s)

    m_ref[...] = jnp.full(m_ref.shape, NEG_INF, F32)

    def score_body(kc, carry):
        scores(kc, False)
        return carry

    lax.fori_loop(0, last, score_body, 0)
    scores(last, True)

    m_ref[...] = _row_max(m_ref[...])
    l_ref[...] = jnp.zeros(l_ref.shape, F32)
    acc_ref[...] = jnp.zeros(acc_ref.shape, F32)

    def value_pair_body(kp, carry):
        kc = 2 * kp
        s_pair = jnp.concatenate([s_ref[kc], s_ref[kc + 1]], axis=1)
        v_pair = k_ref[pl.ds(pl.multiple_of(kc * tk, tk), 2 * tk), :KV_LORA]
        _exp_accumulate(s_pair, m_ref[...], scale, v_pair, l_ref, acc_ref)
        return carry

    n_chunks = last + 1
    n_pairs = 0
    if k_ref.shape[0] >= 2 * tk:
        n_pairs = n_chunks // 2
        lax.fori_loop(0, n_pairs, value_pair_body, 0)

    @pl.when(n_chunks - 2 * n_pairs == 1)
    def _():
        _exp_accumulate(s_ref[last], m_ref[...], scale, k_chunk(last)[:, :KV_LORA], l_ref, acc_ref)

    o_lat = (acc_ref[...] / jnp.sum(l_ref[...], axis=1, keepdims=True)).astype(BF16)
    o = _dot(o_lat[:tq], wuv_ref[0])
    for h in range(1, MLA_HEADS):
        o = o + _dot(o_lat[h * tq:(h + 1) * tq], wuv_ref[h])
    o_ref[...] = o.astype(o_ref.dtype)


def _mla_prompt(qf, kf, w_uv, *, batch, seq, tq, tk):
    nq = seq // tq
    rows = MLA_HEADS * tq
    qf4 = qf.reshape(MLA_HEADS, batch, seq, QK_PAD)
    kf3 = kf.reshape(batch, seq, QK_PAD)
    scale = (QK_NOPE + QK_ROPE) ** -0.5
    return pl.pallas_call(
        functools.partial(_mla_prompt_kernel, tq=tq, tk=tk, scale=scale),
        grid=(batch, nq),
        in_specs=[
            pl.BlockSpec((MLA_HEADS, None, tq, QK_PAD), lambda b, i: (0, b, i, 0)),
            pl.BlockSpec((None, seq, QK_PAD), lambda b, i: (b, 0, 0)),
            _const_spec((MLA_HEADS, KV_LORA, BRANCH_WIDTH)),
        ],
        out_specs=pl.BlockSpec((tq, BRANCH_WIDTH), lambda b, i: (b * nq + i, 0)),
        out_shape=jax.ShapeDtypeStruct((batch * seq, BRANCH_WIDTH), BF16),
        scratch_shapes=[pltpu.VMEM((seq // tk, rows, tk), F32), pltpu.VMEM((rows, LANES), F32),
                        pltpu.VMEM((rows, LANES), F32), pltpu.VMEM((rows, KV_LORA), F32)],
        compiler_params=pltpu.CompilerParams(
            dimension_semantics=("arbitrary", "arbitrary"), vmem_limit_bytes=VMEM_LIMIT),
        name="mla_prompt",
    )(qf4, kf3, w_uv)


def _hi_lo(x):
    hi = x.astype(BF16)
    return hi, (x - hi.astype(F32)).astype(BF16)


def _gate_scores(qf, kmean, kmean_is_transposed):
    mm = _dot if kmean_is_transposed else _dot_nt
    q_hi, q_lo = _hi_lo(qf)
    k_hi, k_lo = _hi_lo(kmean)
    return mm(q_hi, k_hi) + (mm(q_hi, k_lo) + mm(q_lo, k_hi))


def _gate_scores_t(qf, kmean):
    q_hi, q_lo = _hi_lo(qf)
    k_hi, k_lo = _hi_lo(kmean)
    return _dot_nt(jnp.concatenate([k_hi, k_lo, k_hi], axis=1), jnp.concatenate([q_hi, q_hi, q_lo], axis=1))


def _prescaled_query(qf, scale):
    if math.frexp(scale)[0] == 0.5:
        return (qf * scale).astype(BF16), 1.0
    return qf.astype(BF16), scale


def _times(x, factor):
    return x if factor == 1.0 else x * factor


def _topk_blocks(gate, valid, n_sel, axis):
    nb = gate.shape[axis]
    blk = lax.broadcasted_iota(jnp.int32, gate.shape, axis).astype(F32)
    g = jnp.where(valid, gate, NEG_INF)
    picked = jnp.zeros(gate.shape, jnp.bool_)
    for _ in range(n_sel):
        cur = jnp.where(picked, NEG_INF, g)
        best = jnp.max(cur, axis=axis, keepdims=True)
        cand = jnp.logical_and(cur == best, jnp.logical_not(picked))
        idx = jnp.min(jnp.where(cand, blk, float(nb)), axis=axis, keepdims=True)
        picked = jnp.logical_or(picked, blk == idx)
    return jnp.logical_and(picked, valid)


def _moba_prompt_kernel(q_ref, kv_ref, kmean_ref, slope_ref, slope_t_ref, o_ref,
                        choice_ref, shift_ref, s_ref, m_ref, l_ref, acc_ref, *, nb):
    tq = MOBA_BLOCK
    qi = pl.program_id(1)
    rows = MOBA_HEADS * tq
    own_slot = nb - 1
    qf = q_ref[...].reshape(rows, MOBA_KV_WIDTH)
    q, scale = _prescaled_query(qf, MOBA_HEAD_DIM ** -0.5)
    slope = slope_ref[...]
    t_q = lax.broadcasted_iota(jnp.int32, (MOBA_HEADS, tq, 1), 1).reshape(rows, 1)
    off_k = lax.broadcasted_iota(jnp.int32, (1, tq), 1)

    blk_t = lax.broadcasted_iota(jnp.int32, (nb, rows), 0)
    n_sel = min(MOBA_TOPK, nb)

    @pl.when(qi <= n_sel)
    def _():
        choice_ref[...] = jnp.where(blk_t < qi, 0.0, NEG_INF)

    @pl.when(qi > n_sel)
    def _():
        gate_t = _gate_scores_t(qf, kmean_ref[...].reshape(nb, MOBA_KV_WIDTH))
        choice_ref[...] = jnp.where(_topk_blocks(gate_t, blk_t < qi, n_sel, 0), 0.0, NEG_INF)

    shift_t = choice_ref[...] - slope_t_ref[...] * ((qi - blk_t) * tq).astype(F32)
    shift_t = jnp.concatenate([shift_t, jnp.zeros((LANES - nb, rows), F32)], axis=0)
    shift_ref[...] = shift_t.T[:, :nb]
    dist_own = t_q - off_k
    alibi_own = -slope * dist_own.astype(F32)

    def block_rows(j):
        return pl.ds(j * tq if isinstance(j, int) else pl.multiple_of(j * tq, tq), tq)

    def keys(j):
        return kv_ref[block_rows(j), :MOBA_KV_WIDTH]

    def values(j):
        return kv_ref[block_rows(j), MOBA_KV_WIDTH:]

    def qk(j):
        return _times(_dot_nt(q, keys(j)), scale)

    s = jnp.where(dist_own >= 0, qk(qi) + alibi_own, NEG_INF)
    s_ref[own_slot] = s
    m_ref[...] = _slab_max(jnp.full(m_ref.shape, NEG_INF, F32), s)
    for j in range(nb - 1):
        @pl.when(j < qi)
        def _():
            sj = qk(j) + alibi_own + shift_ref[:, j:j + 1]
            s_ref[j] = sj
            m_ref[...] = _slab_max(m_ref[...], sj)

    m_ref[...] = _row_max(m_ref[...])
    l_ref[...] = jnp.zeros(l_ref.shape, F32)
    acc_ref[...] = jnp.zeros(acc_ref.shape, F32)
    _exp_accumulate(s_ref[own_slot], m_ref[...], 1.0, values(qi), l_ref, acc_ref)
    for j in range(nb - 1):
        @pl.when(j < qi)
        def _():
            _exp_accumulate(s_ref[j], m_ref[...], 1.0, values(j), l_ref, acc_ref)

    o = acc_ref[...] / jnp.sum(l_ref[...], axis=1, keepdims=True)
    pieces = []
    for h in range(MOBA_HEADS):
        g = h // MOBA_GROUP
        pieces.append(o[h * tq:(h + 1) * tq, g * MOBA_HEAD_DIM:(g + 1) * MOBA_HEAD_DIM])
    o_ref[...] = jnp.concatenate(pieces, axis=1).astype(o_ref.dtype)


def _moba_prompt(qm, mkv, kmean, slope_rows, *, batch, seq):
    tq = MOBA_BLOCK
    nb = seq // tq
    rows = MOBA_HEADS * tq
    qm4 = qm.reshape(MOBA_HEADS, batch, seq, MOBA_KV_WIDTH)
    mkv3 = mkv.reshape(batch, seq, 2 * MOBA_KV_WIDTH)
    kmean4 = kmean.reshape(batch, nb, 1, MOBA_KV_WIDTH)
    return pl.pallas_call(
        functools.partial(_moba_prompt_kernel, nb=nb),
        grid=(batch, nb),
        in_specs=[
            pl.BlockSpec((MOBA_HEADS, None, tq, MOBA_KV_WIDTH), lambda b, i: (0, b, i, 0)),
            pl.BlockSpec((None, seq, 2 * MOBA_KV_WIDTH), lambda b, i: (b, 0, 0)),
            pl.BlockSpec((None, nb, 1, MOBA_KV_WIDTH), lambda b, i: (b, 0, 0, 0)),
            _const_spec((rows, 1)),
            _const_spec((1, rows)),
        ],
        out_specs=pl.BlockSpec((tq, BRANCH_WIDTH), lambda b, i: (b * nb + i, 0)),
        out_shape=jax.ShapeDtypeStruct((batch * seq, BRANCH_WIDTH), BF16),
        scratch_shapes=[pltpu.VMEM((nb, rows), F32), pltpu.VMEM((rows, nb), F32),
                        pltpu.VMEM((nb, rows, tq), F32),
                        pltpu.VMEM((rows, LANES), F32), pltpu.VMEM((rows, LANES), F32),
                        pltpu.VMEM((rows, MOBA_KV_WIDTH), F32)],
        compiler_params=pltpu.CompilerParams(
            dimension_semantics=("arbitrary", "arbitrary"), vmem_limit_bytes=VMEM_LIMIT),
        name="moba_prompt",
    )(qm4, mkv3, kmean4, slope_rows, slope_rows.reshape(1, rows))


def _page_copies(layer, hbm_refs, bufs, sems, page, slot, p):
    return [pltpu.make_async_copy(hbm.at[layer, page], buf.at[slot, p], sems.at[i, slot])
            for i, (hbm, buf) in enumerate(zip(hbm_refs, bufs))]


def _page_pipeline(pt_ref, layer, n_pages, hbm_refs, bufs, sems):
    b = pl.program_id(0)
    slot = b % 2
    copies = functools.partial(_page_copies, layer, hbm_refs, bufs, sems)

    def start_fetch(request, into):
        def body(p, carry):
            for cp in copies(pt_ref[request, p], into, p):
                cp.start()
            return carry
        lax.fori_loop(0, n_pages, body, 0, unroll=DMA_ISSUE_UNROLL)

    @pl.when(b == 0)
    def _():
        start_fetch(0, 0)

    @pl.when(b + 1 < pl.num_programs(0))
    def _():
        start_fetch(b + 1, 1 - slot)

    for p in range(n_pages):
        for cp in copies(0, slot, p):
            cp.wait()
    return slot


def _mla_sample_kernel(pt_ref, q_ref, nckv_ref, nkr_ref, ckv_hbm, krt_hbm, o_ref,
                       ckv_buf, krt_buf, sems, kb_ref, s_ref, m_ref, l_ref, acc_ref,
                       *, layer, dec_seq, n_pages, scale):
    slot = _page_pipeline(pt_ref, layer, n_pages, (ckv_hbm, krt_hbm), (ckv_buf, krt_buf), sems)
    rows = MLA_HEADS * dec_seq
    ch = min(MLA_CHUNK_PAGES, n_pages)
    ck = ch * PAGE_SIZE
    q = q_ref[...].reshape(rows, QK_PAD)
    q_lat = q[:, :KV_LORA].astype(BF16)
    q_rope = q[:, KV_LORA:KV_LORA + QK_ROPE].astype(BF16)

    kn = nckv_ref[...].astype(BF16)
    s_new = _dot_nt(q_lat, kn) + _dot_nt(q_rope, nkr_ref[...].astype(BF16))
    t_q = lax.broadcasted_iota(jnp.int32, (MLA_HEADS, dec_seq, dec_seq), 1).reshape(rows, dec_seq)
    t_k = lax.broadcasted_iota(jnp.int32, (rows, dec_seq), 1)
    s_new = jnp.where(t_k <= t_q, s_new, NEG_INF)
    m_new = jnp.max(s_new, axis=1, keepdims=True)
    p_new = _exp_scaled(s_new - m_new, scale)
    l_new = jnp.sum(p_new, axis=1, keepdims=True)
    acc_new = _dot(p_new.astype(BF16), kn)

    n_chunks = n_pages // ch
    for c in range(n_chunks):
        kb = ckv_buf[slot, c * ch:(c + 1) * ch].reshape(ck, KV_LORA).astype(BF16)
        krt = jnp.concatenate([krt_buf[slot, c * ch + i] for i in range(ch)], axis=1).astype(BF16)
        kb_ref[c] = kb
        s_ref[c] = _dot_nt(q_lat, kb) + _dot(q_rope, krt)

    def softmax(c):
        s = s_ref[c]
        mc = _row_max(_slab_max(jnp.full((rows, LANES), NEG_INF, F32), s))
        ps = [_exp_scaled(sl - mc, scale) for sl in _lane_slabs(s)]
        m_ref[c] = mc
        l_ref[c] = functools.reduce(operator.add, ps)
        return jnp.concatenate(ps, axis=1).astype(BF16)

    nxt = softmax(0)
    for c in range(n_chunks):
        p = nxt
        if c + 1 < n_chunks:
            nxt = softmax(c + 1)
        acc_ref[c] = _dot(p, kb_ref[c])

    m = functools.reduce(jnp.maximum, [m_ref[c] for c in range(n_chunks)],
                         jnp.broadcast_to(m_new, (rows, LANES)))
    l_lanes = jnp.zeros((rows, LANES), F32)
    acc = [jnp.zeros((rows, LANES), F32)] * (KV_LORA // LANES)
    for c in range(n_chunks):
        w = _exp_scaled(m_ref[c] - m, scale)
        l_lanes = l_lanes + w * l_ref[c]
        acc = [a + w * sl for a, sl in zip(acc, _lane_slabs(acc_ref[c]))]
    w_new = _exp_scaled(m_new - m[:, :1], scale)
    l = jnp.sum(l_lanes, axis=1, keepdims=True) + w_new * l_new
    o = (jnp.concatenate(acc, axis=1) + w_new * acc_new) / l
    o_ref[...] = o.reshape(MLA_HEADS, dec_seq, KV_LORA)


def _mla_sample(page_table, qf, new_ckv, new_kr, cache_ckv, cache_krope_t, layer, *, dec_batch, dec_seq):
    n_pages = page_table.shape[1]
    chunk_pages = min(MLA_CHUNK_PAGES, n_pages)
    n_chunks = n_pages // chunk_pages
    chunk_keys = chunk_pages * PAGE_SIZE
    rows = MLA_HEADS * dec_seq
    scale = (QK_NOPE + QK_ROPE) ** -0.5
    in_specs = [
        pl.BlockSpec((MLA_HEADS, dec_seq, QK_PAD), lambda b, pt: (0, b, 0)),
        pl.BlockSpec((dec_seq, KV_LORA), lambda b, pt: (b, 0)),
        pl.BlockSpec((dec_seq, QK_ROPE), lambda b, pt: (b, 0)),
        pl.BlockSpec(memory_space=pl.ANY),
        pl.BlockSpec(memory_space=pl.ANY),
    ]
    return pl.pallas_call(
        functools.partial(_mla_sample_kernel, layer=layer, dec_seq=dec_seq, n_pages=n_pages, scale=scale),
        grid_spec=pltpu.PrefetchScalarGridSpec(
            num_scalar_prefetch=1, grid=(dec_batch,), in_specs=in_specs,
            out_specs=pl.BlockSpec((MLA_HEADS, dec_seq, KV_LORA), lambda b, pt: (0, b, 0)),
            scratch_shapes=[pltpu.VMEM((2, n_pages, PAGE_SIZE, KV_LORA), F32),
                            pltpu.VMEM((2, n_pages, QK_ROPE, PAGE_SIZE), F32),
                            pltpu.SemaphoreType.DMA((2, 2)),
                            pltpu.VMEM((n_chunks, chunk_keys, KV_LORA), BF16),
                            pltpu.VMEM((n_chunks, rows, chunk_keys), F32),
                            pltpu.VMEM((n_chunks, rows, LANES), F32),
                            pltpu.VMEM((n_chunks, rows, LANES), F32),
                            pltpu.VMEM((n_chunks, rows, KV_LORA), F32)]),
        out_shape=jax.ShapeDtypeStruct((MLA_HEADS, dec_batch * dec_seq, KV_LORA), F32),
        compiler_params=pltpu.CompilerParams(
            dimension_semantics=("arbitrary",), vmem_limit_bytes=VMEM_LIMIT),
        name="mla_sample",
    )(page_table, qf, new_ckv, new_kr, cache_ckv, cache_krope_t)


def _uv_kernel(o_lat_ref, wuv_ref, o_ref):
    o = _dot(o_lat_ref[0].astype(BF16), wuv_ref[0])
    for h in range(1, MLA_HEADS):
        o = o + _dot(o_lat_ref[h].astype(BF16), wuv_ref[h])
    o_ref[...] = o


def _uv_sample(o_lat, w_uv):
    n = o_lat.shape[1]
    return pl.pallas_call(
        _uv_kernel,
        out_shape=jax.ShapeDtypeStruct((n, BRANCH_WIDTH), F32),
        compiler_params=pltpu.CompilerParams(vmem_limit_bytes=VMEM_LIMIT),
        name="uv_sample",
    )(o_lat, w_uv)


def _moba_sample_kernel(pt_ref, q_ref, nk_ref, nv_ref, slope_ref, kt_hbm, vt_hbm, o_ref,
                        kt_buf, vt_buf, sems, s_ref, m_ref, l_ref, acc_ref,
                        *, layer, dec_seq, n_pages, past_len):
    slot = _page_pipeline(pt_ref, layer, n_pages, (kt_hbm, vt_hbm), (kt_buf, vt_buf), sems)
    rows = MOBA_HEADS * dec_seq
    nb_past = n_pages // PAGES_PER_BLOCK
    wide = (rows, LANES)
    qf = q_ref[...].reshape(rows, MOBA_KV_WIDTH)
    q, scale = _prescaled_query(qf, MOBA_HEAD_DIM ** -0.5)
    slope = slope_ref[...]
    t_q = past_len + lax.broadcasted_iota(jnp.int32, (MOBA_HEADS, dec_seq, LANES), 1).reshape(wide)
    off_k = lax.broadcasted_iota(jnp.int32, (1, MOBA_BLOCK), 1).astype(F32)
    alibi_off = slope[:, :1] * off_k
    lane = lax.broadcasted_iota(jnp.int32, wide, 1)
    lane_km = lax.broadcasted_iota(jnp.int32, (MOBA_KV_WIDTH, LANES), 1)

    def block_t(buf, j):
        return jnp.concatenate([buf[slot, PAGES_PER_BLOCK * j + i] for i in range(PAGES_PER_BLOCK)], axis=1)

    km = jnp.zeros((MOBA_KV_WIDTH, LANES), F32)
    for j in range(nb_past):
        kt = block_t(kt_buf, j)
        km = jnp.where(lane_km == j, jnp.sum(kt, axis=1, keepdims=True) / MOBA_BLOCK, km)
        s_ref[j] = _dot(q, kt.astype(BF16))

    gate = _gate_scores(qf, km, True)
    sel = _topk_blocks(gate, lane < nb_past, min(MOBA_TOPK, nb_past), 1)

    def softmax(j):
        shift = -slope * (t_q - j * MOBA_BLOCK).astype(F32)
        raw = s_ref[j]
        s = [sl + shift for sl in _lane_slabs(_times(raw, scale) + alibi_off)]
        mj = _row_max(functools.reduce(jnp.maximum, s))
        ps = [jnp.exp(sl - mj) for sl in s]
        m_ref[j] = mj
        l_ref[j] = jnp.broadcast_to(jnp.sum(functools.reduce(operator.add, ps), axis=1, keepdims=True), wide)
        return jnp.concatenate(ps, axis=1).astype(BF16)

    nxt = softmax(0)
    for j in range(nb_past):
        p = nxt
        if j + 1 < nb_past:
            nxt = softmax(j + 1)
        acc_ref[j] = _dot_nt(block_t(vt_buf, j).astype(BF16), p)

    kn = nk_ref[...].astype(BF16)
    t_k = lax.broadcasted_iota(jnp.int32, (1, dec_seq), 1)
    dist = (t_q[:, :1] - past_len) - t_k
    s = _times(_dot_nt(q, kn), scale) - slope[:, :1] * dist.astype(F32)
    s = jnp.where(dist >= 0, s, NEG_INF)
    m_own = jnp.max(s, axis=1, keepdims=True)
    p = jnp.exp(s - m_own)
    l_own = jnp.sum(p, axis=1, keepdims=True)
    acc_own = _dot(p.astype(BF16), nv_ref[...].astype(BF16))

    m_blk = jnp.zeros(wide, F32)
    l_blk = jnp.zeros(wide, F32)
    for j in range(nb_past):
        m_blk = jnp.where(lane == j, m_ref[j], m_blk)
        l_blk = jnp.where(lane == j, l_ref[j], l_blk)
    m_tot = jnp.maximum(m_own, jnp.max(jnp.where(sel, m_blk, NEG_INF), axis=1, keepdims=True))
    w_blk = jnp.where(sel, jnp.exp(m_blk - m_tot), 0.0)
    w_own = jnp.exp(m_own - m_tot)
    l_tot = jnp.sum(w_blk * l_blk, axis=1, keepdims=True) + w_own * l_own
    w_t = jnp.concatenate([w_blk, jnp.zeros((LANES - rows, LANES), F32)], axis=0).T[:, :rows]
    acc_t = jnp.zeros((MOBA_KV_WIDTH, rows), F32)
    for j in range(nb_past):
        acc_t = acc_t + w_t[j:j + 1, :] * acc_ref[j]
    acc_t = jnp.concatenate([acc_t, jnp.zeros((MOBA_KV_WIDTH, LANES - rows), F32)], axis=1)
    o = (acc_t.T[:rows, :] + w_own * acc_own) / l_tot
    pieces = []
    for h in range(MOBA_HEADS):
        g = h // MOBA_GROUP
        pieces.append(o[h * dec_seq:(h + 1) * dec_seq, g * MOBA_HEAD_DIM:(g + 1) * MOBA_HEAD_DIM])
    o_ref[...] = jnp.concatenate(pieces, axis=1)


def _moba_sample(page_table, qm, new_k, new_v, slope_rows, cache_kt, cache_vt, layer, *, dec_batch, dec_seq):
    n_pages = page_table.shape[1]
    past_len = n_pages * PAGE_SIZE
    nb_past = past_len // MOBA_BLOCK
    rows = MOBA_HEADS * dec_seq
    in_specs = [
        pl.BlockSpec((MOBA_HEADS, dec_seq, MOBA_KV_WIDTH), lambda b, pt: (0, b, 0)),
        pl.BlockSpec((dec_seq, MOBA_KV_WIDTH), lambda b, pt: (b, 0)),
        pl.BlockSpec((dec_seq, MOBA_KV_WIDTH), lambda b, pt: (b, 0)),
        pl.BlockSpec((rows, LANES), lambda b, pt: (0, 0)),
        pl.BlockSpec(memory_space=pl.ANY),
        pl.BlockSpec(memory_space=pl.ANY),
    ]
    page_buf = pltpu.VMEM((2, n_pages, MOBA_KV_WIDTH, PAGE_SIZE), F32)
    stat = pltpu.VMEM((nb_past, rows, LANES), F32)
    return pl.pallas_call(
        functools.partial(_moba_sample_kernel, layer=layer, dec_seq=dec_seq, n_pages=n_pages,
                          past_len=past_len),
        grid_spec=pltpu.PrefetchScalarGridSpec(
            num_scalar_prefetch=1, grid=(dec_batch,), in_specs=in_specs,
            out_specs=pl.BlockSpec((dec_seq, BRANCH_WIDTH), lambda b, pt: (b, 0)),
            scratch_shapes=[page_buf, page_buf, pltpu.SemaphoreType.DMA((2, 2)),
                            pltpu.VMEM((nb_past, rows, MOBA_BLOCK), F32), stat, stat,
                            pltpu.VMEM((nb_past, MOBA_KV_WIDTH, rows), F32)]),
        out_shape=jax.ShapeDtypeStruct((dec_batch * dec_seq, BRANCH_WIDTH), F32),
        compiler_params=pltpu.CompilerParams(
            dimension_semantics=("arbitrary",), vmem_limit_bytes=VMEM_LIMIT),
        name="moba_sample",
    )(page_table, qm, new_k, new_v, slope_rows, cache_kt, cache_vt)


def _ffn_kernel(x_ref, oa_ref, ob_ref, g_attn_ref, wgate_ref, wbr_ref, wo_ref, g_ffn_ref, wgu_ref, wdn_ref,
                g_fin_ref, out_ref, *, final, n_chunks):
    x = x_ref[...]
    xn = _rms(x, g_attn_ref[...]).astype(BF16)
    merged = (jax.nn.sigmoid(_dot(xn, wgate_ref[:, :D_MODEL])) * _dot(oa_ref[...].astype(BF16), wbr_ref[0])
              + jax.nn.sigmoid(_dot(xn, wgate_ref[:, D_MODEL:])) * _dot(ob_ref[...].astype(BF16), wbr_ref[1]))
    h = x + _dot(merged.astype(BF16), wo_ref[...])
    hn = _rms(h, g_ffn_ref[...]).astype(BF16)
    cw = FFN_HIDDEN // n_chunks
    acc = h
    for c in range(n_chunks):
        a = _dot(hn, wgu_ref[:, c * cw:(c + 1) * cw])
        u = _dot(hn, wgu_ref[:, FFN_HIDDEN + c * cw:FFN_HIDDEN + (c + 1) * cw])
        act = (jax.nn.silu(a) * u).astype(BF16)
        acc = acc + _dot(act, wdn_ref[c * cw:(c + 1) * cw, :])
    out_ref[...] = _rms(acc, g_fin_ref[...]) if final else acc


def _ffn(x, o_a, o_b, w, stacked, layer, g_final, *, final, tm, n_chunks=2):
    n = x.shape[0]
    row = lambda i: (i, 0)

    def layer_spec(shape):
        return pl.BlockSpec((None,) + shape, lambda *_: (layer,) + (0,) * len(shape),
                            pipeline_mode=pl.Buffered(1))

    return pl.pallas_call(
        functools.partial(_ffn_kernel, final=final, n_chunks=n_chunks),
        grid=(n // tm,),
        in_specs=[
            pl.BlockSpec((tm, D_MODEL), row),
            pl.BlockSpec((tm, BRANCH_WIDTH), row),
            pl.BlockSpec((tm, BRANCH_WIDTH), row),
            _const_spec((1, D_MODEL)),
            _const_spec((D_MODEL, 2 * D_MODEL)),
            layer_spec((2, BRANCH_WIDTH, D_MODEL)),
            layer_spec((D_MODEL, D_MODEL)),
            _const_spec((1, D_MODEL)),
            layer_spec((D_MODEL, 2 * FFN_HIDDEN)),
            layer_spec((FFN_HIDDEN, D_MODEL)),
            _const_spec((1, D_MODEL)),
        ],
        out_specs=pl.BlockSpec((tm, D_MODEL), row),
        out_shape=jax.ShapeDtypeStruct((n, D_MODEL), F32),
        compiler_params=pltpu.CompilerParams(
            dimension_semantics=("arbitrary",), vmem_limit_bytes=VMEM_LIMIT),
        name="ffn_final" if final else "ffn",
    )(x, o_a, o_b, w["g_attn"], w["w_gate"], stacked["w_branch"], stacked["w_o"], w["g_ffn"],
      stacked["w_gu"], stacked["w_down"], g_final)


def _prep_layer_weights(w_in, g_q, g_kv, w_uq, w_uk, w_uv, g_attn, g_ffn):
    s = [0, Q_LORA, Q_LORA + KV_LORA, Q_LORA + KV_LORA + QK_ROPE]
    s.append(s[-1] + MOBA_HEADS * MOBA_HEAD_DIM)
    s.append(s[-1] + MOBA_KV_WIDTH)
    s.append(s[-1] + MOBA_KV_WIDTH)
    c_q, c_kv, k_r, m_q, m_k, m_v, gate = (w_in[:, s[0]:s[1]], w_in[:, s[1]:s[2]], w_in[:, s[2]:s[3]],
                                           w_in[:, s[3]:s[4]], w_in[:, s[4]:s[5]], w_in[:, s[5]:s[6]],
                                           w_in[:, s[6]:])
    m_q = m_q.reshape(D_MODEL, MOBA_KV_HEADS, MOBA_GROUP, MOBA_HEAD_DIM)
    m_q_paired = jnp.transpose(m_q, (0, 2, 1, 3)).reshape(D_MODEL, MOBA_HEADS * MOBA_HEAD_DIM)
    unused = jnp.zeros((D_MODEL, _OFF_CKV - _OFF_KR - QK_ROPE), w_in.dtype)
    w_in_perm = jnp.concatenate([c_q, k_r, unused, c_kv, m_q_paired, m_k, m_v], axis=1).astype(BF16)

    uq = w_uq.reshape(Q_LORA, MLA_HEADS, QK_NOPE + QK_ROPE)
    w_uqn = jnp.pad(uq[:, :, :QK_NOPE], ((0, 0), (0, 0), (0, LANES - QK_NOPE))).reshape(Q_LORA, MLA_HEADS * LANES)
    w_uqr = jnp.concatenate([uq[:, :, QK_NOPE:QK_NOPE + HALF_ROPE].reshape(Q_LORA, LANES),
                             uq[:, :, QK_NOPE + HALF_ROPE:].reshape(Q_LORA, LANES)], axis=1)
    uk = jnp.transpose(w_uk, (1, 2, 0))
    uk = jnp.pad(uk, ((0, 0), (0, LANES - QK_NOPE), (0, 0)))
    uv = jnp.transpose(w_uv, (1, 0, 2))
    eye = jnp.eye(MLA_HEADS, dtype=w_uv.dtype)
    uv_pad = (uv[:, :, None, :] * eye[:, None, :, None]).reshape(MLA_HEADS, KV_LORA, BRANCH_WIDTH)
    return {
        "w_in": w_in_perm, "w_gate": gate.astype(BF16), "g_q": g_q[None], "g_kv": g_kv[None],
        "g_attn": g_attn[None], "g_ffn": g_ffn[None],
        "w_uqn": w_uqn.astype(BF16), "w_uqr": w_uqr.astype(BF16), "w_uk": uk.astype(BF16),
        "w_uv": uv_pad.astype(BF16),
    }


def _rope_tables(pos):
    inv = ROPE_THETA ** (-jnp.arange(HALF_ROPE, dtype=F32) / HALF_ROPE)
    ang = pos.astype(F32)[:, None] * inv[None, :]
    reps = LANES // HALF_ROPE
    return jnp.tile(jnp.cos(ang), (1, reps)), jnp.tile(jnp.sin(ang), (1, reps))


def _slope_rows(tokens_per_head, width):
    slopes = 2.0 ** (-8.0 * jnp.arange(1, MOBA_HEADS + 1, dtype=F32) / MOBA_HEADS)
    return jnp.broadcast_to(jnp.repeat(slopes, tokens_per_head)[:, None], (MOBA_HEADS * tokens_per_head, width))


def kernel(x_prompt, x_sample, cache_ckv, cache_krope, cache_k, cache_v, page_table, w_in, g_q, g_kv, w_uq, w_uk, w_uv, w_branch, w_o, g_attn, g_ffn, w_gu, w_down, g_final):
    batch, seq, _ = x_prompt.shape
    dec_batch, dec_seq, _ = x_sample.shape
    depth, n_phys = cache_k.shape[:2]
    past_len = page_table.shape[1] * PAGE_SIZE
    n_p, n_s = batch * seq, dec_batch * dec_seq
    tm_p = min(512, seq)
    tm_s = min(512, n_s)

    cache_krope_t = jnp.transpose(cache_krope, (0, 1, 3, 2))
    cache_kt = jnp.transpose(cache_k, (0, 1, 3, 4, 2)).reshape(depth, n_phys, MOBA_KV_WIDTH, PAGE_SIZE)
    cache_vt = jnp.transpose(cache_v, (0, 1, 3, 4, 2)).reshape(depth, n_phys, MOBA_KV_WIDTH, PAGE_SIZE)

    cos_p, sin_p = _rope_tables(jnp.arange(seq, dtype=jnp.int32))
    pos_s = past_len + jnp.arange(dec_seq, dtype=jnp.int32)
    cos_s, sin_s = _rope_tables(jnp.tile(pos_s, tm_s // dec_seq))
    slope_p = _slope_rows(MOBA_BLOCK, 1)
    slope_s = _slope_rows(dec_seq, LANES)
    g_fin = g_final[None]
    stacked = {"w_branch": w_branch.astype(BF16), "w_o": w_o.astype(BF16),
               "w_gu": w_gu.astype(BF16), "w_down": w_down.astype(BF16)}

    hp = x_prompt.reshape(n_p, D_MODEL)
    hs = x_sample.reshape(n_s, D_MODEL)
    rows_p, rows_s = [], []
    for l in range(depth):
        w = _prep_layer_weights(w_in[l], g_q[l], g_kv[l], w_uq[l], w_uk[l], w_uv[l], g_attn[l], g_ffn[l])
        final = l == depth - 1

        ckv, kr, mk, mv, qf, qm, kf, mkv, kmean = _proj(hp, cos_p, sin_p, w, prompt=True, tm=tm_p)
        o_a = _mla_prompt(qf, kf, w["w_uv"], batch=batch, seq=seq, tq=min(256, seq), tk=min(256, seq))
        o_b = _moba_prompt(qm, mkv, kmean, slope_p, batch=batch, seq=seq)
        hp = _ffn(hp, o_a, o_b, w, stacked, l, g_fin, final=final, tm=tm_p)
        rows_p.append((ckv, kr, mk, mv))

        ckv, kr, mk, mv, qf, qm = _proj(hs, cos_s, sin_s, w, prompt=False, tm=tm_s)
        o_lat = _mla_sample(page_table, qf, ckv, kr, cache_ckv, cache_krope_t, l,
                            dec_batch=dec_batch, dec_seq=dec_seq)
        o_a = _uv_sample(o_lat, w["w_uv"])
        o_b = _moba_sample(page_table, qm, mk, mv, slope_s, cache_kt, cache_vt, l,
                           dec_batch=dec_batch, dec_seq=dec_seq)
        hs = _ffn(hs, o_a, o_b, w, stacked, l, g_fin, final=final, tm=tm_s)
        rows_s.append((ckv, kr, mk, mv))

    def stack(rows, i, shape):
        return jnp.stack([r[i] for r in rows]).reshape((depth,) + shape)

    kv_shape = (MOBA_KV_HEADS, MOBA_HEAD_DIM)
    return (hp.reshape(batch, seq, D_MODEL),
            hs.reshape(dec_batch, dec_seq, D_MODEL),
            stack(rows_p, 0, (batch, seq, KV_LORA)),
            stack(rows_p, 1, (batch, seq, QK_ROPE)),
            stack(rows_p, 2, (batch, seq) + kv_shape),
            stack(rows_p, 3, (batch, seq) + kv_shape),
            stack(rows_s, 0, (dec_batch, dec_seq, KV_LORA)),
            stack(rows_s, 1, (dec_batch, dec_seq, QK_ROPE)),
            stack(rows_s, 2, (dec_batch, dec_seq) + kv_shape),
            stack(rows_s, 3, (dec_batch, dec_seq) + kv_shape))
```

```python
import functools
import math
import operator

import jax
import jax.numpy as jnp
from jax import lax
from jax.experimental import pallas as pl
from jax.experimental.pallas import tpu as pltpu

F32 = jnp.float32
BF16 = jnp.bfloat16

D_MODEL = 1024
PAGE_SIZE = 128
MLA_HEADS = 8
Q_LORA = 384
KV_LORA = 256
QK_NOPE = 64
QK_ROPE = 32
ROPE_THETA = 10000.0
MOBA_HEADS = 8
MOBA_KV_HEADS = 2
MOBA_GROUP = MOBA_HEADS // MOBA_KV_HEADS
MOBA_HEAD_DIM = 64
MOBA_BLOCK = 256
MOBA_TOPK = 3
BRANCH_WIDTH = 512
FFN_HIDDEN = 2816
RMS_EPS = 1e-6

LANES = 128
QK_PAD = KV_LORA + LANES
MOBA_KV_WIDTH = MOBA_KV_HEADS * MOBA_HEAD_DIM
HALF_ROPE = QK_ROPE // 2
PAGES_PER_BLOCK = MOBA_BLOCK // PAGE_SIZE

_OFF_CQ = 0
_OFF_KR = _OFF_CQ + Q_LORA
_OFF_CKV = -(-(_OFF_KR + QK_ROPE) // LANES) * LANES
_OFF_MQ = _OFF_CKV + KV_LORA
_OFF_MK = _OFF_MQ + MOBA_HEADS * MOBA_HEAD_DIM
_OFF_MV = _OFF_MK + MOBA_KV_WIDTH
_D_IN_PERM = _OFF_MV + MOBA_KV_WIDTH

VMEM_LIMIT = 56 * 1024 * 1024
MLA_CHUNK_PAGES = 16
DMA_ISSUE_UNROLL = 4
NEG_INF = float("-inf")
LOG2_E = 1.4426950408889634


def _rms(x, g):
    return x * lax.rsqrt(jnp.mean(x * x, axis=-1, keepdims=True) + RMS_EPS) * g


def _dot(a, b):
    return jnp.dot(a, b, preferred_element_type=F32)


def _dot_nt(a, b):
    return lax.dot_general(a, b, (((1,), (1,)), ((), ())), preferred_element_type=F32)


def _const_spec(shape):
    return pl.BlockSpec(shape, lambda *_: (0,) * len(shape), pipeline_mode=pl.Buffered(1))


def _lane_slabs(x):
    return [x[:, j * LANES:(j + 1) * LANES] for j in range(x.shape[1] // LANES)]


def _slab_max(m, s):
    return functools.reduce(jnp.maximum, _lane_slabs(s), m)


def _row_max(m_slab):
    return jnp.broadcast_to(jnp.max(m_slab, axis=1, keepdims=True), m_slab.shape)


def _proj_kernel(x_ref, cos_ref, sin_ref, g_attn_ref, w_in_ref, g_q_ref, g_kv_ref,
                 w_uqn_ref, w_uqr_ref, w_uk_ref, *out_refs, prompt):
    if prompt:
        (ckv_ref, kr_ref, mk_ref, mv_ref, qf_ref, qm_ref,
         kf_ref, mkv_ref, kmean_ref) = out_refs
    else:
        ckv_ref, kr_ref, mk_ref, mv_ref, qf_ref, qm_ref = out_refs
    tm = x_ref.shape[0]
    hn = _rms(x_ref[...], g_attn_ref[...]).astype(BF16)

    def seg(lo, hi):
        return _dot(hn, w_in_ref[:, lo:hi])

    cos = cos_ref[...]
    sin = sin_ref[...]

    ckv = _rms(seg(_OFF_CKV, _OFF_MQ), g_kv_ref[...])
    ckv_ref[...] = ckv
    cq_kr = seg(_OFF_CQ, _OFF_CKV)
    kr = cq_kr[:, _OFF_KR:_OFF_KR + QK_ROPE]
    kr_swapped = jnp.concatenate([kr[:, HALF_ROPE:], kr[:, :HALF_ROPE]], axis=1)
    lane32 = lax.broadcasted_iota(jnp.int32, (tm, QK_ROPE), 1)
    sin_signed = jnp.where(lane32 < HALF_ROPE, -sin[:, :QK_ROPE], sin[:, :QK_ROPE])
    kr = kr * cos[:, :QK_ROPE] + kr_swapped * sin_signed
    kr_ref[...] = kr

    mk = seg(_OFF_MK, _OFF_MV)
    mv = seg(_OFF_MV, _D_IN_PERM)
    mk_ref[...] = mk
    mv_ref[...] = mv

    mq = seg(_OFF_MQ, _OFF_MK)
    lower = lax.broadcasted_iota(jnp.int32, (tm, LANES), 1) < MOBA_HEAD_DIM
    for k in range(MOBA_GROUP):
        slab = mq[:, k * LANES:(k + 1) * LANES]
        qm_ref[k] = jnp.where(lower, slab, 0.0).astype(qm_ref.dtype)
        qm_ref[MOBA_GROUP + k] = jnp.where(lower, 0.0, slab).astype(qm_ref.dtype)

    cq = _rms(cq_kr[:, :Q_LORA], g_q_ref[...]).astype(BF16)
    q_nope = _dot(cq, w_uqn_ref[...]).astype(BF16)
    q_rot = _dot(cq, w_uqr_ref[...])
    r1, r2 = q_rot[:, :LANES], q_rot[:, LANES:]
    o1 = r1 * cos - r2 * sin
    o2 = r2 * cos + r1 * sin
    lane = lax.broadcasted_iota(jnp.int32, (tm, LANES), 1)
    for h in range(MLA_HEADS):
        q_lat = _dot(q_nope[:, h * LANES:(h + 1) * LANES], w_uk_ref[h])
        shift_a = (LANES - HALF_ROPE * h) % LANES
        shift_b = (HALF_ROPE - HALF_ROPE * h) % LANES
        a = o1 if shift_a == 0 else pltpu.roll(o1, shift_a, axis=1)
        b = o2 if shift_b == 0 else pltpu.roll(o2, shift_b, axis=1)
        rope = jnp.where(lane < HALF_ROPE, a, jnp.where(lane < QK_ROPE, b, 0.0))
        qf_ref[h, :, :KV_LORA] = q_lat.astype(qf_ref.dtype)
        qf_ref[h, :, KV_LORA:] = rope.astype(qf_ref.dtype)

    if prompt:
        kf_ref[:, :KV_LORA] = ckv.astype(BF16)
        kr_pad = jnp.concatenate([kr, jnp.zeros((tm, LANES - QK_ROPE), F32)], axis=1)
        kf_ref[:, KV_LORA:] = kr_pad.astype(BF16)
        mkv_ref[:, :MOBA_KV_WIDTH] = mk.astype(BF16)
        mkv_ref[:, MOBA_KV_WIDTH:] = mv.astype(BF16)
        for j in range(tm // MOBA_BLOCK):
            kmean_ref[j] = jnp.mean(mk[j * MOBA_BLOCK:(j + 1) * MOBA_BLOCK], axis=0, keepdims=True)


def _proj(x, cos_tab, sin_tab, w, *, prompt, tm):
    n = x.shape[0]
    n_tab = cos_tab.shape[0] // tm
    grid = (n // tm,)
    row = lambda i: (i, 0)
    q_dtype = BF16 if prompt else F32
    out_shape = [
        jax.ShapeDtypeStruct((n, KV_LORA), F32),
        jax.ShapeDtypeStruct((n, QK_ROPE), F32),
        jax.ShapeDtypeStruct((n, MOBA_KV_WIDTH), F32),
        jax.ShapeDtypeStruct((n, MOBA_KV_WIDTH), F32),
        jax.ShapeDtypeStruct((MLA_HEADS, n, QK_PAD), q_dtype),
        jax.ShapeDtypeStruct((MOBA_HEADS, n, MOBA_KV_WIDTH), F32),
    ]
    out_specs = [
        pl.BlockSpec((tm, KV_LORA), row),
        pl.BlockSpec((tm, QK_ROPE), row),
        pl.BlockSpec((tm, MOBA_KV_WIDTH), row),
        pl.BlockSpec((tm, MOBA_KV_WIDTH), row),
        pl.BlockSpec((MLA_HEADS, tm, QK_PAD), lambda i: (0, i, 0)),
        pl.BlockSpec((MOBA_HEADS, tm, MOBA_KV_WIDTH), lambda i: (0, i, 0)),
    ]
    if prompt:
        nblk = tm // MOBA_BLOCK
        out_shape += [
            jax.ShapeDtypeStruct((n, QK_PAD), BF16),
            jax.ShapeDtypeStruct((n, 2 * MOBA_KV_WIDTH), BF16),
            jax.ShapeDtypeStruct((n // MOBA_BLOCK, 1, MOBA_KV_WIDTH), F32),
        ]
        out_specs += [
            pl.BlockSpec((tm, QK_PAD), row),
            pl.BlockSpec((tm, 2 * MOBA_KV_WIDTH), row),
            pl.BlockSpec((nblk, 1, MOBA_KV_WIDTH), lambda i: (i, 0, 0)),
        ]
    tab = lambda i: (i % n_tab, 0)
    in_specs = [
        pl.BlockSpec((tm, D_MODEL), row),
        pl.BlockSpec((tm, LANES), tab),
        pl.BlockSpec((tm, LANES), tab),
        _const_spec((1, D_MODEL)),
        _const_spec((D_MODEL, _D_IN_PERM)),
        _const_spec((1, Q_LORA)),
        _const_spec((1, KV_LORA)),
        _const_spec((Q_LORA, MLA_HEADS * LANES)),
        _const_spec((Q_LORA, 2 * LANES)),
        _const_spec((MLA_HEADS, LANES, KV_LORA)),
    ]
    return pl.pallas_call(
        functools.partial(_proj_kernel, prompt=prompt),
        grid=grid, in_specs=in_specs, out_specs=out_specs, out_shape=out_shape,
        compiler_params=pltpu.CompilerParams(
            dimension_semantics=("arbitrary",), vmem_limit_bytes=VMEM_LIMIT),
        name="proj_prompt" if prompt else "proj_sample",
    )(x, cos_tab, sin_tab, w["g_attn"], w["w_in"], w["g_q"], w["g_kv"],
      w["w_uqn"], w["w_uqr"], w["w_uk"])


def _exp_scaled(x, scale):
    return jnp.exp(x) if scale == 1.0 else jnp.exp2(x * (scale * LOG2_E))


def _exp_accumulate(s, m, scale, v, l_ref, acc_ref):
    ps = [_exp_scaled(sl - m, scale) for sl in _lane_slabs(s)]
    l_ref[...] += functools.reduce(operator.add, ps)
    acc_ref[...] += _dot(jnp.concatenate(ps, axis=1).astype(BF16), v)


def _mla_prompt_kernel(q_ref, k_ref, wuv_ref, o_ref, s_ref, m_ref, l_ref, acc_ref, *, tq, tk, scale):
    qi = pl.program_id(1)
    rows = MLA_HEADS * tq
    q = q_ref[...].reshape(rows, QK_PAD)
    last = (qi * tq + tq - 1) // tk

    def k_chunk(kc):
        return k_ref[pl.ds(pl.multiple_of(kc * tk, tk), tk), :]

    def scores(kc, masked):
        s = _dot_nt(q, k_chunk(kc))
        if masked:
            q_pos = qi * tq + lax.broadcasted_iota(jnp.int32, (MLA_HEADS, tq, tk), 1).reshape(rows, tk)
            k_pos = kc * tk + lax.broadcasted_iota(jnp.int32, (rows, tk), 1)
            s = jnp.where(k_pos <= q_pos, s, NEG_INF)
        s_ref[kc] = s
        m_ref[...] = _slab_max(m_ref[...], s)

    m_ref[...] = jnp.full(m_ref.shape, NEG_INF, F32)

    def score_body(kc, carry):
        scores(kc, False)
        return carry

    lax.fori_loop(0, last, score_body, 0)
    scores(last, True)

    m_ref[...] = _row_max(m_ref[...])
    l_ref[...] = jnp.zeros(l_ref.shape, F32)
    acc_ref[...] = jnp.zeros(acc_ref.shape, F32)

    def value_pair_body(kp, carry):
        kc = 2 * kp
        s_pair = jnp.concatenate([s_ref[kc], s_ref[kc + 1]], axis=1)
        v_pair = k_ref[pl.ds(pl.multiple_of(kc * tk, tk), 2 * tk), :KV_LORA]
        _exp_accumulate(s_pair, m_ref[...], scale, v_pair, l_ref, acc_ref)
        return carry

    n_chunks = last + 1
    n_pairs = 0
    if k_ref.shape[0] >= 2 * tk:
        n_pairs = n_chunks // 2
        lax.fori_loop(0, n_pairs, value_pair_body, 0)

    @pl.when(n_chunks - 2 * n_pairs == 1)
    def _():
        _exp_accumulate(s_ref[last], m_ref[...], scale, k_chunk(last)[:, :KV_LORA], l_ref, acc_ref)

    o_lat = (acc_ref[...] / jnp.sum(l_ref[...], axis=1, keepdims=True)).astype(BF16)
    o = _dot(o_lat[:tq], wuv_ref[0])
    for h in range(1, MLA_HEADS):
        o = o + _dot(o_lat[h * tq:(h + 1) * tq], wuv_ref[h])
    o_ref[...] = o.astype(o_ref.dtype)


def _mla_prompt(qf, kf, w_uv, *, batch, seq, tq, tk):
    nq = seq // tq
    rows = MLA_HEADS * tq
    qf4 = qf.reshape(MLA_HEADS, batch, seq, QK_PAD)
    kf3 = kf.reshape(batch, seq, QK_PAD)
    scale = (QK_NOPE + QK_ROPE) ** -0.5
    return pl.pallas_call(
        functools.partial(_mla_prompt_kernel, tq=tq, tk=tk, scale=scale),
        grid=(batch, nq),
        in_specs=[
            pl.BlockSpec((MLA_HEADS, None, tq, QK_PAD), lambda b, i: (0, b, i, 0)),
            pl.BlockSpec((None, seq, QK_PAD), lambda b, i: (b, 0, 0)),
            _const_spec((MLA_HEADS, KV_LORA, BRANCH_WIDTH)),
        ],
        out_specs=pl.BlockSpec((tq, BRANCH_WIDTH), lambda b, i: (b * nq + i, 0)),
        out_shape=jax.ShapeDtypeStruct((batch * seq, BRANCH_WIDTH), BF16),
        scratch_shapes=[pltpu.VMEM((seq // tk, rows, tk), F32), pltpu.VMEM((rows, LANES), F32),
                        pltpu.VMEM((rows, LANES), F32), pltpu.VMEM((rows, KV_LORA), F32)],
        compiler_params=pltpu.CompilerParams(
            dimension_semantics=("arbitrary", "arbitrary"), vmem_limit_bytes=VMEM_LIMIT),
        name="mla_prompt",
    )(qf4, kf3, w_uv)


def _hi_lo(x):
    hi = x.astype(BF16)
    return hi, (x - hi.astype(F32)).astype(BF16)


def _gate_scores(qf, kmean, kmean_is_transposed):
    mm = _dot if kmean_is_transposed else _dot_nt
    q_hi, q_lo = _hi_lo(qf)
    k_hi, k_lo = _hi_lo(kmean)
    return mm(q_hi, k_hi) + (mm(q_hi, k_lo) + mm(q_lo, k_hi))


def _gate_scores_t(qf, kmean):
    q_hi, q_lo = _hi_lo(qf)
    k_hi, k_lo = _hi_lo(kmean)
    return _dot_nt(jnp.concatenate([k_hi, k_lo, k_hi], axis=1), jnp.concatenate([q_hi, q_hi, q_lo], axis=1))


def _prescaled_query(qf, scale):
    if math.frexp(scale)[0] == 0.5:
        return (qf * scale).astype(BF16), 1.0
    return qf.astype(BF16), scale


def _times(x, factor):
    return x if factor == 1.0 else x * factor


def _topk_blocks(gate, valid, n_sel, axis):
    nb = gate.shape[axis]
    blk = lax.broadcasted_iota(jnp.int32, gate.shape, axis).astype(F32)
    g = jnp.where(valid, gate, NEG_INF)
    picked = jnp.zeros(gate.shape, jnp.bool_)
    for _ in range(n_sel):
        cur = jnp.where(picked, NEG_INF, g)
        best = jnp.max(cur, axis=axis, keepdims=True)
        cand = jnp.logical_and(cur == best, jnp.logical_not(picked))
        idx = jnp.min(jnp.where(cand, blk, float(nb)), axis=axis, keepdims=True)
        picked = jnp.logical_or(picked, blk == idx)
    return jnp.logical_and(picked, valid)


def _moba_prompt_kernel(q_ref, kv_ref, kmean_ref, slope_ref, slope_t_ref, o_ref,
                        choice_ref, shift_ref, s_ref, m_ref, l_ref, acc_ref, *, nb):
    tq = MOBA_BLOCK
    qi = pl.program_id(1)
    rows = MOBA_HEADS * tq
    own_slot = nb - 1
    qf = q_ref[...].reshape(rows, MOBA_KV_WIDTH)
    q, scale = _prescaled_query(qf, MOBA_HEAD_DIM ** -0.5)
    slope = slope_ref[...]
    t_q = lax.broadcasted_iota(jnp.int32, (MOBA_HEADS, tq, 1), 1).reshape(rows, 1)
    off_k = lax.broadcasted_iota(jnp.int32, (1, tq), 1)

    blk_t = lax.broadcasted_iota(jnp.int32, (nb, rows), 0)
    n_sel = min(MOBA_TOPK, nb)

    @pl.when(qi <= n_sel)
    def _():
        choice_ref[...] = jnp.where(blk_t < qi, 0.0, NEG_INF)

    @pl.when(qi > n_sel)
    def _():
        gate_t = _gate_scores_t(qf, kmean_ref[...].reshape(nb, MOBA_KV_WIDTH))
        choice_ref[...] = jnp.where(_topk_blocks(gate_t, blk_t < qi, n_sel, 0), 0.0, NEG_INF)

    shift_t = choice_ref[...] - slope_t_ref[...] * ((qi - blk_t) * tq).astype(F32)
    shift_t = jnp.concatenate([shift_t, jnp.zeros((LANES - nb, rows), F32)], axis=0)
    shift_ref[...] = shift_t.T[:, :nb]
    dist_own = t_q - off_k
    alibi_own = -slope * dist_own.astype(F32)

    def block_rows(j):
        return pl.ds(j * tq if isinstance(j, int) else pl.multiple_of(j * tq, tq), tq)

    def keys(j):
        return kv_ref[block_rows(j), :MOBA_KV_WIDTH]

    def values(j):
        return kv_ref[block_rows(j), MOBA_KV_WIDTH:]

    def qk(j):
        return _times(_dot_nt(q, keys(j)), scale)

    s = jnp.where(dist_own >= 0, qk(qi) + alibi_own, NEG_INF)
    s_ref[own_slot] = s
    m_ref[...] = _slab_max(jnp.full(m_ref.shape, NEG_INF, F32), s)
    for j in range(nb - 1):
        @pl.when(j < qi)
        def _():
            sj = qk(j) + alibi_own + shift_ref[:, j:j + 1]
            s_ref[j] = sj
            m_ref[...] = _slab_max(m_ref[...], sj)

    m_ref[...] = _row_max(m_ref[...])
    l_ref[...] = jnp.zeros(l_ref.shape, F32)
    acc_ref[...] = jnp.zeros(acc_ref.shape, F32)
    _exp_accumulate(s_ref[own_slot], m_ref[...], 1.0, values(qi), l_ref, acc_ref)
    for j in range(nb - 1):
        @pl.when(j < qi)
        def _():
            _exp_accumulate(s_ref[j], m_ref[...], 1.0, values(j), l_ref, acc_ref)

    o = acc_ref[...] / jnp.sum(l_ref[...], axis=1, keepdims=True)
    pieces = []
    for h in range(MOBA_HEADS):
        g = h // MOBA_GROUP
        pieces.append(o[h * tq:(h + 1) * tq, g * MOBA_HEAD_DIM:(g + 1) * MOBA_HEAD_DIM])
    o_ref[...] = jnp.concatenate(pieces, axis=1).astype(o_ref.dtype)


def _moba_prompt(qm, mkv, kmean, slope_rows, *, batch, seq):
    tq = MOBA_BLOCK
    nb = seq // tq
    rows = MOBA_HEADS * tq
    qm4 = qm.reshape(MOBA_HEADS, batch, seq, MOBA_KV_WIDTH)
    mkv3 = mkv.reshape(batch, seq, 2 * MOBA_KV_WIDTH)
    kmean4 = kmean.reshape(batch, nb, 1, MOBA_KV_WIDTH)
    return pl.pallas_call(
        functools.partial(_moba_prompt_kernel, nb=nb),
        grid=(batch, nb),
        in_specs=[
            pl.BlockSpec((MOBA_HEADS, None, tq, MOBA_KV_WIDTH), lambda b, i: (0, b, i, 0)),
            pl.BlockSpec((None, seq, 2 * MOBA_KV_WIDTH), lambda b, i: (b, 0, 0)),
            pl.BlockSpec((None, nb, 1, MOBA_KV_WIDTH), lambda b, i: (b, 0, 0, 0)),
            _const_spec((rows, 1)),
            _const_spec((1, rows)),
        ],
        out_specs=pl.BlockSpec((tq, BRANCH_WIDTH), lambda b, i: (b * nb + i, 0)),
        out_shape=jax.ShapeDtypeStruct((batch * seq, BRANCH_WIDTH), BF16),
        scratch_shapes=[pltpu.VMEM((nb, rows), F32), pltpu.VMEM((rows, nb), F32),
                        pltpu.VMEM((nb, rows, tq), F32),
                        pltpu.VMEM((rows, LANES), F32), pltpu.VMEM((rows, LANES), F32),
                        pltpu.VMEM((rows, MOBA_KV_WIDTH), F32)],
        compiler_params=pltpu.CompilerParams(
            dimension_semantics=("arbitrary", "arbitrary"), vmem_limit_bytes=VMEM_LIMIT),
        name="moba_prompt",
    )(qm4, mkv3, kmean4, slope_rows, slope_rows.reshape(1, rows))


def _page_copies(layer, hbm_refs, bufs, sems, page, slot, p):
    return [pltpu.make_async_copy(hbm.at[layer, page], buf.at[slot, p], sems.at[i, slot])
            for i, (hbm, buf) in enumerate(zip(hbm_refs, bufs))]


def _page_pipeline(pt_ref, layer, n_pages, hbm_refs, bufs, sems):
    b = pl.program_id(0)
    slot = b % 2
    copies = functools.partial(_page_copies, layer, hbm_refs, bufs, sems)

    def start_fetch(request, into):
        def body(p, carry):
            for thread, cp in enumerate(copies(pt_ref[request, p], into, p)):
                cp.start(priority=thread % 2)
            return carry
        lax.fori_loop(0, n_pages, body, 0, unroll=DMA_ISSUE_UNROLL)

    @pl.when(b == 0)
    def _():
        start_fetch(0, 0)

    @pl.when(b + 1 < pl.num_programs(0))
    def _():
        start_fetch(b + 1, 1 - slot)

    for p in range(n_pages):
        for cp in copies(0, slot, p):
            cp.wait()
    return slot


def _mla_sample_kernel(pt_ref, q_ref, nckv_ref, nkr_ref, ckv_hbm, krt_hbm, o_ref,
                       ckv_buf, krt_buf, sems, kb_ref, s_ref, m_ref, l_ref, acc_ref,
                       *, layer, dec_seq, n_pages, scale):
    slot = _page_pipeline(pt_ref, layer, n_pages, (ckv_hbm, krt_hbm), (ckv_buf, krt_buf), sems)
    rows = MLA_HEADS * dec_seq
    ch = min(MLA_CHUNK_PAGES, n_pages)
    ck = ch * PAGE_SIZE
    q = q_ref[...].reshape(rows, QK_PAD)
    q_lat = q[:, :KV_LORA].astype(BF16)
    q_rope = q[:, KV_LORA:KV_LORA + QK_ROPE].astype(BF16)

    kn = nckv_ref[...].astype(BF16)
    s_new = _dot_nt(q_lat, kn) + _dot_nt(q_rope, nkr_ref[...].astype(BF16))
    t_q = lax.broadcasted_iota(jnp.int32, (MLA_HEADS, dec_seq, dec_seq), 1).reshape(rows, dec_seq)
    t_k = lax.broadcasted_iota(jnp.int32, (rows, dec_seq), 1)
    s_new = jnp.where(t_k <= t_q, s_new, NEG_INF)
    m_new = jnp.max(s_new, axis=1, keepdims=True)
    p_new = _exp_scaled(s_new - m_new, scale)
    l_new = jnp.sum(p_new, axis=1, keepdims=True)
    acc_new = _dot(p_new.astype(BF16), kn)

    n_chunks = n_pages // ch
    for c in range(n_chunks):
        kb = ckv_buf[slot, c * ch:(c + 1) * ch].reshape(ck, KV_LORA).astype(BF16)
        krt = jnp.concatenate([krt_buf[slot, c * ch + i] for i in range(ch)], axis=1).astype(BF16)
        kb_ref[c] = kb
        s_ref[c] = _dot_nt(q_lat, kb) + _dot(q_rope, krt)

    def softmax(c):
        s = s_ref[c]
        mc = _row_max(_slab_max(jnp.full((rows, LANES), NEG_INF, F32), s))
        ps = [_exp_scaled(sl - mc, scale) for sl in _lane_slabs(s)]
        m_ref[c] = mc
        l_ref[c] = functools.reduce(operator.add, ps)
        return jnp.concatenate(ps, axis=1).astype(BF16)

    nxt = softmax(0)
    for c in range(n_chunks):
        p = nxt
        if c + 1 < n_chunks:
            nxt = softmax(c + 1)
        acc_ref[c] = _dot(p, kb_ref[c])

    m = functools.reduce(jnp.maximum, [m_ref[c] for c in range(n_chunks)],
                         jnp.broadcast_to(m_new, (rows, LANES)))
    l_lanes = jnp.zeros((rows, LANES), F32)
    acc = [jnp.zeros((rows, LANES), F32)] * (KV_LORA // LANES)
    for c in range(n_chunks):
        w = _exp_scaled(m_ref[c] - m, scale)
        l_lanes = l_lanes + w * l_ref[c]
        acc = [a + w * sl for a, sl in zip(acc, _lane_slabs(acc_ref[c]))]
    w_new = _exp_scaled(m_new - m[:, :1], scale)
    l = jnp.sum(l_lanes, axis=1, keepdims=True) + w_new * l_new
    o = (jnp.concatenate(acc, axis=1) + w_new * acc_new) / l
    o_ref[...] = o.reshape(MLA_HEADS, dec_seq, KV_LORA)


def _mla_sample(page_table, qf, new_ckv, new_kr, cache_ckv, cache_krope_t, layer, *, dec_batch, dec_seq):
    n_pages = page_table.shape[1]
    chunk_pages = min(MLA_CHUNK_PAGES, n_pages)
    n_chunks = n_pages // chunk_pages
    chunk_keys = chunk_pages * PAGE_SIZE
    rows = MLA_HEADS * dec_seq
    scale = (QK_NOPE + QK_ROPE) ** -0.5
    in_specs = [
        pl.BlockSpec((MLA_HEADS, dec_seq, QK_PAD), lambda b, pt: (0, b, 0)),
        pl.BlockSpec((dec_seq, KV_LORA), lambda b, pt: (b, 0)),
        pl.BlockSpec((dec_seq, QK_ROPE), lambda b, pt: (b, 0)),
        pl.BlockSpec(memory_space=pl.ANY),
        pl.BlockSpec(memory_space=pl.ANY),
    ]
    return pl.pallas_call(
        functools.partial(_mla_sample_kernel, layer=layer, dec_seq=dec_seq, n_pages=n_pages, scale=scale),
        grid_spec=pltpu.PrefetchScalarGridSpec(
            num_scalar_prefetch=1, grid=(dec_batch,), in_specs=in_specs,
            out_specs=pl.BlockSpec((MLA_HEADS, dec_seq, KV_LORA), lambda b, pt: (0, b, 0)),
            scratch_shapes=[pltpu.VMEM((2, n_pages, PAGE_SIZE, KV_LORA), F32),
                            pltpu.VMEM((2, n_pages, QK_ROPE, PAGE_SIZE), F32),
                            pltpu.SemaphoreType.DMA((2, 2)),
                            pltpu.VMEM((n_chunks, chunk_keys, KV_LORA), BF16),
                            pltpu.VMEM((n_chunks, rows, chunk_keys), F32),
                            pltpu.VMEM((n_chunks, rows, LANES), F32),
                            pltpu.VMEM((n_chunks, rows, LANES), F32),
                            pltpu.VMEM((n_chunks, rows, KV_LORA), F32)]),
        out_shape=jax.ShapeDtypeStruct((MLA_HEADS, dec_batch * dec_seq, KV_LORA), F32),
        compiler_params=pltpu.CompilerParams(
            dimension_semantics=("arbitrary",), vmem_limit_bytes=VMEM_LIMIT),
        name="mla_sample",
    )(page_table, qf, new_ckv, new_kr, cache_ckv, cache_krope_t)


def _uv_kernel(o_lat_ref, wuv_ref, o_ref):
    o = _dot(o_lat_ref[0].astype(BF16), wuv_ref[0])
    for h in range(1, MLA_HEADS):
        o = o + _dot(o_lat_ref[h].astype(BF16), wuv_ref[h])
    o_ref[...] = o


def _uv_sample(o_lat, w_uv):
    n = o_lat.shape[1]
    return pl.pallas_call(
        _uv_kernel,
        out_shape=jax.ShapeDtypeStruct((n, BRANCH_WIDTH), F32),
        compiler_params=pltpu.CompilerParams(vmem_limit_bytes=VMEM_LIMIT),
        name="uv_sample",
    )(o_lat, w_uv)


def _moba_sample_kernel(pt_ref, q_ref, nk_ref, nv_ref, slope_ref, kt_hbm, vt_hbm, o_ref,
                        kt_buf, vt_buf, sems, s_ref, m_ref, l_ref, acc_ref,
                        *, layer, dec_seq, n_pages, past_len):
    slot = _page_pipeline(pt_ref, layer, n_pages, (kt_hbm, vt_hbm), (kt_buf, vt_buf), sems)
    rows = MOBA_HEADS * dec_seq
    nb_past = n_pages // PAGES_PER_BLOCK
    wide = (rows, LANES)
    qf = q_ref[...].reshape(rows, MOBA_KV_WIDTH)
    q, scale = _prescaled_query(qf, MOBA_HEAD_DIM ** -0.5)
    slope = slope_ref[...]
    t_q = past_len + lax.broadcasted_iota(jnp.int32, (MOBA_HEADS, dec_seq, LANES), 1).reshape(wide)
    off_k = lax.broadcasted_iota(jnp.int32, (1, MOBA_BLOCK), 1).astype(F32)
    alibi_off = slope[:, :1] * off_k
    lane = lax.broadcasted_iota(jnp.int32, wide, 1)
    lane_km = lax.broadcasted_iota(jnp.int32, (MOBA_KV_WIDTH, LANES), 1)

    def block_t(buf, j):
        return jnp.concatenate([buf[slot, PAGES_PER_BLOCK * j + i] for i in range(PAGES_PER_BLOCK)], axis=1)

    km = jnp.zeros((MOBA_KV_WIDTH, LANES), F32)
    for j in range(nb_past):
        kt = block_t(kt_buf, j)
        km = jnp.where(lane_km == j, jnp.sum(kt, axis=1, keepdims=True) / MOBA_BLOCK, km)
        s_ref[j] = _dot(q, kt.astype(BF16))

    gate = _gate_scores(qf, km, True)
    sel = _topk_blocks(gate, lane < nb_past, min(MOBA_TOPK, nb_past), 1)

    def softmax(j):
        shift = -slope * (t_q - j * MOBA_BLOCK).astype(F32)
        raw = s_ref[j]
        s = [sl + shift for sl in _lane_slabs(_times(raw, scale) + alibi_off)]
        mj = _row_max(functools.reduce(jnp.maximum, s))
        ps = [jnp.exp(sl - mj) for sl in s]
        m_ref[j] = mj
        l_ref[j] = jnp.broadcast_to(jnp.sum(functools.reduce(operator.add, ps), axis=1, keepdims=True), wide)
        return jnp.concatenate(ps, axis=1).astype(BF16)

    nxt = softmax(0)
    for j in range(nb_past):
        p = nxt
        if j + 1 < nb_past:
            nxt = softmax(j + 1)
        acc_ref[j] = _dot_nt(block_t(vt_buf, j).astype(BF16), p)

    kn = nk_ref[...].astype(BF16)
    t_k = lax.broadcasted_iota(jnp.int32, (1, dec_seq), 1)
    dist = (t_q[:, :1] - past_len) - t_k
    s = _times(_dot_nt(q, kn), scale) - slope[:, :1] * dist.astype(F32)
    s = jnp.where(dist >= 0, s, NEG_INF)
    m_own = jnp.max(s, axis=1, keepdims=True)
    p = jnp.exp(s - m_own)
    l_own = jnp.sum(p, axis=1, keepdims=True)
    acc_own = _dot(p.astype(BF16), nv_ref[...].astype(BF16))

    m_blk = jnp.zeros(wide, F32)
    l_blk = jnp.zeros(wide, F32)
    for j in range(nb_past):
        m_blk = jnp.where(lane == j, m_ref[j], m_blk)
        l_blk = jnp.where(lane == j, l_ref[j], l_blk)
    m_tot = jnp.maximum(m_own, jnp.max(jnp.where(sel, m_blk, NEG_INF), axis=1, keepdims=True))
    w_blk = jnp.where(sel, jnp.exp(m_blk - m_tot), 0.0)
    w_own = jnp.exp(m_own - m_tot)
    l_tot = jnp.sum(w_blk * l_blk, axis=1, keepdims=True) + w_own * l_own
    w_t = jnp.concatenate([w_blk, jnp.zeros((LANES - rows, LANES), F32)], axis=0).T[:, :rows]
    acc_t = jnp.zeros((MOBA_KV_WIDTH, rows), F32)
    for j in range(nb_past):
        acc_t = acc_t + w_t[j:j + 1, :] * acc_ref[j]
    acc_t = jnp.concatenate([acc_t, jnp.zeros((MOBA_KV_WIDTH, LANES - rows), F32)], axis=1)
    o = (acc_t.T[:rows, :] + w_own * acc_own) / l_tot
    pieces = []
    for h in range(MOBA_HEADS):
        g = h // MOBA_GROUP
        pieces.append(o[h * dec_seq:(h + 1) * dec_seq, g * MOBA_HEAD_DIM:(g + 1) * MOBA_HEAD_DIM])
    o_ref[...] = jnp.concatenate(pieces, axis=1)


def _moba_sample(page_table, qm, new_k, new_v, slope_rows, cache_kt, cache_vt, layer, *, dec_batch, dec_seq):
    n_pages = page_table.shape[1]
    past_len = n_pages * PAGE_SIZE
    nb_past = past_len // MOBA_BLOCK
    rows = MOBA_HEADS * dec_seq
    in_specs = [
        pl.BlockSpec((MOBA_HEADS, dec_seq, MOBA_KV_WIDTH), lambda b, pt: (0, b, 0)),
        pl.BlockSpec((dec_seq, MOBA_KV_WIDTH), lambda b, pt: (b, 0)),
        pl.BlockSpec((dec_seq, MOBA_KV_WIDTH), lambda b, pt: (b, 0)),
        pl.BlockSpec((rows, LANES), lambda b, pt: (0, 0)),
        pl.BlockSpec(memory_space=pl.ANY),
        pl.BlockSpec(memory_space=pl.ANY),
    ]
    page_buf = pltpu.VMEM((2, n_pages, MOBA_KV_WIDTH, PAGE_SIZE), F32)
    stat = pltpu.VMEM((nb_past, rows, LANES), F32)
    return pl.pallas_call(
        functools.partial(_moba_sample_kernel, layer=layer, dec_seq=dec_seq, n_pages=n_pages,
                          past_len=past_len),
        grid_spec=pltpu.PrefetchScalarGridSpec(
            num_scalar_prefetch=1, grid=(dec_batch,), in_specs=in_specs,
            out_specs=pl.BlockSpec((dec_seq, BRANCH_WIDTH), lambda b, pt: (b, 0)),
            scratch_shapes=[page_buf, page_buf, pltpu.SemaphoreType.DMA((2, 2)),
                            pltpu.VMEM((nb_past, rows, MOBA_BLOCK), F32), stat, stat,
                            pltpu.VMEM((nb_past, MOBA_KV_WIDTH, rows), F32)]),
        out_shape=jax.ShapeDtypeStruct((dec_batch * dec_seq, BRANCH_WIDTH), F32),
        compiler_params=pltpu.CompilerParams(
            dimension_semantics=("arbitrary",), vmem_limit_bytes=VMEM_LIMIT),
        name="moba_sample",
    )(page_table, qm, new_k, new_v, slope_rows, cache_kt, cache_vt)


def _ffn_kernel(x_ref, oa_ref, ob_ref, g_attn_ref, wgate_ref, wbr_ref, wo_ref, g_ffn_ref, wgu_ref, wdn_ref,
                g_fin_ref, out_ref, *, final, n_chunks):
    x = x_ref[...]
    xn = _rms(x, g_attn_ref[...]).astype(BF16)
    merged = (jax.nn.sigmoid(_dot(xn, wgate_ref[:, :D_MODEL])) * _dot(oa_ref[...].astype(BF16), wbr_ref[0])
              + jax.nn.sigmoid(_dot(xn, wgate_ref[:, D_MODEL:])) * _dot(ob_ref[...].astype(BF16), wbr_ref[1]))
    h = x + _dot(merged.astype(BF16), wo_ref[...])
    hn = _rms(h, g_ffn_ref[...]).astype(BF16)
    cw = FFN_HIDDEN // n_chunks
    acc = h
    for c in range(n_chunks):
        a = _dot(hn, wgu_ref[:, c * cw:(c + 1) * cw])
        u = _dot(hn, wgu_ref[:, FFN_HIDDEN + c * cw:FFN_HIDDEN + (c + 1) * cw])
        act = (jax.nn.silu(a) * u).astype(BF16)
        acc = acc + _dot(act, wdn_ref[c * cw:(c + 1) * cw, :])
    out_ref[...] = _rms(acc, g_fin_ref[...]) if final else acc


def _ffn(x, o_a, o_b, w, stacked, layer, g_final, *, final, tm, n_chunks=2):
    n = x.shape[0]
    row = lambda i: (i, 0)

    def layer_spec(shape):
        return pl.BlockSpec((None,) + shape, lambda *_: (layer,) + (0,) * len(shape),
                            pipeline_mode=pl.Buffered(1))

    return pl.pallas_call(
        functools.partial(_ffn_kernel, final=final, n_chunks=n_chunks),
        grid=(n // tm,),
        in_specs=[
            pl.BlockSpec((tm, D_MODEL), row),
            pl.BlockSpec((tm, BRANCH_WIDTH), row),
            pl.BlockSpec((tm, BRANCH_WIDTH), row),
            _const_spec((1, D_MODEL)),
            _const_spec((D_MODEL, 2 * D_MODEL)),
            layer_spec((2, BRANCH_WIDTH, D_MODEL)),
            layer_spec((D_MODEL, D_MODEL)),
            _const_spec((1, D_MODEL)),
            layer_spec((D_MODEL, 2 * FFN_HIDDEN)),
            layer_spec((FFN_HIDDEN, D_MODEL)),
            _const_spec((1, D_MODEL)),
        ],
        out_specs=pl.BlockSpec((tm, D_MODEL), row),
        out_shape=jax.ShapeDtypeStruct((n, D_MODEL), F32),
        compiler_params=pltpu.CompilerParams(
            dimension_semantics=("arbitrary",), vmem_limit_bytes=VMEM_LIMIT),
        name="ffn_final" if final else "ffn",
    )(x, o_a, o_b, w["g_attn"], w["w_gate"], stacked["w_branch"], stacked["w_o"], w["g_ffn"],
      stacked["w_gu"], stacked["w_down"], g_final)


def _prep_layer_weights(w_in, g_q, g_kv, w_uq, w_uk, w_uv, g_attn, g_ffn):
    s = [0, Q_LORA, Q_LORA + KV_LORA, Q_LORA + KV_LORA + QK_ROPE]
    s.append(s[-1] + MOBA_HEADS * MOBA_HEAD_DIM)
    s.append(s[-1] + MOBA_KV_WIDTH)
    s.append(s[-1] + MOBA_KV_WIDTH)
    c_q, c_kv, k_r, m_q, m_k, m_v, gate = (w_in[:, s[0]:s[1]], w_in[:, s[1]:s[2]], w_in[:, s[2]:s[3]],
                                           w_in[:, s[3]:s[4]], w_in[:, s[4]:s[5]], w_in[:, s[5]:s[6]],
                                           w_in[:, s[6]:])
    m_q = m_q.reshape(D_MODEL, MOBA_KV_HEADS, MOBA_GROUP, MOBA_HEAD_DIM)
    m_q_paired = jnp.transpose(m_q, (0, 2, 1, 3)).reshape(D_MODEL, MOBA_HEADS * MOBA_HEAD_DIM)
    unused = jnp.zeros((D_MODEL, _OFF_CKV - _OFF_KR - QK_ROPE), w_in.dtype)
    w_in_perm = jnp.concatenate([c_q, k_r, unused, c_kv, m_q_paired, m_k, m_v], axis=1).astype(BF16)

    uq = w_uq.reshape(Q_LORA, MLA_HEADS, QK_NOPE + QK_ROPE)
    w_uqn = jnp.pad(uq[:, :, :QK_NOPE], ((0, 0), (0, 0), (0, LANES - QK_NOPE))).reshape(Q_LORA, MLA_HEADS * LANES)
    w_uqr = jnp.concatenate([uq[:, :, QK_NOPE:QK_NOPE + HALF_ROPE].reshape(Q_LORA, LANES),
                             uq[:, :, QK_NOPE + HALF_ROPE:].reshape(Q_LORA, LANES)], axis=1)
    uk = jnp.transpose(w_uk, (1, 2, 0))
    uk = jnp.pad(uk, ((0, 0), (0, LANES - QK_NOPE), (0, 0)))
    uv = jnp.transpose(w_uv, (1, 0, 2))
    eye = jnp.eye(MLA_HEADS, dtype=w_uv.dtype)
    uv_pad = (uv[:, :, None, :] * eye[:, None, :, None]).reshape(MLA_HEADS, KV_LORA, BRANCH_WIDTH)
    return {
        "w_in": w_in_perm, "w_gate": gate.astype(BF16), "g_q": g_q[None], "g_kv": g_kv[None],
        "g_attn": g_attn[None], "g_ffn": g_ffn[None],
        "w_uqn": w_uqn.astype(BF16), "w_uqr": w_uqr.astype(BF16), "w_uk": uk.astype(BF16),
        "w_uv": uv_pad.astype(BF16),
    }


def _rope_tables(pos):
    inv = ROPE_THETA ** (-jnp.arange(HALF_ROPE, dtype=F32) / HALF_ROPE)
    ang = pos.astype(F32)[:, None] * inv[None, :]
    reps = LANES // HALF_ROPE
    return jnp.tile(jnp.cos(ang), (1, reps)), jnp.tile(jnp.sin(ang), (1, reps))


def _slope_rows(tokens_per_head, width):
    slopes = 2.0 ** (-8.0 * jnp.arange(1, MOBA_HEADS + 1, dtype=F32) / MOBA_HEADS)
    return jnp.broadcast_to(jnp.repeat(slopes, tokens_per_head)[:, None], (MOBA_HEADS * tokens_per_head, width))


def kernel(x_prompt, x_sample, cache_ckv, cache_krope, cache_k, cache_v, page_table, w_in, g_q, g_kv, w_uq, w_uk, w_uv, w_branch, w_o, g_attn, g_ffn, w_gu, w_down, g_final):
    batch, seq, _ = x_prompt.shape
    dec_batch, dec_seq, _ = x_sample.shape
    depth, n_phys = cache_k.shape[:2]
    past_len = page_table.shape[1] * PAGE_SIZE
    n_p, n_s = batch * seq, dec_batch * dec_seq
    tm_p = min(512, seq)
    tm_s = min(512, n_s)

    cache_krope_t = jnp.transpose(cache_krope, (0, 1, 3, 2))
    cache_kt = jnp.transpose(cache_k, (0, 1, 3, 4, 2)).reshape(depth, n_phys, MOBA_KV_WIDTH, PAGE_SIZE)
    cache_vt = jnp.transpose(cache_v, (0, 1, 3, 4, 2)).reshape(depth, n_phys, MOBA_KV_WIDTH, PAGE_SIZE)

    cos_p, sin_p = _rope_tables(jnp.arange(seq, dtype=jnp.int32))
    pos_s = past_len + jnp.arange(dec_seq, dtype=jnp.int32)
    cos_s, sin_s = _rope_tables(jnp.tile(pos_s, tm_s // dec_seq))
    slope_p = _slope_rows(MOBA_BLOCK, 1)
    slope_s = _slope_rows(dec_seq, LANES)
    g_fin = g_final[None]
    stacked = {"w_branch": w_branch.astype(BF16), "w_o": w_o.astype(BF16),
               "w_gu": w_gu.astype(BF16), "w_down": w_down.astype(BF16)}

    hp = x_prompt.reshape(n_p, D_MODEL)
    hs = x_sample.reshape(n_s, D_MODEL)
    rows_p, rows_s = [], []
    for l in range(depth):
        w = _prep_layer_weights(w_in[l], g_q[l], g_kv[l], w_uq[l], w_uk[l], w_uv[l], g_attn[l], g_ffn[l])
        final = l == depth - 1

        ckv, kr, mk, mv, qf, qm, kf, mkv, kmean = _proj(hp, cos_p, sin_p, w, prompt=True, tm=tm_p)
        o_a = _mla_prompt(qf, kf, w["w_uv"], batch=batch, seq=seq, tq=min(256, seq), tk=min(256, seq))
        o_b = _moba_prompt(qm, mkv, kmean, slope_p, batch=batch, seq=seq)
        hp = _ffn(hp, o_a, o_b, w, stacked, l, g_fin, final=final, tm=tm_p)
        rows_p.append((ckv, kr, mk, mv))

        ckv, kr, mk, mv, qf, qm = _proj(hs, cos_s, sin_s, w, prompt=False, tm=tm_s)
        o_lat = _mla_sample(page_table, qf, ckv, kr, cache_ckv, cache_krope_t, l,
                            dec_batch=dec_batch, dec_seq=dec_seq)
        o_a = _uv_sample(o_lat, w["w_uv"])
        o_b = _moba_sample(page_table, qm, mk, mv, slope_s, cache_kt, cache_vt, l,
                           dec_batch=dec_batch, dec_seq=dec_seq)
        hs = _ffn(hs, o_a, o_b, w, stacked, l, g_fin, final=final, tm=tm_s)
        rows_s.append((ckv, kr, mk, mv))

    def stack(rows, i, shape):
        return jnp.stack([r[i] for r in rows]).reshape((depth,) + shape)

    kv_shape = (MOBA_KV_HEADS, MOBA_HEAD_DIM)
    return (hp.reshape(batch, seq, D_MODEL),
            hs.reshape(dec_batch, dec_seq, D_MODEL),
            stack(rows_p, 0, (batch, seq, KV_LORA)),
            stack(rows_p, 1, (batch, seq, QK_ROPE)),
            stack(rows_p, 2, (batch, seq) + kv_shape),
            stack(rows_p, 3, (batch, seq) + kv_shape),
            stack(rows_s, 0, (dec_batch, dec_seq, KV_LORA)),
            stack(rows_s, 1, (dec_batch, dec_seq, QK_ROPE)),
            stack(rows_s, 2, (dec_batch, dec_seq) + kv_shape),
            stack(rows_s, 3, (dec_batch, dec_seq) + kv_shape))
```
